```python
import jax, jax.numpy as jnp
from jax import lax
import numpy as np

D_MODEL = 1024
BATCH = 4
SEQ = 4096
DEPTH = 2
DEC_BATCH = 8
DEC_SEQ = 4096
PAST_LEN = 128

HEAD_DIM = 64
N_Q_HEADS = 8
N_KV_HEADS = 2
Q_PER_KV = N_Q_HEADS // N_KV_HEADS
ATTN_WIDTH = N_Q_HEADS * HEAD_DIM
KV_WIDTH = N_KV_HEADS * HEAD_DIM
FOURIER_WIDTH = D_MODEL - ATTN_WIDTH
N_FOURIER_GROUPS = 4
FOURIER_GROUP_DIM = FOURIER_WIDTH // N_FOURIER_GROUPS
MIX_WIDTH = ATTN_WIDTH + FOURIER_WIDTH
IN_WIDTH = ATTN_WIDTH + 2 * KV_WIDTH + ATTN_WIDTH + FOURIER_WIDTH + FOURIER_WIDTH
IN_SPLITS = (ATTN_WIDTH,
             ATTN_WIDTH + KV_WIDTH,
             ATTN_WIDTH + 2 * KV_WIDTH,
             2 * ATTN_WIDTH + 2 * KV_WIDTH,
             2 * ATTN_WIDTH + 2 * KV_WIDTH + FOURIER_WIDTH)
WINDOW = 128
BLOCK = 128
ROPE_THETA = 10000.0
EPS = 1e-6
NEG = -1e30

kernel_name = 'hymba_window_sink_gqa_fnet_encoder'


def rms_norm(x, gain):
    xf = x.astype(jnp.float32)
    xf = xf * lax.rsqrt(jnp.mean(xf * xf, axis=-1, keepdims=True) + EPS)
    return (xf * gain.astype(jnp.float32)).astype(x.dtype)


def rope(t):
    S = t.shape[1]
    half = HEAD_DIM // 2
    inv_freq = 1.0 / (ROPE_THETA ** (jnp.arange(half, dtype=jnp.float32) / half))
    ang = jnp.arange(S, dtype=jnp.float32)[:, None] * inv_freq[None, :]
    cos = jnp.cos(ang)[None, :, None, :]
    sin = jnp.sin(ang)[None, :, None, :]
    t1, t2 = t[..., :half], t[..., half:]
    return jnp.concatenate([t1 * cos - t2 * sin, t2 * cos + t1 * sin], axis=-1)


def banded_sink_attention(q, k, v, sink):
    B, S = q.shape[0], q.shape[1]
    nb = S // BLOCK
    qb = q.reshape(B, nb, BLOCK, N_KV_HEADS, Q_PER_KV, HEAD_DIM)
    pad = ((0, 0), (BLOCK, BLOCK), (0, 0), (0, 0))

    def bands(t):
        tb = jnp.pad(t, pad).reshape(B, nb + 2, BLOCK, N_KV_HEADS, HEAD_DIM)
        return jnp.concatenate([tb[:, :-2], tb[:, 1:-1], tb[:, 2:]], axis=2)

    kb, vb = bands(k), bands(v)
    s = jnp.einsum('bnqkgd,bnmkd->bnkgqm', qb, kb) * (HEAD_DIM ** -0.5)
    qi = jnp.arange(BLOCK)[:, None]
    mi = jnp.arange(3 * BLOCK)[None, :]
    rel = mi - BLOCK - qi
    key_pos = jnp.arange(nb)[:, None, None] * BLOCK + mi[None] - BLOCK
    valid = (jnp.abs(rel) <= WINDOW)[None] & (key_pos >= 0) & (key_pos < S)
    s = jnp.where(valid[None, :, None, None], s, NEG)
    sink_l = sink.astype(jnp.float32).reshape(N_KV_HEADS, Q_PER_KV)[None, None, :, :, None, None]
    m = jnp.maximum(jnp.max(s, axis=-1, keepdims=True), sink_l)
    p = jnp.exp(s - m)
    denom = jnp.sum(p, axis=-1, keepdims=True) + jnp.exp(sink_l - m)
    o = jnp.einsum('bnkgqm,bnmkd->bnqkgd', p / denom, vb)
    return o.reshape(B, S, ATTN_WIDTH)


def fourier_mix(u, w_lin):
    B, S = u.shape[0], u.shape[1]
    ug = u.astype(jnp.float32).reshape(B, S, N_FOURIER_GROUPS, FOURIER_GROUP_DIM)
    f = jnp.real(jnp.fft.fft2(ug, axes=(1, 3), norm='ortho'))
    out = jnp.einsum('bsgc,gcd->bsgd', f, w_lin.astype(jnp.float32))
    return out.reshape(B, S, FOURIER_WIDTH)


def hybrid_layer(x, g_norm, w_in, q_gain, k_gain, sink, w_four, w_out):
    B, S, _ = x.shape
    h = rms_norm(x, g_norm)
    z = h @ w_in
    q, k, v, g_attn, u, g_four = jnp.split(z, IN_SPLITS, axis=-1)
    q = rms_norm(q.reshape(B, S, N_Q_HEADS, HEAD_DIM), q_gain).astype(jnp.float32)
    k = rms_norm(k.reshape(B, S, N_KV_HEADS, HEAD_DIM), k_gain).astype(jnp.float32)
    v = v.reshape(B, S, N_KV_HEADS, HEAD_DIM).astype(jnp.float32)
    attn = banded_sink_attention(rope(q), rope(k), v, sink).astype(x.dtype)
    four = fourier_mix(u, w_four).astype(x.dtype)
    mixed = jnp.concatenate([jax.nn.silu(g_attn) * attn, jax.nn.silu(g_four) * four], axis=-1)
    return x + mixed @ w_out


def setup_inputs(seed: int = 0) -> dict:
    key = jax.random.key(seed)
    ks = jax.random.split(key, 9)
    f32 = jnp.float32
    x_prompt = jax.random.normal(ks[0], (BATCH, SEQ, D_MODEL), f32)
    x_sample = jax.random.normal(ks[1], (DEC_BATCH, DEC_SEQ, D_MODEL), f32)
    norm_gain = 1.0 + 0.02 * jax.random.normal(ks[2], (DEPTH, D_MODEL), f32)
    w_in = jax.random.normal(ks[3], (DEPTH, D_MODEL, IN_WIDTH), f32) * D_MODEL ** -0.5
    q_norm_gain = 1.0 + 0.02 * jax.random.normal(ks[4], (DEPTH, HEAD_DIM), f32)
    k_norm_gain = 1.0 + 0.02 * jax.random.normal(ks[5], (DEPTH, HEAD_DIM), f32)
    sink_logit = 0.5 * jax.random.normal(ks[6], (DEPTH, N_Q_HEADS), f32)
    w_fourier = jax.random.normal(ks[7], (DEPTH, N_FOURIER_GROUPS, FOURIER_GROUP_DIM, FOURIER_GROUP_DIM), f32) * FOURIER_GROUP_DIM ** -0.5
    w_out = jax.random.normal(ks[8], (DEPTH, MIX_WIDTH, D_MODEL), f32) * MIX_WIDTH ** -0.5
    return {'x_prompt': x_prompt, 'x_sample': x_sample, 'norm_gain': norm_gain, 'w_in': w_in,
            'q_norm_gain': q_norm_gain, 'k_norm_gain': k_norm_gain, 'sink_logit': sink_logit,
            'w_fourier': w_fourier, 'w_out': w_out}


def reference(x_prompt, x_sample, norm_gain, w_in, q_norm_gain, k_norm_gain, sink_logit, w_fourier, w_out):
    y_prompt = x_prompt
    y_sample = x_sample
    for l in range(DEPTH):
        params = (norm_gain[l], w_in[l], q_norm_gain[l], k_norm_gain[l], sink_logit[l], w_fourier[l], w_out[l])
        y_prompt = hybrid_layer(y_prompt, *params)
        y_sample = hybrid_layer(y_sample, *params)
    return (y_prompt, y_sample)
```

```python
import functools
import math

import jax
import jax.numpy as jnp
from jax.experimental import pallas as pl
from jax.experimental.pallas import tpu as pltpu

D_MODEL = 1024
HEAD_DIM = 64
N_Q_HEADS = 8
N_KV_HEADS = 2
ATTN_WIDTH = N_Q_HEADS * HEAD_DIM
KV_WIDTH = N_KV_HEADS * HEAD_DIM
FOURIER_WIDTH = D_MODEL - ATTN_WIDTH
N_GROUPS = 4
GROUP_DIM = FOURIER_WIDTH // N_GROUPS
IN_WIDTH = 2 * ATTN_WIDTH + 2 * KV_WIDTH + 2 * FOURIER_WIDTH
Q_OFF = 0
K_OFF = ATTN_WIDTH
V_OFF = K_OFF + KV_WIDTH
GA_OFF = V_OFF + KV_WIDTH
U_OFF = GA_OFF + ATTN_WIDTH
GF_OFF = U_OFF + FOURIER_WIDTH
BLOCK = 128
ROPE_THETA = 10000.0
EPS = 1e-6
NEG = -1e30
LANES = 128
RADIX = 4

ROW_TILE = 512
ATTN_TILE = 512
VMEM_LIMIT = 48 * 1024 * 1024

BF16 = jnp.bfloat16
F32 = jnp.float32


def _dot(a, b):
    return jnp.dot(a, b, preferred_element_type=F32)


def _dot_nt(a, b):
    return jax.lax.dot_general(a, b, (((1,), (1,)), ((), ())), preferred_element_type=F32)


def _silu(x):
    return x / (1.0 + jnp.exp(-x))


def _rotate_half(t, first_half):
    width = t.shape[-1]
    fwd = pltpu.roll(t, HEAD_DIM // 2, axis=1)
    bwd = pltpu.roll(t, width - HEAD_DIM // 2, axis=1)
    return jnp.where(first_half, bwd, fwd)


def _fourier_weight_kernel(cs_ref, w_ref, o_ref):
    for g in range(N_GROUPS):
        w = w_ref[g]
        c = jnp.dot(cs_ref[0], w, preferred_element_type=F32, precision=jax.lax.Precision.HIGHEST)
        s = jnp.dot(cs_ref[1], w, preferred_element_type=F32, precision=jax.lax.Precision.HIGHEST)
        o_ref[g] = jnp.concatenate([c, s], axis=1).astype(o_ref.dtype)


def _fourier_weights(w_four, seq_len):
    idx = (jnp.arange(GROUP_DIM, dtype=jnp.int32)[:, None] * jnp.arange(GROUP_DIM, dtype=jnp.int32)[None, :]) % GROUP_DIM
    ang = idx.astype(F32) * (2.0 * math.pi / GROUP_DIM)
    scale = 1.0 / math.sqrt(seq_len * GROUP_DIM)
    cs = jnp.stack([jnp.cos(ang), jnp.sin(ang)]) * scale
    return pl.pallas_call(
        _fourier_weight_kernel,
        out_shape=jax.ShapeDtypeStruct((N_GROUPS, GROUP_DIM, 2 * GROUP_DIM), BF16),
        name="fourier_weight_prep",
    )(cs, w_four)


def _in_proj_kernel(x_ref, gn_ref, w_ref, qg_ref, kg_ref, cos_ref, sin_ref, gq_ref, gk_ref, wab_ref,
                    q_ref, ka_ref, kb_ref, va_ref, vb_ref, sga_ref, a_ref, b_ref, sgf_ref):
    x = x_ref[...]
    ms = jnp.mean(x * x, axis=-1, keepdims=True)
    h = (x * jax.lax.rsqrt(ms + EPS) * gn_ref[...]).astype(BF16)

    cos = cos_ref[...]
    sin = sin_ref[...]
    lane = jax.lax.broadcasted_iota(jnp.int32, (1, LANES), 1)
    first_half_pair = (lane % HEAD_DIM) < (HEAD_DIM // 2)
    low_head = lane < HEAD_DIM

    zq = _dot(h, w_ref[:, Q_OFF:Q_OFF + ATTN_WIDTH])
    ssq = _dot((zq * zq).astype(BF16), gq_ref[...]) * (1.0 / HEAD_DIM)
    qn = zq * jax.lax.rsqrt(ssq + EPS) * qg_ref[...]
    n_pairs = ATTN_WIDTH // LANES
    cos_q = jnp.concatenate([cos] * n_pairs, axis=1)
    sin_q = jnp.concatenate([sin] * n_pairs, axis=1)
    fh_q = jnp.concatenate([first_half_pair] * n_pairs, axis=1)
    qr = qn * cos_q + _rotate_half(qn, fh_q) * sin_q
    q_ref[...] = (qr * (HEAD_DIM ** -0.5)).astype(q_ref.dtype)

    zk = _dot(h, w_ref[:, K_OFF:K_OFF + KV_WIDTH])
    ssk = _dot((zk * zk).astype(BF16), gk_ref[...]) * (1.0 / HEAD_DIM)
    kn = zk * jax.lax.rsqrt(ssk + EPS) * kg_ref[...]
    kr = kn * cos + _rotate_half(kn, first_half_pair) * sin
    kr_sw = pltpu.roll(kr, HEAD_DIM, axis=1)
    ka_ref[...] = jnp.where(low_head, kr, kr_sw).astype(ka_ref.dtype)
    kb_ref[...] = jnp.where(low_head, kr_sw, kr).astype(kb_ref.dtype)

    zv = _dot(h, w_ref[:, V_OFF:V_OFF + KV_WIDTH])
    zv_sw = pltpu.roll(zv, HEAD_DIM, axis=1)
    va_ref[...] = jnp.where(low_head, zv, zv_sw).astype(va_ref.dtype)
    vb_ref[...] = jnp.where(low_head, zv_sw, zv).astype(vb_ref.dtype)

    sga_ref[...] = _silu(_dot(h, w_ref[:, GA_OFF:GA_OFF + ATTN_WIDTH])).astype(sga_ref.dtype)
    sgf_ref[...] = _silu(_dot(h, w_ref[:, GF_OFF:GF_OFF + FOURIER_WIDTH])).astype(sgf_ref.dtype)

    zu = _dot(h, w_ref[:, U_OFF:U_OFF + FOURIER_WIDTH]).astype(BF16)
    for g in range(N_GROUPS):
        ab = _dot(zu[:, g * GROUP_DIM:(g + 1) * GROUP_DIM], wab_ref[g])
        a_ref[:, g * GROUP_DIM:(g + 1) * GROUP_DIM] = ab[:, :GROUP_DIM].astype(a_ref.dtype)
        b_ref[:, g * GROUP_DIM:(g + 1) * GROUP_DIM] = ab[:, GROUP_DIM:].astype(b_ref.dtype)


def _in_proj(x2d, seq_len, gn, w_bf, qg, kg, cos_t, sin_t, gq, gk, wab):
    rows = x2d.shape[0]
    tm = ROW_TILE
    steps_per_seq = seq_len // tm
    row_spec = lambda width: pl.BlockSpec((tm, width), lambda i: (i, 0))
    full = lambda shape: pl.BlockSpec(shape, lambda i: (0,) * len(shape))
    tab_spec = pl.BlockSpec((tm, LANES), lambda i: (i % steps_per_seq, 0))
    out_widths = (ATTN_WIDTH, LANES, LANES, LANES, LANES, ATTN_WIDTH, FOURIER_WIDTH, FOURIER_WIDTH, FOURIER_WIDTH)
    return pl.pallas_call(
        _in_proj_kernel,
        grid=(rows // tm,),
        in_specs=[row_spec(D_MODEL), full((1, D_MODEL)), full((D_MODEL, IN_WIDTH)),
                  full((1, ATTN_WIDTH)), full((1, KV_WIDTH)), tab_spec, tab_spec,
                  full((ATTN_WIDTH, ATTN_WIDTH)), full((KV_WIDTH, KV_WIDTH)),
                  full((N_GROUPS, GROUP_DIM, 2 * GROUP_DIM))],
        out_specs=[row_spec(w) for w in out_widths],
        out_shape=[jax.ShapeDtypeStruct((rows, w), BF16) for w in out_widths],
        compiler_params=pltpu.CompilerParams(dimension_semantics=("arbitrary",), vmem_limit_bytes=VMEM_LIMIT),
        name="in_proj",
    )(x2d, gn, w_bf, qg, kg, cos_t, sin_t, gq, gk, wab)


def _attn_kernel(sink_ref, q_ref, ka_ref, kb_ref, va_ref, vb_ref, sg_ref, o_ref, *, n_blocks):
    i = pl.program_id(1)
    blocks_per_step = ATTN_TILE // BLOCK
    row = jax.lax.broadcasted_iota(jnp.int32, (BLOCK, 2 * BLOCK), 0)
    col = jax.lax.broadcasted_iota(jnp.int32, (BLOCK, 2 * BLOCK), 1) % BLOCK
    lane = jax.lax.broadcasted_iota(jnp.int32, (1, LANES), 1)
    low = lane < HEAD_DIM
    zero = jnp.zeros((), BF16)
    one = jnp.ones((), BF16)

    for jb in range(blocks_per_step):
        ib = i * blocks_per_step + jb
        rows = slice(jb * BLOCK, (jb + 1) * BLOCK)
        bias_prev = jnp.where(col >= row, 0.0, NEG) + jnp.where(ib == 0, NEG, 0.0)
        bias_next = jnp.where(col <= row, 0.0, NEG) + jnp.where(ib == n_blocks - 1, NEG, 0.0)
        starts = [pl.multiple_of(jnp.clip(ib + c, 0, n_blocks - 1) * BLOCK, BLOCK) for c in (-1, 0, 1)]
        for kvh, (k_ref, v_ref) in enumerate(((ka_ref, va_ref), (kb_ref, vb_ref))):
            kbd, v_even, v_odd = [], [], []
            for st in starts:
                kblk = k_ref[0, pl.ds(st, BLOCK), :]
                vblk = v_ref[0, pl.ds(st, BLOCK), :]
                kbd.append(jnp.concatenate([jnp.where(low, kblk, zero), jnp.where(low, zero, kblk)], axis=0))
                v_even.append(jnp.where(low, vblk, one))
                v_odd.append(jnp.where(low, one, vblk))
            v_even = jnp.concatenate(v_even, axis=0)
            v_odd = jnp.concatenate(v_odd, axis=0)
            for pair in range(2):
                pidx = kvh * 2 + pair
                lanes = slice(pidx * LANES, (pidx + 1) * LANES)
                qp = q_ref[0, rows, lanes]
                s_prev = _dot_nt(qp, kbd[0]) + bias_prev
                s_mid = _dot_nt(qp, kbd[1])
                s_next = _dot_nt(qp, kbd[2]) + bias_next
                smax = jnp.maximum(jnp.maximum(s_prev, s_mid), s_next)
                sink_e = sink_ref[2 * pidx]
                sink_o = sink_ref[2 * pidx + 1]
                m_e = jnp.maximum(jnp.max(smax[:, :BLOCK], axis=-1, keepdims=True), sink_e)
                m_o = jnp.maximum(jnp.max(smax[:, BLOCK:], axis=-1, keepdims=True), sink_o)
                p_e = jnp.concatenate([jnp.exp(s[:, :BLOCK] - m_e) for s in (s_prev, s_mid, s_next)], axis=1)
                p_o = jnp.concatenate([jnp.exp(s[:, BLOCK:] - m_o) for s in (s_prev, s_mid, s_next)], axis=1)
                acc_e = _dot(p_e.astype(BF16), v_even)
                acc_o = _dot(p_o.astype(BF16), v_odd)
                num = jnp.where(low, acc_e, acc_o)
                den = pltpu.roll(jnp.where(low, acc_o, acc_e), HEAD_DIM, axis=1)
                den = den + jnp.where(low, jnp.exp(sink_e - m_e), jnp.exp(sink_o - m_o))
                out = num / den * sg_ref[0, rows, lanes].astype(F32)
                o_ref[0, rows, lanes] = out.astype(o_ref.dtype)


def _attention(sink, q, ka, kb, va, vb, sga):
    bsz, seq_len, _ = q.shape
    tq = ATTN_TILE
    q_spec = pl.BlockSpec((1, tq, ATTN_WIDTH), lambda b, i, s: (b, i, 0))
    kv_spec = pl.BlockSpec((1, seq_len, LANES), lambda b, i, s: (b, 0, 0))
    grid_spec = pltpu.PrefetchScalarGridSpec(
        num_scalar_prefetch=1,
        grid=(bsz, seq_len // tq),
        in_specs=[q_spec, kv_spec, kv_spec, kv_spec, kv_spec, q_spec],
        out_specs=q_spec,
    )
    return pl.pallas_call(
        functools.partial(_attn_kernel, n_blocks=seq_len // BLOCK),
        grid_spec=grid_spec,
        out_shape=jax.ShapeDtypeStruct((bsz, seq_len, ATTN_WIDTH), BF16),
        compiler_params=pltpu.CompilerParams(dimension_semantics=("arbitrary", "arbitrary"),
                                             vmem_limit_bytes=VMEM_LIMIT),
        name="band_attention",
    )(sink, q, ka, kb, va, vb, sga)


def _seq_dft_kernel(a_ref, b_ref, m_ref, sg_ref, o_ref, xy_ref, *, chunk):
    r = pl.program_id(1)

    def chunks(ref):
        return [ref[0, q * chunk:(q + 1) * chunk, :].astype(F32) for q in range(RADIX)]

    def store(x, y):
        xy_ref[:chunk, :] = x.astype(xy_ref.dtype)
        xy_ref[chunk:, :] = y.astype(xy_ref.dtype)

    @pl.when(r == 0)
    def _():
        a, b = chunks(a_ref), chunks(b_ref)
        store((a[0] + a[2]) + (a[1] + a[3]), -((b[0] + b[2]) + (b[1] + b[3])))

    @pl.when(r == 1)
    def _():
        a, b = chunks(a_ref), chunks(b_ref)
        store((a[0] - a[2]) - (b[1] - b[3]), -((b[0] - b[2]) + (a[1] - a[3])))

    @pl.when(r == 2)
    def _():
        a, b = chunks(a_ref), chunks(b_ref)
        store((a[0] + a[2]) - (a[1] + a[3]), -((b[0] + b[2]) - (b[1] + b[3])))

    @pl.when(r == 3)
    def _():
        a, b = chunks(a_ref), chunks(b_ref)
        store((a[0] - a[2]) + (b[1] - b[3]), -((b[0] - b[2]) - (a[1] - a[3])))

    f = _dot(m_ref[r], xy_ref[...])
    o_ref[0] = (f * sg_ref[0].astype(F32)).astype(o_ref.dtype)


def _dft_matrix(seq_len):
    chunk = seq_len // RADIX
    k = RADIX * jnp.arange(chunk, dtype=jnp.int32)[None, :, None] + jnp.arange(RADIX, dtype=jnp.int32)[:, None, None]
    m = jnp.arange(chunk, dtype=jnp.int32)[None, None, :]
    ang = ((k * m) % seq_len).astype(F32) * (2.0 * math.pi / seq_len)
    return jnp.concatenate([jnp.cos(ang), jnp.sin(ang)], axis=2).astype(BF16)


def _seq_dft(a, b, dft_m, sgf):
    bsz, seq_len, width = a.shape
    chunk = seq_len // RADIX
    sg_view = sgf.reshape(bsz, chunk, RADIX * width)
    ab_spec = pl.BlockSpec((1, seq_len, width), lambda bi, r: (bi, 0, 0))
    m_spec = pl.BlockSpec((RADIX, chunk, 2 * chunk), lambda bi, r: (0, 0, 0), pipeline_mode=pl.Buffered(1))
    io_spec = pl.BlockSpec((1, chunk, width), lambda bi, r: (bi, 0, r))
    out = pl.pallas_call(
        functools.partial(_seq_dft_kernel, chunk=chunk),
        grid=(bsz, RADIX),
        in_specs=[ab_spec, ab_spec, m_spec, io_spec],
        out_specs=io_spec,
        out_shape=jax.ShapeDtypeStruct((bsz, chunk, RADIX * width), BF16),
        scratch_shapes=[pltpu.VMEM((2 * chunk, width), BF16)],
        compiler_params=pltpu.CompilerParams(dimension_semantics=("arbitrary", "arbitrary"),
                                             vmem_limit_bytes=VMEM_LIMIT),
        name="seq_dft",
    )(a, b, dft_m, sg_view)
    return out.reshape(bsz, seq_len, width)


def _out_proj_kernel(x_ref, ma_ref, mf_ref, w_ref, o_ref):
    y = _dot(ma_ref[...], w_ref[:ATTN_WIDTH, :]) + _dot(mf_ref[...], w_ref[ATTN_WIDTH:, :])
    o_ref[...] = x_ref[...] + y


def _out_proj(x2d, ma, mf, w_bf):
    rows = x2d.shape[0]
    tm = ROW_TILE
    row_spec = lambda width: pl.BlockSpec((tm, width), lambda i: (i, 0))
    return pl.pallas_call(
        _out_proj_kernel,
        grid=(rows // tm,),
        in_specs=[row_spec(D_MODEL), row_spec(ATTN_WIDTH), row_spec(FOURIER_WIDTH),
                  pl.BlockSpec((D_MODEL, D_MODEL), lambda i: (0, 0))],
        out_specs=row_spec(D_MODEL),
        out_shape=jax.ShapeDtypeStruct((rows, D_MODEL), F32),
        compiler_params=pltpu.CompilerParams(dimension_semantics=("arbitrary",), vmem_limit_bytes=VMEM_LIMIT),
        name="out_proj",
    )(x2d, ma, mf, w_bf)


def _rope_tables(seq_len):
    half = HEAD_DIM // 2
    inv_freq = 1.0 / (ROPE_THETA ** (jnp.arange(half, dtype=F32) / half))
    ang = jnp.arange(seq_len, dtype=F32)[:, None] * inv_freq[None, :]
    cos = jnp.cos(ang)
    sin = jnp.sin(ang)
    cos_t = jnp.concatenate([cos, cos, cos, cos], axis=1)
    sin_t = jnp.concatenate([-sin, sin, -sin, sin], axis=1)
    return cos_t, sin_t


def _head_sum_matrix(width):
    head = jnp.arange(width, dtype=jnp.int32) // HEAD_DIM
    return (head[:, None] == head[None, :]).astype(BF16)


def _layer(x, consts, params):
    bsz, seq_len, _ = x.shape
    gn, w_in_bf, qg, kg, sink, wab, w_out_bf = params
    cos_t, sin_t, gq, gk, dft_m = consts
    x2d = x.reshape(bsz * seq_len, D_MODEL)
    q, ka, kb, va, vb, sga, a, b, sgf = _in_proj(x2d, seq_len, gn, w_in_bf, qg, kg, cos_t, sin_t, gq, gk, wab)
    r3 = lambda t: t.reshape(bsz, seq_len, t.shape[-1])
    ma = _attention(sink, r3(q), r3(ka), r3(kb), r3(va), r3(vb), r3(sga))
    mf = _seq_dft(r3(a), r3(b), dft_m, r3(sgf))
    y = _out_proj(x2d, ma.reshape(bsz * seq_len, ATTN_WIDTH), mf.reshape(bsz * seq_len, FOURIER_WIDTH), w_out_bf)
    return y.reshape(bsz, seq_len, D_MODEL)


def kernel(x_prompt, x_sample, norm_gain, w_in, q_norm_gain, k_norm_gain, sink_logit, w_fourier, w_out):
    depth = norm_gain.shape[0]
    seq_p, seq_s = x_prompt.shape[1], x_sample.shape[1]
    assert seq_p == seq_s
    consts = _rope_tables(seq_p) + (_head_sum_matrix(ATTN_WIDTH), _head_sum_matrix(KV_WIDTH), _dft_matrix(seq_p))
    y_prompt, y_sample = x_prompt, x_sample
    for l in range(depth):
        params = (norm_gain[l][None, :], w_in[l].astype(BF16),
                  jnp.tile(q_norm_gain[l], N_Q_HEADS)[None, :], jnp.tile(k_norm_gain[l], N_KV_HEADS)[None, :],
                  sink_logit[l], _fourier_weights(w_fourier[l], seq_p), w_out[l].astype(BF16))
        y_prompt = _layer(y_prompt, consts, params)
        y_sample = _layer(y_sample, consts, params)
    return (y_prompt, y_sample)
```

```python
import functools
import math

import jax
import jax.numpy as jnp
from jax.experimental import pallas as pl
from jax.experimental.pallas import tpu as pltpu

D_MODEL = 1024
HEAD_DIM = 64
N_Q_HEADS = 8
N_KV_HEADS = 2
ATTN_WIDTH = N_Q_HEADS * HEAD_DIM
KV_WIDTH = N_KV_HEADS * HEAD_DIM
FOURIER_WIDTH = D_MODEL - ATTN_WIDTH
N_GROUPS = 4
GROUP_DIM = FOURIER_WIDTH // N_GROUPS
IN_WIDTH = 2 * ATTN_WIDTH + 2 * KV_WIDTH + 2 * FOURIER_WIDTH
Q_OFF = 0
K_OFF = ATTN_WIDTH
V_OFF = K_OFF + KV_WIDTH
GA_OFF = V_OFF + KV_WIDTH
U_OFF = GA_OFF + ATTN_WIDTH
GF_OFF = U_OFF + FOURIER_WIDTH
BLOCK = 128
ROPE_THETA = 10000.0
EPS = 1e-6
NEG = -1e30
LANES = 128
RADIX = 4
LOG2E = math.log2(math.e)
MAX_UNSHIFTED_LOGIT2 = 100.0

ROW_TILE = 512
ATTN_TILE = 512
VMEM_LIMIT = 48 * 1024 * 1024

BF16 = jnp.bfloat16
F32 = jnp.float32


def _dot(a, b):
    return jnp.dot(a, b, preferred_element_type=F32)


def _dot_nt(a, b):
    return jax.lax.dot_general(a, b, (((1,), (1,)), ((), ())), preferred_element_type=F32)


def _silu(x):
    return x / (1.0 + jnp.exp(-x))


def _rotate_half(t, first_half):
    width = t.shape[-1]
    fwd = pltpu.roll(t, HEAD_DIM // 2, axis=1)
    bwd = pltpu.roll(t, width - HEAD_DIM // 2, axis=1)
    return jnp.where(first_half, bwd, fwd)


def _fourier_weight_kernel(cs_ref, w_ref, o_ref):
    for g in range(N_GROUPS):
        w = w_ref[g]
        c = jnp.dot(cs_ref[0], w, preferred_element_type=F32, precision=jax.lax.Precision.HIGHEST)
        s = jnp.dot(cs_ref[1], w, preferred_element_type=F32, precision=jax.lax.Precision.HIGHEST)
        o_ref[g] = jnp.concatenate([c, s], axis=1).astype(o_ref.dtype)


def _fourier_weights(w_four, seq_len):
    idx = (jnp.arange(GROUP_DIM, dtype=jnp.int32)[:, None] * jnp.arange(GROUP_DIM, dtype=jnp.int32)[None, :]) % GROUP_DIM
    ang = idx.astype(F32) * (2.0 * math.pi / GROUP_DIM)
    scale = 1.0 / math.sqrt(seq_len * GROUP_DIM)
    cs = jnp.stack([jnp.cos(ang), jnp.sin(ang)]) * scale
    return pl.pallas_call(
        _fourier_weight_kernel,
        out_shape=jax.ShapeDtypeStruct((N_GROUPS, GROUP_DIM, 2 * GROUP_DIM), BF16),
        name="fourier_weight_prep",
    )(cs, w_four)


def _in_proj_kernel(x_ref, gn_ref, w_ref, qg_ref, kg_ref, cos_ref, sin_ref, gq_ref, gk_ref, wab_ref,
                    q_ref, ka_ref, kb_ref, va_ref, vb_ref, sga_ref, a_ref, b_ref, sgf_ref, perm_ref):
    x = x_ref[...]
    ms = jnp.mean(x * x, axis=-1, keepdims=True)
    h = (x * jax.lax.rsqrt(ms + EPS) * gn_ref[...]).astype(BF16)

    cos = cos_ref[...]
    sin = sin_ref[...]
    lane = jax.lax.broadcasted_iota(jnp.int32, (1, LANES), 1)
    first_half_pair = (lane % HEAD_DIM) < (HEAD_DIM // 2)
    low_head = lane < HEAD_DIM

    zq = _dot(h, w_ref[:, Q_OFF:Q_OFF + ATTN_WIDTH])
    ssq = _dot((zq * zq).astype(BF16), gq_ref[...]) * (1.0 / HEAD_DIM)
    qn = zq * jax.lax.rsqrt(ssq + EPS) * qg_ref[...]
    n_pairs = ATTN_WIDTH // LANES
    cos_q = jnp.concatenate([cos] * n_pairs, axis=1)
    sin_q = jnp.concatenate([sin] * n_pairs, axis=1)
    fh_q = jnp.concatenate([first_half_pair] * n_pairs, axis=1)
    qr = qn * cos_q + _rotate_half(qn, fh_q) * sin_q
    q_ref[...] = (qr * (HEAD_DIM ** -0.5 * LOG2E)).astype(q_ref.dtype)

    zk = _dot(h, w_ref[:, K_OFF:K_OFF + KV_WIDTH])
    ssk = _dot((zk * zk).astype(BF16), gk_ref[...]) * (1.0 / HEAD_DIM)
    kn = zk * jax.lax.rsqrt(ssk + EPS) * kg_ref[...]
    kr = kn * cos + _rotate_half(kn, first_half_pair) * sin
    kr_sw = pltpu.roll(kr, HEAD_DIM, axis=1)
    ka_ref[...] = jnp.where(low_head, kr, kr_sw).astype(ka_ref.dtype)
    kb_ref[...] = jnp.where(low_head, kr_sw, kr).astype(kb_ref.dtype)

    zv = _dot(h, w_ref[:, V_OFF:V_OFF + KV_WIDTH])
    zv_sw = pltpu.roll(zv, HEAD_DIM, axis=1)
    va_ref[...] = jnp.where(low_head, zv, zv_sw).astype(va_ref.dtype)
    vb_ref[...] = jnp.where(low_head, zv_sw, zv).astype(vb_ref.dtype)

    sga_ref[...] = _silu(_dot(h, w_ref[:, GA_OFF:GA_OFF + ATTN_WIDTH])).astype(sga_ref.dtype)
    sgf = _silu(_dot(h, w_ref[:, GF_OFF:GF_OFF + FOURIER_WIDTH]))
    for cb in range(FOURIER_WIDTH // LANES):
        perm_ref[cb] = sgf[:, cb * LANES:(cb + 1) * LANES]
    for r in range(RADIX):
        for cb in range(FOURIER_WIDTH // LANES):
            col = r * FOURIER_WIDTH + cb * LANES
            sgf_ref[:, col:col + LANES] = (
                perm_ref[cb, pl.ds(r, ROW_TILE // RADIX, stride=RADIX), :].astype(sgf_ref.dtype))

    zu = _dot(h, w_ref[:, U_OFF:U_OFF + FOURIER_WIDTH]).astype(BF16)
    for g in range(N_GROUPS):
        ab = _dot(zu[:, g * GROUP_DIM:(g + 1) * GROUP_DIM], wab_ref[g])
        a_ref[:, g * GROUP_DIM:(g + 1) * GROUP_DIM] = ab[:, :GROUP_DIM].astype(a_ref.dtype)
        b_ref[:, g * GROUP_DIM:(g + 1) * GROUP_DIM] = ab[:, GROUP_DIM:].astype(b_ref.dtype)


def _in_proj(x2d, seq_len, gn, w_bf, qg, kg, cos_t, sin_t, gq, gk, wab):
    rows = x2d.shape[0]
    tm = ROW_TILE
    steps_per_seq = seq_len // tm
    row_spec = lambda width: pl.BlockSpec((tm, width), lambda i: (i, 0))
    full = lambda shape: pl.BlockSpec(shape, lambda i: (0,) * len(shape))
    tab_spec = pl.BlockSpec((tm, LANES), lambda i: (i % steps_per_seq, 0))
    out_widths = (ATTN_WIDTH, LANES, LANES, LANES, LANES, ATTN_WIDTH, FOURIER_WIDTH, FOURIER_WIDTH)
    perm_spec = pl.BlockSpec((tm // RADIX, RADIX * FOURIER_WIDTH), lambda i: (i, 0))
    return pl.pallas_call(
        _in_proj_kernel,
        grid=(rows // tm,),
        in_specs=[row_spec(D_MODEL), full((1, D_MODEL)), full((D_MODEL, IN_WIDTH)),
                  full((1, ATTN_WIDTH)), full((1, KV_WIDTH)), tab_spec, tab_spec,
                  full((ATTN_WIDTH, ATTN_WIDTH)), full((KV_WIDTH, KV_WIDTH)),
                  full((N_GROUPS, GROUP_DIM, 2 * GROUP_DIM))],
        out_specs=[row_spec(w) for w in out_widths] + [perm_spec],
        out_shape=[jax.ShapeDtypeStruct((rows, w), BF16) for w in out_widths]
        + [jax.ShapeDtypeStruct((rows // RADIX, RADIX * FOURIER_WIDTH), BF16)],
        scratch_shapes=[pltpu.VMEM((FOURIER_WIDTH // LANES, tm, LANES), F32)],
        compiler_params=pltpu.CompilerParams(dimension_semantics=("arbitrary",), vmem_limit_bytes=VMEM_LIMIT),
        name="in_proj",
    )(x2d, gn, w_bf, qg, kg, cos_t, sin_t, gq, gk, wab)


def _attn_kernel(sink_ref, bounded_ref, q_ref, ka_ref, kb_ref, va_ref, vb_ref, sg_ref, o_ref, *, n_blocks):
    refs = (sink_ref, q_ref, ka_ref, kb_ref, va_ref, vb_ref, sg_ref, o_ref)

    @pl.when(bounded_ref[0] == 1)
    def _():
        _attn_body(*refs, n_blocks=n_blocks, shift=False)

    @pl.when(bounded_ref[0] == 0)
    def _():
        _attn_body(*refs, n_blocks=n_blocks, shift=True)


def _attn_body(sink_ref, q_ref, ka_ref, kb_ref, va_ref, vb_ref, sg_ref, o_ref, *, n_blocks, shift):
    i = pl.program_id(1)
    blocks_per_step = ATTN_TILE // BLOCK
    row = jax.lax.broadcasted_iota(jnp.int32, (BLOCK, 2 * BLOCK), 0)
    col = jax.lax.broadcasted_iota(jnp.int32, (BLOCK, 2 * BLOCK), 1) % BLOCK
    lane = jax.lax.broadcasted_iota(jnp.int32, (1, LANES), 1)
    low = lane < HEAD_DIM
    zero = jnp.zeros((), BF16)
    one = jnp.ones((), BF16)

    for jb in range(blocks_per_step):
        ib = i * blocks_per_step + jb
        rows = slice(jb * BLOCK, (jb + 1) * BLOCK)
        bias_prev = jnp.where(col >= row, 0.0, NEG) + jnp.where(ib == 0, NEG, 0.0)
        bias_next = jnp.where(col <= row, 0.0, NEG) + jnp.where(ib == n_blocks - 1, NEG, 0.0)
        starts = [pl.multiple_of(jnp.clip(ib + c, 0, n_blocks - 1) * BLOCK, BLOCK) for c in (-1, 0, 1)]
        for kvh, (k_ref, v_ref) in enumerate(((ka_ref, va_ref), (kb_ref, vb_ref))):
            kbd, v_even, v_odd = [], [], []
            for st in starts:
                kblk = k_ref[0, pl.ds(st, BLOCK), :]
                vblk = v_ref[0, pl.ds(st, BLOCK), :]
                kbd.append(jnp.concatenate([jnp.where(low, kblk, zero), jnp.where(low, zero, kblk)], axis=0))
                v_even.append(jnp.where(low, vblk, one))
                v_odd.append(jnp.where(low, one, vblk))
            v_even = jnp.concatenate(v_even, axis=0)
            v_odd = jnp.concatenate(v_odd, axis=0)
            for pair in range(2):
                pidx = kvh * 2 + pair
                lanes = slice(pidx * LANES, (pidx + 1) * LANES)
                qp = q_ref[0, rows, lanes]
                s_prev = _dot_nt(qp, kbd[0]) + bias_prev
                s_mid = _dot_nt(qp, kbd[1])
                s_next = _dot_nt(qp, kbd[2]) + bias_next
                scores = (s_prev, s_mid, s_next)
                sink_e = sink_ref[2 * pidx]
                sink_o = sink_ref[2 * pidx + 1]
                if shift:
                    smax = jnp.maximum(jnp.maximum(s_prev, s_mid), s_next)
                    m_e = jnp.maximum(jnp.max(smax[:, :BLOCK], axis=-1, keepdims=True), sink_e)
                    m_o = jnp.maximum(jnp.max(smax[:, BLOCK:], axis=-1, keepdims=True), sink_o)
                    p_e = jnp.concatenate([jnp.exp2(s[:, :BLOCK] - m_e) for s in scores], axis=1)
                    p_o = jnp.concatenate([jnp.exp2(s[:, BLOCK:] - m_o) for s in scores], axis=1)
                    p_sink = jnp.where(low, jnp.exp2(sink_e - m_e), jnp.exp2(sink_o - m_o))
                else:
                    p_e = jnp.concatenate([jnp.exp2(s[:, :BLOCK]) for s in scores], axis=1)
                    p_o = jnp.concatenate([jnp.exp2(s[:, BLOCK:]) for s in scores], axis=1)
                    p_sink = jnp.exp2(jnp.where(low, sink_e, sink_o))
                acc_e = _dot(p_e.astype(BF16), v_even)
                acc_o = _dot(p_o.astype(BF16), v_odd)
                num = jnp.where(low, acc_e, acc_o)
                den = pltpu.roll(jnp.where(low, acc_o, acc_e), HEAD_DIM, axis=1) + p_sink
                out = num / den * sg_ref[0, rows, lanes].astype(F32)
                o_ref[0, rows, lanes] = out.astype(o_ref.dtype)


def _attention(sink2, bounded, q, ka, kb, va, vb, sga):
    bsz, seq_len, _ = q.shape
    tq = ATTN_TILE
    q_spec = pl.BlockSpec((1, tq, ATTN_WIDTH), lambda b, i, s, f: (b, i, 0))
    kv_spec = pl.BlockSpec((1, seq_len, LANES), lambda b, i, s, f: (b, 0, 0))
    grid_spec = pltpu.PrefetchScalarGridSpec(
        num_scalar_prefetch=2,
        grid=(bsz, seq_len // tq),
        in_specs=[q_spec, kv_spec, kv_spec, kv_spec, kv_spec, q_spec],
        out_specs=q_spec,
    )
    return pl.pallas_call(
        functools.partial(_attn_kernel, n_blocks=seq_len // BLOCK),
        grid_spec=grid_spec,
        out_shape=jax.ShapeDtypeStruct((bsz, seq_len, ATTN_WIDTH), BF16),
        compiler_params=pltpu.CompilerParams(dimension_semantics=("arbitrary", "arbitrary"),
                                             vmem_limit_bytes=VMEM_LIMIT),
        name="band_attention",
    )(sink2, bounded, q, ka, kb, va, vb, sga)


def _seq_dft_kernel(a_ref, b_ref, m_ref, sg_ref, o_ref, xy_ref, *, chunk):
    r = pl.program_id(1)

    def chunks(ref):
        return [ref[0, q * chunk:(q + 1) * chunk, :].astype(F32) for q in range(RADIX)]

    def store(x, y):
        xy_ref[:chunk, :] = x.astype(xy_ref.dtype)
        xy_ref[chunk:, :] = y.astype(xy_ref.dtype)

    @pl.when(r == 0)
    def _():
        a, b = chunks(a_ref), chunks(b_ref)
        store((a[0] + a[2]) + (a[1] + a[3]), -((b[0] + b[2]) + (b[1] + b[3])))

    @pl.when(r == 1)
    def _():
        a, b = chunks(a_ref), chunks(b_ref)
        store((a[0] - a[2]) - (b[1] - b[3]), -((b[0] - b[2]) + (a[1] - a[3])))

    @pl.when(r == 2)
    def _():
        a, b = chunks(a_ref), chunks(b_ref)
        store((a[0] + a[2]) - (a[1] + a[3]), -((b[0] + b[2]) - (b[1] + b[3])))

    @pl.when(r == 3)
    def _():
        a, b = chunks(a_ref), chunks(b_ref)
        store((a[0] - a[2]) + (b[1] - b[3]), -((b[0] - b[2]) - (a[1] - a[3])))

    f = _dot(m_ref[r], xy_ref[...])
    o_ref[0] = (f * sg_ref[0].astype(F32)).astype(o_ref.dtype)


def _dft_matrix(seq_len):
    chunk = seq_len // RADIX
    split = 32
    m = jnp.arange(chunk, dtype=jnp.int32)[None, :]
    ang_hi = ((jnp.arange(chunk // split, dtype=jnp.int32)[:, None] * m) % split).astype(F32) * (2.0 * math.pi / split)
    ang_lo = ((jnp.arange(RADIX * split, dtype=jnp.int32)[:, None] * m) % seq_len).astype(F32) * (2.0 * math.pi / seq_len)
    c_hi, s_hi = jnp.cos(ang_hi)[None, :, None, :], jnp.sin(ang_hi)[None, :, None, :]
    lo = lambda t: t.reshape(split, RADIX, chunk).transpose(1, 0, 2)[:, None, :, :]
    c_lo, s_lo = lo(jnp.cos(ang_lo)), lo(jnp.sin(ang_lo))
    cos = (c_hi * c_lo - s_hi * s_lo).reshape(RADIX, chunk, chunk)
    sin = (s_hi * c_lo + c_hi * s_lo).reshape(RADIX, chunk, chunk)
    return jnp.concatenate([cos, sin], axis=2).astype(BF16)


def _seq_dft(a, b, dft_m, sg_perm):
    bsz, seq_len, width = a.shape
    chunk = seq_len // RADIX
    ab_spec = pl.BlockSpec((1, seq_len, width), lambda bi, r: (bi, 0, 0))
    m_spec = pl.BlockSpec((RADIX, chunk, 2 * chunk), lambda bi, r: (0, 0, 0), pipeline_mode=pl.Buffered(1))
    io_spec = pl.BlockSpec((1, chunk, width), lambda bi, r: (bi, 0, r))
    return pl.pallas_call(
        functools.partial(_seq_dft_kernel, chunk=chunk),
        grid=(bsz, RADIX),
        in_specs=[ab_spec, ab_spec, m_spec, io_spec],
        out_specs=io_spec,
        out_shape=jax.ShapeDtypeStruct((bsz, chunk, RADIX * width), BF16),
        scratch_shapes=[pltpu.VMEM((2 * chunk, width), BF16)],
        compiler_params=pltpu.CompilerParams(dimension_semantics=("arbitrary", "arbitrary"),
                                             vmem_limit_bytes=VMEM_LIMIT),
        name="seq_dft",
    )(a, b, dft_m, sg_perm)


def _out_proj_kernel(x_ref, ma_ref, mf_ref, w_ref, o_ref, acc_ref):
    n_cb = D_MODEL // LANES
    y = x_ref[...] + _dot(ma_ref[...], w_ref[:ATTN_WIDTH, :])
    for cb in range(n_cb):
        acc_ref[cb] = y[:, cb * LANES:(cb + 1) * LANES]
    for r in range(RADIX):
        y = _dot(mf_ref[:, r * FOURIER_WIDTH:(r + 1) * FOURIER_WIDTH], w_ref[ATTN_WIDTH:, :])
        for cb in range(n_cb):
            acc_ref[cb, pl.ds(r, ROW_TILE // RADIX, stride=RADIX), :] += y[:, cb * LANES:(cb + 1) * LANES]
    for cb in range(n_cb):
        o_ref[:, cb * LANES:(cb + 1) * LANES] = acc_ref[cb]


def _out_proj(x2d, ma, mf_perm, w_bf):
    rows = x2d.shape[0]
    tm = ROW_TILE
    row_spec = lambda width: pl.BlockSpec((tm, width), lambda i: (i, 0))
    return pl.pallas_call(
        _out_proj_kernel,
        grid=(rows // tm,),
        in_specs=[row_spec(D_MODEL), row_spec(ATTN_WIDTH),
                  pl.BlockSpec((tm // RADIX, RADIX * FOURIER_WIDTH), lambda i: (i, 0)),
                  pl.BlockSpec((D_MODEL, D_MODEL), lambda i: (0, 0))],
        out_specs=row_spec(D_MODEL),
        out_shape=jax.ShapeDtypeStruct((rows, D_MODEL), F32),
        scratch_shapes=[pltpu.VMEM((D_MODEL // LANES, tm, LANES), F32)],
        compiler_params=pltpu.CompilerParams(dimension_semantics=("arbitrary",), vmem_limit_bytes=VMEM_LIMIT),
        name="out_proj",
    )(x2d, ma, mf_perm, w_bf)


def _rope_tables(seq_len):
    half = HEAD_DIM // 2
    inv_freq = 1.0 / (ROPE_THETA ** (jnp.arange(half, dtype=F32) / half))
    ang = jnp.arange(seq_len, dtype=F32)[:, None] * inv_freq[None, :]
    cos = jnp.cos(ang)
    sin = jnp.sin(ang)
    cos_t = jnp.concatenate([cos, cos, cos, cos], axis=1)
    sin_t = jnp.concatenate([-sin, sin, -sin, sin], axis=1)
    return cos_t, sin_t


def _head_sum_matrix(width):
    head = jnp.arange(width, dtype=jnp.int32) // HEAD_DIM
    return (head[:, None] == head[None, :]).astype(BF16)


def _layer(x, consts, params):
    bsz, seq_len, _ = x.shape
    gn, w_in_bf, qg, kg, sink2, bounded, wab, w_out_bf = params
    cos_t, sin_t, gq, gk, dft_m = consts
    x2d = x.reshape(bsz * seq_len, D_MODEL)
    q, ka, kb, va, vb, sga, a, b, sgf = _in_proj(x2d, seq_len, gn, w_in_bf, qg, kg, cos_t, sin_t, gq, gk, wab)
    r3 = lambda t: t.reshape(bsz, seq_len, t.shape[-1])
    ma = _attention(sink2, bounded, r3(q), r3(ka), r3(kb), r3(va), r3(vb), r3(sga))
    perm_rows = seq_len // RADIX
    mf = _seq_dft(r3(a), r3(b), dft_m, sgf.reshape(bsz, perm_rows, RADIX * FOURIER_WIDTH))
    y = _out_proj(x2d, ma.reshape(bsz * seq_len, ATTN_WIDTH),
                  mf.reshape(bsz * perm_rows, RADIX * FOURIER_WIDTH), w_out_bf)
    return y.reshape(bsz, seq_len, D_MODEL)


def kernel(x_prompt, x_sample, norm_gain, w_in, q_norm_gain, k_norm_gain, sink_logit, w_fourier, w_out):
    depth = norm_gain.shape[0]
    seq_p, seq_s = x_prompt.shape[1], x_sample.shape[1]
    assert seq_p == seq_s
    consts = _rope_tables(seq_p) + (_head_sum_matrix(ATTN_WIDTH), _head_sum_matrix(KV_WIDTH), _dft_matrix(seq_p))
    y_prompt, y_sample = x_prompt, x_sample
    for l in range(depth):
        sink2 = sink_logit[l].astype(F32) * LOG2E
        logit_bound = (HEAD_DIM ** 0.5 * LOG2E) * jnp.max(jnp.abs(q_norm_gain[l])) * jnp.max(jnp.abs(k_norm_gain[l]))
        bounded = jnp.maximum(logit_bound, jnp.max(jnp.abs(sink2))) <= MAX_UNSHIFTED_LOGIT2
        params = (norm_gain[l][None, :], w_in[l].astype(BF16),
                  jnp.tile(q_norm_gain[l], N_Q_HEADS)[None, :], jnp.tile(k_norm_gain[l], N_KV_HEADS)[None, :],
                  sink2, bounded.astype(jnp.int32)[None],
                  _fourier_weights(w_fourier[l], seq_p), w_out[l].astype(BF16))
        y_prompt = _layer(y_prompt, consts, params)
        y_sample = _layer(y_sample, consts, params)
    return (y_prompt, y_sample)
```

```python
import functools
import math

import jax
import jax.numpy as jnp
from jax.experimental import pallas as pl
from jax.experimental.pallas import tpu as pltpu

D_MODEL = 1024
HEAD_DIM = 64
N_Q_HEADS = 8
N_KV_HEADS = 2
ATTN_WIDTH = N_Q_HEADS * HEAD_DIM
KV_WIDTH = N_KV_HEADS * HEAD_DIM
FOURIER_WIDTH = D_MODEL - ATTN_WIDTH
N_GROUPS = 4
GROUP_DIM = FOURIER_WIDTH // N_GROUPS
IN_WIDTH = 2 * ATTN_WIDTH + 2 * KV_WIDTH + 2 * FOURIER_WIDTH
Q_OFF = 0
K_OFF = ATTN_WIDTH
V_OFF = K_OFF + KV_WIDTH
GA_OFF = V_OFF + KV_WIDTH
U_OFF = GA_OFF + ATTN_WIDTH
GF_OFF = U_OFF + FOURIER_WIDTH
BLOCK = 128
ROPE_THETA = 10000.0
EPS = 1e-6
NEG = -1e30
LANES = 128
RADIX = 4
DFT_COLS = 256
DFT_PREP_ROWS = 128
LOG2E = math.log2(math.e)
MAX_UNSHIFTED_LOGIT2 = 100.0

ROW_TILE = 512
ATTN_TILE = 512
VMEM_LIMIT = 48 * 1024 * 1024

BF16 = jnp.bfloat16
F32 = jnp.float32


def _dot(a, b):
    return jnp.dot(a, b, preferred_element_type=F32)


def _dot_nt(a, b):
    return jax.lax.dot_general(a, b, (((1,), (1,)), ((), ())), preferred_element_type=F32)


def _silu(x):
    return x / (1.0 + jnp.exp(-x))


def _rotate_half(t, first_half):
    width = t.shape[-1]
    fwd = pltpu.roll(t, HEAD_DIM // 2, axis=1)
    bwd = pltpu.roll(t, width - HEAD_DIM // 2, axis=1)
    return jnp.where(first_half, bwd, fwd)


def _fourier_weight_kernel(cs_ref, w_ref, o_ref):
    for g in range(N_GROUPS):
        w = w_ref[g]
        c = jnp.dot(cs_ref[0], w, preferred_element_type=F32, precision=jax.lax.Precision.HIGHEST)
        s = jnp.dot(cs_ref[1], w, preferred_element_type=F32, precision=jax.lax.Precision.HIGHEST)
        o_ref[g] = jnp.concatenate([c, s], axis=1).astype(o_ref.dtype)


def _fourier_weights(w_four, seq_len):
    idx = (jnp.arange(GROUP_DIM, dtype=jnp.int32)[:, None] * jnp.arange(GROUP_DIM, dtype=jnp.int32)[None, :]) % GROUP_DIM
    ang = idx.astype(F32) * (2.0 * math.pi / GROUP_DIM)
    scale = 1.0 / math.sqrt(seq_len * GROUP_DIM)
    cs = jnp.stack([jnp.cos(ang), jnp.sin(ang)]) * scale
    return pl.pallas_call(
        _fourier_weight_kernel,
        out_shape=jax.ShapeDtypeStruct((N_GROUPS, GROUP_DIM, 2 * GROUP_DIM), BF16),
        name="fourier_weight_prep",
    )(cs, w_four)


def _perm_col(r, feature):
    return ((feature // DFT_COLS) * RADIX + r) * DFT_COLS + feature % DFT_COLS


def _in_proj_kernel(x_ref, gn_ref, w_ref, qg_ref, kg_ref, cos_ref, sin_ref, hs_ref, wab_ref,
                    q_ref, ka_ref, kb_ref, va_ref, vb_ref, sga_ref, a_ref, b_ref, sgf_ref, perm_ref):
    x = x_ref[...]
    ms = jnp.mean(x * x, axis=-1, keepdims=True)
    h = (x * jax.lax.rsqrt(ms + EPS) * gn_ref[...]).astype(BF16)

    cos = cos_ref[...]
    sin = sin_ref[...]
    lane = jax.lax.broadcasted_iota(jnp.int32, (1, LANES), 1)
    first_half_pair = (lane % HEAD_DIM) < (HEAD_DIM // 2)
    low_head = lane < HEAD_DIM
    hs = hs_ref[...]

    def head_mean_sq(z):
        return _dot((z * z).astype(BF16), hs) * (1.0 / HEAD_DIM)

    zq = _dot(h, w_ref[:, Q_OFF:Q_OFF + ATTN_WIDTH])
    half = ATTN_WIDTH // 2
    ssq = jnp.concatenate([head_mean_sq(zq[:, :half]), head_mean_sq(zq[:, half:])], axis=1)
    qn = zq * jax.lax.rsqrt(ssq + EPS) * qg_ref[...]
    n_pairs = ATTN_WIDTH // LANES
    cos_q = jnp.concatenate([cos] * n_pairs, axis=1)
    sin_q = jnp.concatenate([sin] * n_pairs, axis=1)
    fh_q = jnp.concatenate([first_half_pair] * n_pairs, axis=1)
    qr = qn * cos_q + _rotate_half(qn, fh_q) * sin_q
    q_ref[...] = (qr * (HEAD_DIM ** -0.5 * LOG2E)).astype(q_ref.dtype)

    zkv = _dot(h, w_ref[:, K_OFF:K_OFF + 2 * KV_WIDTH])
    zk = zkv[:, :KV_WIDTH]
    ssk = head_mean_sq(zkv)[:, :KV_WIDTH]
    kn = zk * jax.lax.rsqrt(ssk + EPS) * kg_ref[...]
    kr = kn * cos + _rotate_half(kn, first_half_pair) * sin
    kr_sw = pltpu.roll(kr, HEAD_DIM, axis=1)
    ka_ref[...] = jnp.where(low_head, kr, kr_sw).astype(ka_ref.dtype)
    kb_ref[...] = jnp.where(low_head, kr_sw, kr).astype(kb_ref.dtype)
    zv = zkv[:, KV_WIDTH:]
    zv_sw = pltpu.roll(zv, HEAD_DIM, axis=1)
    va_ref[...] = jnp.where(low_head, zv, zv_sw).astype(va_ref.dtype)
    vb_ref[...] = jnp.where(low_head, zv_sw, zv).astype(vb_ref.dtype)

    sga_ref[...] = _silu(_dot(h, w_ref[:, GA_OFF:GA_OFF + ATTN_WIDTH])).astype(sga_ref.dtype)
    sgf = _silu(_dot(h, w_ref[:, GF_OFF:GF_OFF + FOURIER_WIDTH]))
    for cb in range(FOURIER_WIDTH // LANES):
        perm_ref[cb] = sgf[:, cb * LANES:(cb + 1) * LANES]
    for r in range(RADIX):
        for cb in range(FOURIER_WIDTH // LANES):
            col = _perm_col(r, cb * LANES)
            sgf_ref[:, col:col + LANES] = (
                perm_ref[cb, pl.ds(r, ROW_TILE // RADIX, stride=RADIX), :].astype(sgf_ref.dtype))

    zu = _dot(h, w_ref[:, U_OFF:U_OFF + FOURIER_WIDTH]).astype(BF16)
    for g in range(N_GROUPS):
        ab = _dot(zu[:, g * GROUP_DIM:(g + 1) * GROUP_DIM], wab_ref[g])
        a_ref[:, g * GROUP_DIM:(g + 1) * GROUP_DIM] = ab[:, :GROUP_DIM].astype(a_ref.dtype)
        b_ref[:, g * GROUP_DIM:(g + 1) * GROUP_DIM] = ab[:, GROUP_DIM:].astype(b_ref.dtype)


def _in_proj(x2d, seq_len, gn, w_bf, qg, kg, cos_t, sin_t, hs, wab):
    rows = x2d.shape[0]
    tm = ROW_TILE
    steps_per_seq = seq_len // tm
    row_spec = lambda width: pl.BlockSpec((tm, width), lambda i: (i, 0))
    full = lambda shape: pl.BlockSpec(shape, lambda i: (0,) * len(shape))
    tab_spec = pl.BlockSpec((tm, LANES), lambda i: (i % steps_per_seq, 0))
    out_widths = (ATTN_WIDTH, LANES, LANES, LANES, LANES, ATTN_WIDTH, FOURIER_WIDTH, FOURIER_WIDTH)
    perm_spec = pl.BlockSpec((tm // RADIX, RADIX * FOURIER_WIDTH), lambda i: (i, 0))
    return pl.pallas_call(
        _in_proj_kernel,
        grid=(rows // tm,),
        in_specs=[row_spec(D_MODEL), full((1, D_MODEL)), full((D_MODEL, IN_WIDTH)),
                  full((1, ATTN_WIDTH)), full((1, KV_WIDTH)), tab_spec, tab_spec,
                  full((2 * LANES, 2 * LANES)), full((N_GROUPS, GROUP_DIM, 2 * GROUP_DIM))],
        out_specs=[row_spec(w) for w in out_widths] + [perm_spec],
        out_shape=[jax.ShapeDtypeStruct((rows, w), BF16) for w in out_widths]
        + [jax.ShapeDtypeStruct((rows // RADIX, RADIX * FOURIER_WIDTH), BF16)],
        scratch_shapes=[pltpu.VMEM((FOURIER_WIDTH // LANES, tm, LANES), F32)],
        compiler_params=pltpu.CompilerParams(dimension_semantics=("arbitrary",), vmem_limit_bytes=VMEM_LIMIT),
        name="in_proj",
    )(x2d, gn, w_bf, qg, kg, cos_t, sin_t, hs, wab)


def _attn_kernel(sink_ref, bounded_ref, q_ref, ka_ref, kb_ref, va_ref, vb_ref, sg_ref, o_ref, *, n_blocks):
    refs = (sink_ref, q_ref, ka_ref, kb_ref, va_ref, vb_ref, sg_ref, o_ref)

    @pl.when(bounded_ref[0] == 1)
    def _():
        _attn_body(*refs, n_blocks=n_blocks, shift=False)

    @pl.when(bounded_ref[0] == 0)
    def _():
        _attn_body(*refs, n_blocks=n_blocks, shift=True)


def _attn_body(sink_ref, q_ref, ka_ref, kb_ref, va_ref, vb_ref, sg_ref, o_ref, *, n_blocks, shift):
    i = pl.program_id(1)
    blocks_per_step = ATTN_TILE // BLOCK
    row = jax.lax.broadcasted_iota(jnp.int32, (BLOCK, 2 * BLOCK), 0)
    col = jax.lax.broadcasted_iota(jnp.int32, (BLOCK, 2 * BLOCK), 1) % BLOCK
    lane = jax.lax.broadcasted_iota(jnp.int32, (1, LANES), 1)
    low = lane < HEAD_DIM
    zero = jnp.zeros((), BF16)
    lane_full = jax.lax.broadcasted_iota(jnp.int32, (BLOCK, LANES), 1)
    ones_low = jnp.where(lane_full < HEAD_DIM, 1.0, 0.0).astype(BF16)
    ones_high = jnp.where(lane_full < HEAD_DIM, 0.0, 1.0).astype(BF16)

    for jb in range(blocks_per_step):
        ib = i * blocks_per_step + jb
        rows = slice(jb * BLOCK, (jb + 1) * BLOCK)
        edge_prev = jnp.where(ib == 0, NEG, 0.0)
        edge_next = jnp.where(ib == n_blocks - 1, NEG, 0.0)
        bias_prev = jnp.where(col >= row, 0.0, NEG) + edge_prev
        bias_next = jnp.where(col <= row, 0.0, NEG) + edge_next
        starts = [pl.multiple_of(jnp.clip(ib + c, 0, n_blocks - 1) * BLOCK, BLOCK) for c in (-1, 0, 1)]
        for kvh, (k_ref, v_ref) in enumerate(((ka_ref, va_ref), (kb_ref, vb_ref))):
            kbd, vbd = [], []
            for st in starts:
                kblk = k_ref[0, pl.ds(st, BLOCK), :]
                vblk = v_ref[0, pl.ds(st, BLOCK), :]
                kbd.append(jnp.concatenate([jnp.where(low, kblk, zero), jnp.where(low, zero, kblk)], axis=0))
                vbd.append(jnp.concatenate([
                    jnp.concatenate([jnp.where(low, vblk, zero), ones_low], axis=1),
                    jnp.concatenate([jnp.where(low, zero, vblk), ones_high], axis=1)], axis=0))
            vbd = jnp.concatenate(vbd, axis=0)
            for pair in range(2):
                pidx = kvh * 2 + pair
                lanes = slice(pidx * LANES, (pidx + 1) * LANES)
                qp = q_ref[0, rows, lanes]
                scores = [_dot_nt(qp, kb) for kb in kbd]
                sink_e = sink_ref[2 * pidx]
                sink_o = sink_ref[2 * pidx + 1]
                if shift:
                    scores = [scores[0] + bias_prev, scores[1], scores[2] + bias_next]
                    smax = jnp.maximum(jnp.maximum(scores[0], scores[1]), scores[2])
                    m_e = jnp.maximum(jnp.max(smax[:, :BLOCK], axis=-1, keepdims=True), sink_e)
                    m_o = jnp.maximum(jnp.max(smax[:, BLOCK:], axis=-1, keepdims=True), sink_o)
                    m_both = jnp.concatenate([jnp.broadcast_to(m_e, (BLOCK, BLOCK)),
                                              jnp.broadcast_to(m_o, (BLOCK, BLOCK))], axis=1)
                    probs = [jnp.exp2(s - m_both).astype(BF16) for s in scores]
                    p_sink = jnp.where(low, jnp.exp2(sink_e - m_e), jnp.exp2(sink_o - m_o))
                else:
                    scores = [s.astype(BF16) for s in scores]
                    scores = [scores[0] + bias_prev.astype(BF16), scores[1], scores[2] + bias_next.astype(BF16)]
                    probs = [jnp.exp2(s) for s in scores]
                    p_sink = jnp.exp2(jnp.where(low, sink_e, sink_o))
                acc = _dot(jnp.concatenate(probs, axis=1), vbd)
                out = acc[:, :LANES] / (acc[:, LANES:] + p_sink) * sg_ref[0, rows, lanes].astype(F32)
                o_ref[0, rows, lanes] = out.astype(o_ref.dtype)


def _attention(sink2, bounded, q, ka, kb, va, vb, sga):
    bsz, seq_len, _ = q.shape
    tq = ATTN_TILE
    q_spec = pl.BlockSpec((1, tq, ATTN_WIDTH), lambda b, i, s, f: (b, i, 0))
    kv_spec = pl.BlockSpec((1, seq_len, LANES), lambda b, i, s, f: (b, 0, 0))
    grid_spec = pltpu.PrefetchScalarGridSpec(
        num_scalar_prefetch=2,
        grid=(bsz, seq_len // tq),
        in_specs=[q_spec, kv_spec, kv_spec, kv_spec, kv_spec, q_spec],
        out_specs=q_spec,
    )
    return pl.pallas_call(
        functools.partial(_attn_kernel, n_blocks=seq_len // BLOCK),
        grid_spec=grid_spec,
        out_shape=jax.ShapeDtypeStruct((bsz, seq_len, ATTN_WIDTH), BF16),
        compiler_params=pltpu.CompilerParams(dimension_semantics=("arbitrary", "arbitrary"),
                                             vmem_limit_bytes=VMEM_LIMIT),
        name="band_attention",
    )(sink2, bounded, q, ka, kb, va, vb, sga)


def _seq_dft_kernel(a_ref, b_ref, m_ref, sg_ref, o_ref, xy_ref, *, chunk):
    combos = (
        lambda a, b: ((a[0] + a[2]) + (a[1] + a[3]), (b[0] + b[2]) + (b[1] + b[3])),
        lambda a, b: ((a[0] - a[2]) - (b[1] - b[3]), (b[0] - b[2]) + (a[1] - a[3])),
        lambda a, b: ((a[0] + a[2]) - (a[1] + a[3]), (b[0] + b[2]) - (b[1] + b[3])),
        lambda a, b: ((a[0] - a[2]) + (b[1] - b[3]), (b[0] - b[2]) - (a[1] - a[3])),
    )

    def combine(r):
        for rb in range(chunk // DFT_PREP_ROWS):
            rows = lambda q: slice(q * chunk + rb * DFT_PREP_ROWS, q * chunk + (rb + 1) * DFT_PREP_ROWS)
            a = [a_ref[0, rows(q), :].astype(F32) for q in range(RADIX)]
            b = [b_ref[0, rows(q), :].astype(F32) for q in range(RADIX)]
            x, y = combos[r](a, b)
            xy_ref[r, rows(0), :] = x.astype(xy_ref.dtype)
            xy_ref[r, rows(1), :] = y.astype(xy_ref.dtype)

    combine(0)
    for r in range(RADIX):
        f = _dot(m_ref[r], xy_ref[r])
        if r + 1 < RADIX:
            combine(r + 1)
        cols = slice(r * DFT_COLS, (r + 1) * DFT_COLS)
        o_ref[0, :, cols] = (f * sg_ref[0, :, cols].astype(F32)).astype(o_ref.dtype)


def _dft_matrix(seq_len):
    chunk = seq_len // RADIX
    split = 32
    period = seq_len // (RADIX * split)
    m = jnp.arange(chunk, dtype=jnp.int32)[None, :]
    ang_hi = ((jnp.arange(chunk // split, dtype=jnp.int32)[:, None] * m) % period).astype(F32) * (2.0 * math.pi / period)
    ang_lo = ((jnp.arange(RADIX * split, dtype=jnp.int32)[:, None] * m) % seq_len).astype(F32) * (2.0 * math.pi / seq_len)
    c_hi, s_hi = jnp.cos(ang_hi)[None, :, None, :], jnp.sin(ang_hi)[None, :, None, :]
    lo = lambda t: t.reshape(split, RADIX, chunk).transpose(1, 0, 2)[:, None, :, :]
    c_lo, s_lo = lo(jnp.cos(ang_lo)), lo(jnp.sin(ang_lo))
    cos = (c_hi * c_lo - s_hi * s_lo).reshape(RADIX, chunk, chunk)
    sin = (s_hi * c_lo + c_hi * s_lo).reshape(RADIX, chunk, chunk)
    return jnp.concatenate([cos, -sin], axis=2).astype(BF16)


def _seq_dft(a, b, dft_m, sg_perm):
    bsz, seq_len, width = a.shape
    chunk = seq_len // RADIX
    ab_spec = pl.BlockSpec((1, seq_len, DFT_COLS), lambda bi, h: (bi, 0, h))
    m_spec = pl.BlockSpec((RADIX, chunk, 2 * chunk), lambda bi, h: (0, 0, 0), pipeline_mode=pl.Buffered(1))
    io_spec = pl.BlockSpec((1, chunk, RADIX * DFT_COLS), lambda bi, h: (bi, 0, h))
    return pl.pallas_call(
        functools.partial(_seq_dft_kernel, chunk=chunk),
        grid=(bsz, width // DFT_COLS),
        in_specs=[ab_spec, ab_spec, m_spec, io_spec],
        out_specs=io_spec,
        out_shape=jax.ShapeDtypeStruct((bsz, chunk, RADIX * width), BF16),
        scratch_shapes=[pltpu.VMEM((RADIX, 2 * chunk, DFT_COLS), BF16)],
        compiler_params=pltpu.CompilerParams(dimension_semantics=("arbitrary", "arbitrary"),
                                             vmem_limit_bytes=VMEM_LIMIT),
        name="seq_dft",
    )(a, b, dft_m, sg_perm)


def _out_proj_kernel(x_ref, ma_ref, mf_ref, w_ref, o_ref, acc_ref):
    n_cb = D_MODEL // LANES
    y = x_ref[...] + _dot(ma_ref[...], w_ref[:ATTN_WIDTH, :])
    for cb in range(n_cb):
        acc_ref[cb] = y[:, cb * LANES:(cb + 1) * LANES]
    for r in range(RADIX):
        mf_r = jnp.concatenate([mf_ref[:, _perm_col(r, f):_perm_col(r, f) + DFT_COLS]
                                for f in range(0, FOURIER_WIDTH, DFT_COLS)], axis=1)
        y = _dot(mf_r, w_ref[ATTN_WIDTH:, :])
        for cb in range(n_cb):
            acc_ref[cb, pl.ds(r, ROW_TILE // RADIX, stride=RADIX), :] += y[:, cb * LANES:(cb + 1) * LANES]
    for cb in range(n_cb):
        o_ref[:, cb * LANES:(cb + 1) * LANES] = acc_ref[cb]


def _out_proj(x2d, ma, mf_perm, w_bf):
    rows = x2d.shape[0]
    tm = ROW_TILE
    row_spec = lambda width: pl.BlockSpec((tm, width), lambda i: (i, 0))
    return pl.pallas_call(
        _out_proj_kernel,
        grid=(rows // tm,),
        in_specs=[row_spec(D_MODEL), row_spec(ATTN_WIDTH),
                  pl.BlockSpec((tm // RADIX, RADIX * FOURIER_WIDTH), lambda i: (i, 0)),
                  pl.BlockSpec((D_MODEL, D_MODEL), lambda i: (0, 0))],
        out_specs=row_spec(D_MODEL),
        out_shape=jax.ShapeDtypeStruct((rows, D_MODEL), F32),
        scratch_shapes=[pltpu.VMEM((D_MODEL // LANES, tm, LANES), F32)],
        compiler_params=pltpu.CompilerParams(dimension_semantics=("arbitrary",), vmem_limit_bytes=VMEM_LIMIT),
        name="out_proj",
    )(x2d, ma, mf_perm, w_bf)


def _rope_tables(seq_len):
    half = HEAD_DIM // 2
    inv_freq = 1.0 / (ROPE_THETA ** (jnp.arange(half, dtype=F32) / half))
    ang = jnp.arange(seq_len, dtype=F32)[:, None] * inv_freq[None, :]
    cos = jnp.cos(ang)
    sin = jnp.sin(ang)
    cos_t = jnp.concatenate([cos, cos, cos, cos], axis=1)
    sin_t = jnp.concatenate([-sin, sin, -sin, sin], axis=1)
    return cos_t, sin_t


def _head_sum_matrix(width):
    head = jnp.arange(width, dtype=jnp.int32) // HEAD_DIM
    return (head[:, None] == head[None, :]).astype(BF16)


def _layer(x, consts, params):
    bsz, seq_len, _ = x.shape
    gn, w_in_bf, qg, kg, sink2, bounded, wab, w_out_bf = params
    cos_t, sin_t, hs, dft_m = consts
    x2d = x.reshape(bsz * seq_len, D_MODEL)
    q, ka, kb, va, vb, sga, a, b, sgf = _in_proj(x2d, seq_len, gn, w_in_bf, qg, kg, cos_t, sin_t, hs, wab)
    r3 = lambda t: t.reshape(bsz, seq_len, t.shape[-1])
    ma = _attention(sink2, bounded, r3(q), r3(ka), r3(kb), r3(va), r3(vb), r3(sga))
    perm_rows = seq_len // RADIX
    mf = _seq_dft(r3(a), r3(b), dft_m, sgf.reshape(bsz, perm_rows, RADIX * FOURIER_WIDTH))
    y = _out_proj(x2d, ma.reshape(bsz * seq_len, ATTN_WIDTH),
                  mf.reshape(bsz * perm_rows, RADIX * FOURIER_WIDTH), w_out_bf)
    return y.reshape(bsz, seq_len, D_MODEL)


def kernel(x_prompt, x_sample, norm_gain, w_in, q_norm_gain, k_norm_gain, sink_logit, w_fourier, w_out):
    depth = norm_gain.shape[0]
    seq_p, seq_s = x_prompt.shape[1], x_sample.shape[1]
    assert seq_p == seq_s
    consts = _rope_tables(seq_p) + (_head_sum_matrix(2 * LANES), _dft_matrix(seq_p))
    y_prompt, y_sample = x_prompt, x_sample
    for l in range(depth):
        sink2 = sink_logit[l].astype(F32) * LOG2E
        logit_bound = (HEAD_DIM ** 0.5 * LOG2E) * jnp.max(jnp.abs(q_norm_gain[l])) * jnp.max(jnp.abs(k_norm_gain[l]))
        bounded = jnp.maximum(logit_bound, jnp.max(jnp.abs(sink2))) <= MAX_UNSHIFTED_LOGIT2
        params = (norm_gain[l][None, :], w_in[l].astype(BF16),
                  jnp.tile(q_norm_gain[l], N_Q_HEADS)[None, :], jnp.tile(k_norm_gain[l], N_KV_HEADS)[None, :],
                  sink2, bounded.astype(jnp.int32)[None],
                  _fourier_weights(w_fourier[l], seq_p), w_out[l].astype(BF16))
        y_prompt = _layer(y_prompt, consts, params)
        y_sample = _layer(y_sample, consts, params)
    return (y_prompt, y_sample)
```

```python
import functools
import math

import jax
import jax.numpy as jnp
from jax.experimental import pallas as pl
from jax.experimental.pallas import tpu as pltpu

D_MODEL = 1024
HEAD_DIM = 64
N_Q_HEADS = 8
N_KV_HEADS = 2
ATTN_WIDTH = N_Q_HEADS * HEAD_DIM
KV_WIDTH = N_KV_HEADS * HEAD_DIM
FOURIER_WIDTH = D_MODEL - ATTN_WIDTH
N_GROUPS = 4
GROUP_DIM = FOURIER_WIDTH // N_GROUPS
IN_WIDTH = 2 * ATTN_WIDTH + 2 * KV_WIDTH + 2 * FOURIER_WIDTH
Q_OFF = 0
K_OFF = ATTN_WIDTH
V_OFF = K_OFF + KV_WIDTH
GA_OFF = V_OFF + KV_WIDTH
U_OFF = GA_OFF + ATTN_WIDTH
GF_OFF = U_OFF + FOURIER_WIDTH
BLOCK = 128
ROPE_THETA = 10000.0
EPS = 1e-6
NEG = -1e30
LANES = 128
RADIX = 4
DFT_COLS = 256
DFT_PREP_ROWS = 128
LOG2E = math.log2(math.e)
MAX_UNSHIFTED_LOGIT2 = 100.0

ROW_TILE = 1024
SUB_ROWS = 512
FUSED_ROW_TILE = 512
ATTN_TILE = 512
VMEM_LIMIT = 48 * 1024 * 1024

BF16 = jnp.bfloat16
F32 = jnp.float32


def _dot(a, b):
    return jnp.dot(a, b, preferred_element_type=F32)


def _dot_nt(a, b):
    return jax.lax.dot_general(a, b, (((1,), (1,)), ((), ())), preferred_element_type=F32)


def _silu(x):
    return x / (1.0 + jnp.exp(-x))


def _rotate_half(t, first_half):
    width = t.shape[-1]
    fwd = pltpu.roll(t, HEAD_DIM // 2, axis=1)
    bwd = pltpu.roll(t, width - HEAD_DIM // 2, axis=1)
    return jnp.where(first_half, bwd, fwd)


def _fourier_weight_kernel(cs_ref, w_ref, o_ref):
    for g in range(N_GROUPS):
        w = w_ref[g]
        c = jnp.dot(cs_ref[0], w, preferred_element_type=F32, precision=jax.lax.Precision.HIGHEST)
        s = jnp.dot(cs_ref[1], w, preferred_element_type=F32, precision=jax.lax.Precision.HIGHEST)
        o_ref[g] = jnp.concatenate([c, s], axis=1).astype(o_ref.dtype)


def _fourier_weights(w_four, seq_len):
    idx = (jnp.arange(GROUP_DIM, dtype=jnp.int32)[:, None] * jnp.arange(GROUP_DIM, dtype=jnp.int32)[None, :]) % GROUP_DIM
    ang = idx.astype(F32) * (2.0 * math.pi / GROUP_DIM)
    scale = 1.0 / math.sqrt(seq_len * GROUP_DIM)
    cs = jnp.stack([jnp.cos(ang), jnp.sin(ang)]) * scale
    return pl.pallas_call(
        _fourier_weight_kernel,
        out_shape=jax.ShapeDtypeStruct((N_GROUPS, GROUP_DIM, 2 * GROUP_DIM), BF16),
        name="fourier_weight_prep",
    )(cs, w_four)


def _perm_col(r, feature):
    return ((feature // DFT_COLS) * RADIX + r) * DFT_COLS + feature % DFT_COLS


def _in_proj_rows(x, sub, w_ref, qcos_ref, qsin_ref, kcos_ref, ksin_ref, hs_ref, wab_ref,
                  q_ref, ka_ref, kb_ref, va_ref, vb_ref, sga_ref, a_ref, b_ref, sgf_ref, perm_ref):
    rows = slice(sub * SUB_ROWS, (sub + 1) * SUB_ROWS)
    xb = x.astype(BF16)
    ms = jnp.mean(x * x, axis=-1, keepdims=True)
    r = jnp.broadcast_to(jax.lax.rsqrt(ms + EPS), (SUB_ROWS, LANES))
    eps_z = jnp.broadcast_to(EPS * (ms + EPS), (SUB_ROWS, LANES))
    wide = lambda t, width: jnp.concatenate([t] * (width // LANES), axis=1)

    lane = jax.lax.broadcasted_iota(jnp.int32, (1, LANES), 1)
    first_half_pair = (lane % HEAD_DIM) < (HEAD_DIM // 2)
    low_head = lane < HEAD_DIM
    hs = hs_ref[...]

    def head_mean_sq(z):
        return _dot((z * z).astype(BF16), hs)

    def silu_of_scaled(z, width):
        h = z * wide(0.5 * r, width)
        return h + h * jnp.tanh(h)

    zq = _dot(xb, w_ref[:, Q_OFF:Q_OFF + ATTN_WIDTH])
    half = ATTN_WIDTH // 2
    ssq = jnp.concatenate([head_mean_sq(zq[:, :half]), head_mean_sq(zq[:, half:])], axis=1)
    qr = (zq * wide(qcos_ref[rows, :], ATTN_WIDTH)
          + _rotate_half(zq, wide(first_half_pair, ATTN_WIDTH)) * wide(qsin_ref[rows, :], ATTN_WIDTH))
    q_ref[rows, :] = (qr * jax.lax.rsqrt(ssq + wide(eps_z, ATTN_WIDTH))).astype(q_ref.dtype)

    zkv = _dot(xb, w_ref[:, K_OFF:K_OFF + 2 * KV_WIDTH])
    zk = zkv[:, :KV_WIDTH]
    ssk = head_mean_sq(zkv)[:, :KV_WIDTH]
    kr = (zk * kcos_ref[rows, :] + _rotate_half(zk, first_half_pair) * ksin_ref[rows, :]) * jax.lax.rsqrt(ssk + eps_z)
    kr_sw = pltpu.roll(kr, HEAD_DIM, axis=1)
    ka_ref[rows, :] = jnp.where(low_head, kr, kr_sw).astype(ka_ref.dtype)
    kb_ref[rows, :] = jnp.where(low_head, kr_sw, kr).astype(kb_ref.dtype)
    zv = zkv[:, KV_WIDTH:] * r
    zv_sw = pltpu.roll(zv, HEAD_DIM, axis=1)
    va_ref[rows, :] = jnp.where(low_head, zv, zv_sw).astype(va_ref.dtype)
    vb_ref[rows, :] = jnp.where(low_head, zv_sw, zv).astype(vb_ref.dtype)

    zga = _dot(xb, w_ref[:, GA_OFF:GA_OFF + ATTN_WIDTH])
    sga_ref[rows, :] = silu_of_scaled(zga, ATTN_WIDTH).astype(sga_ref.dtype)
    sgf = silu_of_scaled(_dot(xb, w_ref[:, GF_OFF:GF_OFF + FOURIER_WIDTH]), FOURIER_WIDTH)
    for cb in range(FOURIER_WIDTH // LANES):
        perm_ref[sub, cb] = sgf[:, cb * LANES:(cb + 1) * LANES]
    perm_rows = slice(sub * (SUB_ROWS // RADIX), (sub + 1) * (SUB_ROWS // RADIX))
    for rr in range(RADIX):
        for cb in range(FOURIER_WIDTH // LANES):
            col = _perm_col(rr, cb * LANES)
            sgf_ref[perm_rows, col:col + LANES] = (
                perm_ref[sub, cb, pl.ds(rr, SUB_ROWS // RADIX, stride=RADIX), :].astype(sgf_ref.dtype))

    zu = (_dot(xb, w_ref[:, U_OFF:U_OFF + FOURIER_WIDTH]) * wide(r, FOURIER_WIDTH)).astype(BF16)
    for g in range(N_GROUPS):
        ab = _dot(zu[:, g * GROUP_DIM:(g + 1) * GROUP_DIM], wab_ref[g])
        a_ref[rows, g * GROUP_DIM:(g + 1) * GROUP_DIM] = ab[:, :GROUP_DIM].astype(a_ref.dtype)
        b_ref[rows, g * GROUP_DIM:(g + 1) * GROUP_DIM] = ab[:, GROUP_DIM:].astype(b_ref.dtype)


N_IN_PROJ_INPUTS = 7
N_IN_PROJ_OUTPUTS = 9


def _in_proj_kernel(x_ref, *refs):
    for sub in range(x_ref.shape[0] // SUB_ROWS):
        _in_proj_rows(x_ref[sub * SUB_ROWS:(sub + 1) * SUB_ROWS, :], sub, *refs)


def _row_spec(tm, width):
    return pl.BlockSpec((tm, width), lambda i: (i, 0))


def _resident_spec(shape):
    return pl.BlockSpec(shape, lambda i: (0,) * len(shape), pipeline_mode=pl.Buffered(1))


def _in_proj_specs(tm, rows, seq_len):
    steps_per_seq = seq_len // tm
    tab_spec = pl.BlockSpec((tm, LANES), lambda i: (i % steps_per_seq, 0))
    in_specs = [_resident_spec((D_MODEL, IN_WIDTH)), tab_spec, tab_spec, tab_spec, tab_spec,
                _resident_spec((2 * LANES, 2 * LANES)), _resident_spec((N_GROUPS, GROUP_DIM, 2 * GROUP_DIM))]
    out_widths = (ATTN_WIDTH, LANES, LANES, LANES, LANES, ATTN_WIDTH, FOURIER_WIDTH, FOURIER_WIDTH)
    out_specs = [_row_spec(tm, w) for w in out_widths] + [_row_spec(tm // RADIX, RADIX * FOURIER_WIDTH)]
    out_shape = ([jax.ShapeDtypeStruct((rows, w), BF16) for w in out_widths]
                 + [jax.ShapeDtypeStruct((rows // RADIX, RADIX * FOURIER_WIDTH), BF16)])
    scratch = pltpu.VMEM((tm // SUB_ROWS, FOURIER_WIDTH // LANES, SUB_ROWS, LANES), F32)
    return in_specs, out_specs, out_shape, scratch


def _in_proj(x2d, seq_len, in_w):
    rows = x2d.shape[0]
    tm = ROW_TILE
    in_specs, out_specs, out_shape, scratch = _in_proj_specs(tm, rows, seq_len)
    return pl.pallas_call(
        _in_proj_kernel,
        grid=(rows // tm,),
        in_specs=[_row_spec(tm, D_MODEL)] + in_specs,
        out_specs=out_specs,
        out_shape=out_shape,
        scratch_shapes=[scratch],
        compiler_params=pltpu.CompilerParams(dimension_semantics=("arbitrary",), vmem_limit_bytes=VMEM_LIMIT),
        name="in_proj",
    )(x2d, *in_w)


def _attn_kernel(sink_ref, bounded_ref, q_ref, ka_ref, kb_ref, va_ref, vb_ref, sg_ref, o_ref, *, n_blocks):
    refs = (sink_ref, q_ref, ka_ref, kb_ref, va_ref, vb_ref, sg_ref, o_ref)

    @pl.when(bounded_ref[0] == 1)
    def _():
        _attn_body(*refs, n_blocks=n_blocks, shift=False)

    @pl.when(bounded_ref[0] == 0)
    def _():
        _attn_body(*refs, n_blocks=n_blocks, shift=True)


def _attn_body(sink_ref, q_ref, ka_ref, kb_ref, va_ref, vb_ref, sg_ref, o_ref, *, n_blocks, shift):
    i = pl.program_id(1)
    blocks_per_step = ATTN_TILE // BLOCK
    row = jax.lax.broadcasted_iota(jnp.int32, (BLOCK, 2 * BLOCK), 0)
    col = jax.lax.broadcasted_iota(jnp.int32, (BLOCK, 2 * BLOCK), 1) % BLOCK
    lane = jax.lax.broadcasted_iota(jnp.int32, (1, LANES), 1)
    low = lane < HEAD_DIM
    zero = jnp.zeros((), BF16)
    lane_full = jax.lax.broadcasted_iota(jnp.int32, (BLOCK, LANES), 1)
    ones_low = jnp.where(lane_full < HEAD_DIM, 1.0, 0.0).astype(BF16)
    ones_high = jnp.where(lane_full < HEAD_DIM, 0.0, 1.0).astype(BF16)

    for jb in range(blocks_per_step):
        ib = i * blocks_per_step + jb
        rows = slice(jb * BLOCK, (jb + 1) * BLOCK)
        edge_prev = jnp.where(ib == 0, NEG, 0.0)
        edge_next = jnp.where(ib == n_blocks - 1, NEG, 0.0)
        bias_prev = jnp.where(col >= row, 0.0, NEG) + edge_prev
        bias_next = jnp.where(col <= row, 0.0, NEG) + edge_next
        starts = [pl.multiple_of(jnp.clip(ib + c, 0, n_blocks - 1) * BLOCK, BLOCK) for c in (-1, 0, 1)]
        for kvh, (k_ref, v_ref) in enumerate(((ka_ref, va_ref), (kb_ref, vb_ref))):
            kbd, vbd = [], []
            for st in starts:
                kblk = k_ref[0, pl.ds(st, BLOCK), :]
                vblk = v_ref[0, pl.ds(st, BLOCK), :]
                kbd.append(jnp.concatenate([jnp.where(low, kblk, zero), jnp.where(low, zero, kblk)], axis=0))
                vbd.append(jnp.concatenate([
                    jnp.concatenate([jnp.where(low, vblk, zero), ones_low], axis=1),
                    jnp.concatenate([jnp.where(low, zero, vblk), ones_high], axis=1)], axis=0))
            vbd = jnp.concatenate(vbd, axis=0)
            for pair in range(2):
                pidx = kvh * 2 + pair
                lanes = slice(pidx * LANES, (pidx + 1) * LANES)
                qp = q_ref[0, rows, lanes]
                scores = [_dot_nt(qp, kb) for kb in kbd]
                sink_e = sink_ref[2 * pidx]
                sink_o = sink_ref[2 * pidx + 1]
                if shift:
                    scores = [scores[0] + bias_prev, scores[1], scores[2] + bias_next]
                    smax = jnp.maximum(jnp.maximum(scores[0], scores[1]), scores[2])
                    m_e = jnp.maximum(jnp.max(smax[:, :BLOCK], axis=-1, keepdims=True), sink_e)
                    m_o = jnp.maximum(jnp.max(smax[:, BLOCK:], axis=-1, keepdims=True), sink_o)
                    m_both = jnp.concatenate([jnp.broadcast_to(m_e, (BLOCK, BLOCK)),
                                              jnp.broadcast_to(m_o, (BLOCK, BLOCK))], axis=1)
                    probs = [jnp.exp2(s - m_both).astype(BF16) for s in scores]
                    p_sink = jnp.where(low, jnp.exp2(sink_e - m_e), jnp.exp2(sink_o - m_o))
                else:
                    scores = [s.astype(BF16) for s in scores]
                    scores = [scores[0] + bias_prev.astype(BF16), scores[1], scores[2] + bias_next.astype(BF16)]
                    probs = [jnp.exp2(s) for s in scores]
                    p_sink = jnp.exp2(jnp.where(low, sink_e, sink_o))
                acc = _dot(jnp.concatenate(probs, axis=1), vbd)
                out = acc[:, :LANES] / (acc[:, LANES:] + p_sink) * sg_ref[0, rows, lanes].astype(F32)
                o_ref[0, rows, lanes] = out.astype(o_ref.dtype)


def _attention(sink2, bounded, q, ka, kb, va, vb, sga):
    bsz, seq_len, _ = q.shape
    tq = ATTN_TILE
    q_spec = pl.BlockSpec((1, tq, ATTN_WIDTH), lambda b, i, s, f: (b, i, 0))
    kv_spec = pl.BlockSpec((1, seq_len, LANES), lambda b, i, s, f: (b, 0, 0))
    grid_spec = pltpu.PrefetchScalarGridSpec(
        num_scalar_prefetch=2,
        grid=(bsz, seq_len // tq),
        in_specs=[q_spec, kv_spec, kv_spec, kv_spec, kv_spec, q_spec],
        out_specs=q_spec,
    )
    return pl.pallas_call(
        functools.partial(_attn_kernel, n_blocks=seq_len // BLOCK),
        grid_spec=grid_spec,
        out_shape=jax.ShapeDtypeStruct((bsz, seq_len, ATTN_WIDTH), BF16),
        compiler_params=pltpu.CompilerParams(dimension_semantics=("arbitrary", "arbitrary"),
                                             vmem_limit_bytes=VMEM_LIMIT),
        name="band_attention",
    )(sink2, bounded, q, ka, kb, va, vb, sga)


def _seq_dft_kernel(a_ref, b_ref, m_ref, sg_ref, o_ref, xy_ref, *, chunk):
    combos = (
        lambda a, b: ((a[0] + a[2]) + (a[1] + a[3]), (b[0] + b[2]) + (b[1] + b[3])),
        lambda a, b: ((a[0] - a[2]) - (b[1] - b[3]), (b[0] - b[2]) + (a[1] - a[3])),
        lambda a, b: ((a[0] + a[2]) - (a[1] + a[3]), (b[0] + b[2]) - (b[1] + b[3])),
        lambda a, b: ((a[0] - a[2]) + (b[1] - b[3]), (b[0] - b[2]) - (a[1] - a[3])),
    )

    def combine(r):
        for rb in range(chunk // DFT_PREP_ROWS):
            rows = lambda q: slice(q * chunk + rb * DFT_PREP_ROWS, q * chunk + (rb + 1) * DFT_PREP_ROWS)
            a = [a_ref[0, rows(q), :].astype(F32) for q in range(RADIX)]
            b = [b_ref[0, rows(q), :].astype(F32) for q in range(RADIX)]
            x, y = combos[r](a, b)
            xy_ref[r, rows(0), :] = x.astype(xy_ref.dtype)
            xy_ref[r, rows(1), :] = y.astype(xy_ref.dtype)

    combine(0)
    for r in range(RADIX):
        f = _dot(m_ref[r], xy_ref[r])
        if r + 1 < RADIX:
            combine(r + 1)
        cols = slice(r * DFT_COLS, (r + 1) * DFT_COLS)
        o_ref[0, :, cols] = (f * sg_ref[0, :, cols].astype(F32)).astype(o_ref.dtype)


def _dft_matrix(seq_len):
    chunk = seq_len // RADIX
    split = 32
    period = seq_len // (RADIX * split)
    m = jnp.arange(chunk, dtype=jnp.int32)[None, :]
    ang_hi = ((jnp.arange(chunk // split, dtype=jnp.int32)[:, None] * m) % period).astype(F32) * (2.0 * math.pi / period)
    ang_lo = ((jnp.arange(RADIX * split, dtype=jnp.int32)[:, None] * m) % seq_len).astype(F32) * (2.0 * math.pi / seq_len)
    c_hi, s_hi = jnp.cos(ang_hi)[None, :, None, :], jnp.sin(ang_hi)[None, :, None, :]
    lo = lambda t: t.reshape(split, RADIX, chunk).transpose(1, 0, 2)[:, None, :, :]
    c_lo, s_lo = lo(jnp.cos(ang_lo)), lo(jnp.sin(ang_lo))
    cos = (c_hi * c_lo - s_hi * s_lo).reshape(RADIX, chunk, chunk)
    sin = (s_hi * c_lo + c_hi * s_lo).reshape(RADIX, chunk, chunk)
    return jnp.concatenate([cos, -sin], axis=2).astype(BF16)


def _seq_dft(a, b, dft_m, sg_perm):
    bsz, seq_len, width = a.shape
    chunk = seq_len // RADIX
    ab_spec = pl.BlockSpec((1, seq_len, DFT_COLS), lambda bi, h: (bi, 0, h))
    m_spec = pl.BlockSpec((RADIX, chunk, 2 * chunk), lambda bi, h: (0, 0, 0), pipeline_mode=pl.Buffered(1))
    io_spec = pl.BlockSpec((1, chunk, RADIX * DFT_COLS), lambda bi, h: (bi, 0, h))
    return pl.pallas_call(
        functools.partial(_seq_dft_kernel, chunk=chunk),
        grid=(bsz, width // DFT_COLS),
        in_specs=[ab_spec, ab_spec, m_spec, io_spec],
        out_specs=io_spec,
        out_shape=jax.ShapeDtypeStruct((bsz, chunk, RADIX * width), BF16),
        scratch_shapes=[pltpu.VMEM((RADIX, 2 * chunk, DFT_COLS), BF16)],
        compiler_params=pltpu.CompilerParams(dimension_semantics=("arbitrary", "arbitrary"),
                                             vmem_limit_bytes=VMEM_LIMIT),
        name="seq_dft",
    )(a, b, dft_m, sg_perm)


def _out_proj_rows(sub, x_ref, ma_ref, mf_ref, w_ref, o_ref, acc_ref):
    n_cb = D_MODEL // LANES
    rows = slice(sub * SUB_ROWS, (sub + 1) * SUB_ROWS)
    perm_rows = slice(sub * (SUB_ROWS // RADIX), (sub + 1) * (SUB_ROWS // RADIX))
    y = x_ref[rows, :] + _dot(ma_ref[rows, :], w_ref[:ATTN_WIDTH, :])
    for cb in range(n_cb):
        acc_ref[sub, cb] = y[:, cb * LANES:(cb + 1) * LANES]
    for r in range(RADIX):
        mf_r = jnp.concatenate([mf_ref[perm_rows, _perm_col(r, f):_perm_col(r, f) + DFT_COLS]
                                for f in range(0, FOURIER_WIDTH, DFT_COLS)], axis=1)
        y = _dot(mf_r, w_ref[ATTN_WIDTH:, :])
        for cb in range(n_cb):
            acc_ref[sub, cb, pl.ds(r, SUB_ROWS // RADIX, stride=RADIX), :] += y[:, cb * LANES:(cb + 1) * LANES]
    for cb in range(n_cb):
        o_ref[rows, cb * LANES:(cb + 1) * LANES] = acc_ref[sub, cb]


def _out_proj_kernel(x_ref, ma_ref, mf_ref, w_ref, o_ref, acc_ref):
    for sub in range(x_ref.shape[0] // SUB_ROWS):
        _out_proj_rows(sub, x_ref, ma_ref, mf_ref, w_ref, o_ref, acc_ref)


def _out_in_proj_kernel(x_ref, ma_ref, mf_ref, wo_ref, *refs):
    in_refs = refs[:N_IN_PROJ_INPUTS]
    y_ref = refs[N_IN_PROJ_INPUTS]
    out_refs = refs[N_IN_PROJ_INPUTS + 1:N_IN_PROJ_INPUTS + 1 + N_IN_PROJ_OUTPUTS]
    acc_ref, perm_ref = refs[-2:]
    for sub in range(x_ref.shape[0] // SUB_ROWS):
        _out_proj_rows(sub, x_ref, ma_ref, mf_ref, wo_ref, y_ref, acc_ref)
        _in_proj_rows(y_ref[sub * SUB_ROWS:(sub + 1) * SUB_ROWS, :], sub, *in_refs, *out_refs, perm_ref)


def _out_proj_specs(tm):
    in_specs = [_row_spec(tm, D_MODEL), _row_spec(tm, ATTN_WIDTH), _row_spec(tm // RADIX, RADIX * FOURIER_WIDTH),
                _resident_spec((D_MODEL, D_MODEL))]
    scratch = pltpu.VMEM((tm // SUB_ROWS, D_MODEL // LANES, SUB_ROWS, LANES), F32)
    return in_specs, scratch


def _out_proj(x2d, ma, mf_perm, w_bf):
    rows = x2d.shape[0]
    tm = ROW_TILE
    in_specs, scratch = _out_proj_specs(tm)
    return pl.pallas_call(
        _out_proj_kernel,
        grid=(rows // tm,),
        in_specs=in_specs,
        out_specs=_row_spec(tm, D_MODEL),
        out_shape=jax.ShapeDtypeStruct((rows, D_MODEL), F32),
        scratch_shapes=[scratch],
        compiler_params=pltpu.CompilerParams(dimension_semantics=("arbitrary",), vmem_limit_bytes=VMEM_LIMIT),
        name="out_proj",
    )(x2d, ma, mf_perm, w_bf)


def _out_in_proj(x2d, ma, mf_perm, w_bf, seq_len, in_w):
    rows = x2d.shape[0]
    tm = FUSED_ROW_TILE
    o_in_specs, acc_scratch = _out_proj_specs(tm)
    i_in_specs, i_out_specs, i_out_shape, perm_scratch = _in_proj_specs(tm, rows, seq_len)
    outs = pl.pallas_call(
        _out_in_proj_kernel,
        grid=(rows // tm,),
        in_specs=o_in_specs + i_in_specs,
        out_specs=[_row_spec(tm, D_MODEL)] + i_out_specs,
        out_shape=[jax.ShapeDtypeStruct((rows, D_MODEL), F32)] + i_out_shape,
        scratch_shapes=[acc_scratch, perm_scratch],
        compiler_params=pltpu.CompilerParams(dimension_semantics=("arbitrary",), vmem_limit_bytes=VMEM_LIMIT),
        name="out_in_proj",
    )(x2d, ma, mf_perm, w_bf, *in_w)
    return outs[0], outs[1:]


def _rope_tables(seq_len):
    half = HEAD_DIM // 2
    inv_freq = 1.0 / (ROPE_THETA ** (jnp.arange(half, dtype=F32) / half))
    ang = jnp.arange(seq_len, dtype=F32)[:, None] * inv_freq[None, :]
    cos = jnp.cos(ang)
    sin = jnp.sin(ang)
    cos_t = jnp.concatenate([cos, cos, cos, cos], axis=1)
    sin_t = jnp.concatenate([-sin, sin, -sin, sin], axis=1)
    return cos_t, sin_t


def _gained_rope(cos_t, sin_t, gain, scale):
    g = jnp.tile(gain.astype(F32), LANES // HEAD_DIM)[None, :]
    g_rot = jnp.tile(jnp.roll(gain.astype(F32), HEAD_DIM // 2), LANES // HEAD_DIM)[None, :]
    return cos_t * (g * scale), sin_t * (g_rot * scale)


def _head_mean_matrix(width):
    head = jnp.arange(width, dtype=jnp.int32) // HEAD_DIM
    return jnp.where(head[:, None] == head[None, :], 1.0 / HEAD_DIM, 0.0).astype(BF16)


def _mixers(proj, bsz, seq_len, dft_m, sink2, bounded):
    q, ka, kb, va, vb, sga, a, b, sgf = proj
    r3 = lambda t: t.reshape(bsz, seq_len, t.shape[-1])
    ma = _attention(sink2, bounded, r3(q), r3(ka), r3(kb), r3(va), r3(vb), r3(sga))
    perm_rows = seq_len // RADIX
    mf = _seq_dft(r3(a), r3(b), dft_m, sgf.reshape(bsz, perm_rows, RADIX * FOURIER_WIDTH))
    return ma.reshape(bsz * seq_len, ATTN_WIDTH), mf.reshape(bsz * perm_rows, RADIX * FOURIER_WIDTH)


def _trunk(x, layers, dft_m):
    bsz, seq_len, _ = x.shape
    x2d = x.reshape(bsz * seq_len, D_MODEL)
    proj = _in_proj(x2d, seq_len, layers[0]["in_w"])
    for l, layer in enumerate(layers):
        ma, mf = _mixers(proj, bsz, seq_len, dft_m, layer["sink2"], layer["bounded"])
        if l + 1 < len(layers):
            x2d, proj = _out_in_proj(x2d, ma, mf, layer["w_out"], seq_len, layers[l + 1]["in_w"])
        else:
            x2d = _out_proj(x2d, ma, mf, layer["w_out"])
    return x2d.reshape(bsz, seq_len, D_MODEL)


def kernel(x_prompt, x_sample, norm_gain, w_in, q_norm_gain, k_norm_gain, sink_logit, w_fourier, w_out):
    depth = norm_gain.shape[0]
    seq_p, seq_s = x_prompt.shape[1], x_sample.shape[1]
    assert seq_p == seq_s
    cos_t, sin_t = _rope_tables(seq_p)
    hs = _head_mean_matrix(2 * LANES)
    layers = []
    for l in range(depth):
        sink2 = sink_logit[l].astype(F32) * LOG2E
        logit_bound = (HEAD_DIM ** 0.5 * LOG2E) * jnp.max(jnp.abs(q_norm_gain[l])) * jnp.max(jnp.abs(k_norm_gain[l]))
        bounded = jnp.maximum(logit_bound, jnp.max(jnp.abs(sink2))) <= MAX_UNSHIFTED_LOGIT2
        in_w = ((norm_gain[l].astype(F32)[:, None] * w_in[l]).astype(BF16),
                *_gained_rope(cos_t, sin_t, q_norm_gain[l], HEAD_DIM ** -0.5 * LOG2E),
                *_gained_rope(cos_t, sin_t, k_norm_gain[l], 1.0),
                hs, _fourier_weights(w_fourier[l], seq_p))
        layers.append(dict(in_w=in_w, sink2=sink2, bounded=bounded.astype(jnp.int32)[None],
                           w_out=w_out[l].astype(BF16)))
    dft_m = _dft_matrix(seq_p)
    return (_trunk(x_prompt, layers, dft_m), _trunk(x_sample, layers, dft_m))
```

```python
import functools
import math

import jax
import jax.numpy as jnp
from jax.experimental import pallas as pl
from jax.experimental.pallas import tpu as pltpu

D_MODEL = 1024
HEAD_DIM = 64
N_Q_HEADS = 8
N_KV_HEADS = 2
ATTN_WIDTH = N_Q_HEADS * HEAD_DIM
KV_WIDTH = N_KV_HEADS * HEAD_DIM
FOURIER_WIDTH = D_MODEL - ATTN_WIDTH
N_GROUPS = 4
GROUP_DIM = FOURIER_WIDTH // N_GROUPS
IN_WIDTH = 2 * ATTN_WIDTH + 2 * KV_WIDTH + 2 * FOURIER_WIDTH
Q_OFF = 0
K_OFF = ATTN_WIDTH
V_OFF = K_OFF + KV_WIDTH
GA_OFF = V_OFF + KV_WIDTH
U_OFF = GA_OFF + ATTN_WIDTH
GF_OFF = U_OFF + FOURIER_WIDTH
BLOCK = 128
ROPE_THETA = 10000.0
EPS = 1e-6
NEG = -1e30
LANES = 128
RADIX = 4
DFT_COLS = 256
DFT_PREP_ROWS = 128
LOG2E = math.log2(math.e)
MAX_UNSHIFTED_LOGIT2 = 100.0

ROW_TILE = 1024
SUB_ROWS = 512
FUSED_ROW_TILE = 1024
ATTN_TILE = 1024
VMEM_LIMIT = 48 * 1024 * 1024
FUSED_VMEM_LIMIT = 54 * 1024 * 1024

BF16 = jnp.bfloat16
F32 = jnp.float32


def _dot(a, b):
    return jnp.dot(a, b, preferred_element_type=F32)


def _dot_nt(a, b):
    return jax.lax.dot_general(a, b, (((1,), (1,)), ((), ())), preferred_element_type=F32)


def _silu(x):
    return x / (1.0 + jnp.exp(-x))


def _rotate_half(t, first_half):
    width = t.shape[-1]
    fwd = pltpu.roll(t, HEAD_DIM // 2, axis=1)
    bwd = pltpu.roll(t, width - HEAD_DIM // 2, axis=1)
    return jnp.where(first_half, bwd, fwd)


def _fourier_weight_kernel(cs_ref, w_ref, o_ref):
    for g in range(N_GROUPS):
        w = w_ref[g]
        c = jnp.dot(cs_ref[0], w, preferred_element_type=F32, precision=jax.lax.Precision.HIGHEST)
        s = jnp.dot(cs_ref[1], w, preferred_element_type=F32, precision=jax.lax.Precision.HIGHEST)
        o_ref[g] = jnp.concatenate([c, s], axis=1).astype(o_ref.dtype)


def _fourier_weights(w_four, seq_len):
    idx = (jnp.arange(GROUP_DIM, dtype=jnp.int32)[:, None] * jnp.arange(GROUP_DIM, dtype=jnp.int32)[None, :]) % GROUP_DIM
    ang = idx.astype(F32) * (2.0 * math.pi / GROUP_DIM)
    scale = 1.0 / math.sqrt(seq_len * GROUP_DIM)
    cs = jnp.stack([jnp.cos(ang), jnp.sin(ang)]) * scale
    return pl.pallas_call(
        _fourier_weight_kernel,
        out_shape=jax.ShapeDtypeStruct((N_GROUPS, GROUP_DIM, 2 * GROUP_DIM), BF16),
        name="fourier_weight_prep",
    )(cs, w_four)


def _perm_col(r, feature):
    return ((feature // DFT_COLS) * RADIX + r) * DFT_COLS + feature % DFT_COLS


def _in_proj_rows(x, sub, gn_ref, w_ref, cos_ref, sin_ref, gains_ref, hs_ref, wab_ref,
                  q_ref, ka_ref, kb_ref, va_ref, vb_ref, sga_ref, a_ref, b_ref, sgf_ref, perm_ref):
    rows = slice(sub * SUB_ROWS, (sub + 1) * SUB_ROWS)
    xb = (x * gn_ref[...]).astype(BF16)
    cos = cos_ref[rows, :]
    sin = sin_ref[rows, :]
    qcos, qsin = cos * gains_ref[0:1, :], sin * gains_ref[1:2, :]
    kcos, ksin = cos * gains_ref[2:3, :], sin * gains_ref[3:4, :]
    ms = jnp.mean(x * x, axis=-1, keepdims=True)
    r = jnp.broadcast_to(jax.lax.rsqrt(ms + EPS), (SUB_ROWS, LANES))
    eps_z = jnp.broadcast_to(EPS * (ms + EPS), (SUB_ROWS, LANES))
    wide = lambda t, width: jnp.concatenate([t] * (width // LANES), axis=1)

    lane = jax.lax.broadcasted_iota(jnp.int32, (1, LANES), 1)
    first_half_pair = (lane % HEAD_DIM) < (HEAD_DIM // 2)
    low_head = lane < HEAD_DIM
    hs = hs_ref[...]

    def head_mean_sq(z):
        return _dot((z * z).astype(BF16), hs)

    def silu_of_scaled(z, width):
        h = z * wide(0.5 * r, width)
        return h + h * jnp.tanh(h)

    zq = _dot(xb, w_ref[:, Q_OFF:Q_OFF + ATTN_WIDTH])
    half = ATTN_WIDTH // 2
    ssq = jnp.concatenate([head_mean_sq(zq[:, :half]), head_mean_sq(zq[:, half:])], axis=1)
    qr = (zq * wide(qcos, ATTN_WIDTH)
          + _rotate_half(zq, wide(first_half_pair, ATTN_WIDTH)) * wide(qsin, ATTN_WIDTH))
    q_ref[rows, :] = (qr * jax.lax.rsqrt(ssq + wide(eps_z, ATTN_WIDTH))).astype(q_ref.dtype)

    zkv = _dot(xb, w_ref[:, K_OFF:K_OFF + 2 * KV_WIDTH])
    zk = zkv[:, :KV_WIDTH]
    ssk = head_mean_sq(zkv)[:, :KV_WIDTH]
    kr = (zk * kcos + _rotate_half(zk, first_half_pair) * ksin) * jax.lax.rsqrt(ssk + eps_z)
    kr_sw = pltpu.roll(kr, HEAD_DIM, axis=1)
    ka_ref[rows, :] = jnp.where(low_head, kr, kr_sw).astype(ka_ref.dtype)
    kb_ref[rows, :] = jnp.where(low_head, kr_sw, kr).astype(kb_ref.dtype)
    zv = zkv[:, KV_WIDTH:] * r
    zv_sw = pltpu.roll(zv, HEAD_DIM, axis=1)
    va_ref[rows, :] = jnp.where(low_head, zv, zv_sw).astype(va_ref.dtype)
    vb_ref[rows, :] = jnp.where(low_head, zv_sw, zv).astype(vb_ref.dtype)

    zga = _dot(xb, w_ref[:, GA_OFF:GA_OFF + ATTN_WIDTH])
    sga_ref[rows, :] = silu_of_scaled(zga, ATTN_WIDTH).astype(sga_ref.dtype)
    sgf = silu_of_scaled(_dot(xb, w_ref[:, GF_OFF:GF_OFF + FOURIER_WIDTH]), FOURIER_WIDTH)
    for cb in range(FOURIER_WIDTH // LANES):
        perm_ref[sub, cb] = sgf[:, cb * LANES:(cb + 1) * LANES]
    perm_rows = slice(sub * (SUB_ROWS // RADIX), (sub + 1) * (SUB_ROWS // RADIX))
    for rr in range(RADIX):
        for cb in range(FOURIER_WIDTH // LANES):
            col = _perm_col(rr, cb * LANES)
            sgf_ref[perm_rows, col:col + LANES] = (
                perm_ref[sub, cb, pl.ds(rr, SUB_ROWS // RADIX, stride=RADIX), :].astype(sgf_ref.dtype))

    zu = (_dot(xb, w_ref[:, U_OFF:U_OFF + FOURIER_WIDTH]) * wide(r, FOURIER_WIDTH)).astype(BF16)
    for g in range(N_GROUPS):
        ab = _dot(zu[:, g * GROUP_DIM:(g + 1) * GROUP_DIM], wab_ref[g])
        a_ref[rows, g * GROUP_DIM:(g + 1) * GROUP_DIM] = ab[:, :GROUP_DIM].astype(a_ref.dtype)
        b_ref[rows, g * GROUP_DIM:(g + 1) * GROUP_DIM] = ab[:, GROUP_DIM:].astype(b_ref.dtype)


N_IN_PROJ_INPUTS = 7
N_IN_PROJ_OUTPUTS = 9


def _in_proj_kernel(x_ref, *refs):
    for sub in range(x_ref.shape[0] // SUB_ROWS):
        _in_proj_rows(x_ref[sub * SUB_ROWS:(sub + 1) * SUB_ROWS, :], sub, *refs)


def _row_spec(tm, width):
    return pl.BlockSpec((tm, width), lambda i: (i, 0))


def _resident_spec(shape):
    return pl.BlockSpec(shape, lambda i: (0,) * len(shape), pipeline_mode=pl.Buffered(1))


def _in_proj_specs(tm, rows, seq_len):
    steps_per_seq = seq_len // tm
    tab_spec = pl.BlockSpec((tm, LANES), lambda i: (i % steps_per_seq, 0))
    in_specs = [_resident_spec((1, D_MODEL)), _resident_spec((D_MODEL, IN_WIDTH)), tab_spec, tab_spec,
                _resident_spec((4, LANES)), _resident_spec((2 * LANES, 2 * LANES)),
                _resident_spec((N_GROUPS, GROUP_DIM, 2 * GROUP_DIM))]
    out_widths = (ATTN_WIDTH, LANES, LANES, LANES, LANES, ATTN_WIDTH, FOURIER_WIDTH, FOURIER_WIDTH)
    out_specs = [_row_spec(tm, w) for w in out_widths] + [_row_spec(tm // RADIX, RADIX * FOURIER_WIDTH)]
    out_shape = ([jax.ShapeDtypeStruct((rows, w), BF16) for w in out_widths]
                 + [jax.ShapeDtypeStruct((rows // RADIX, RADIX * FOURIER_WIDTH), BF16)])
    scratch = pltpu.VMEM((tm // SUB_ROWS, FOURIER_WIDTH // LANES, SUB_ROWS, LANES), F32)
    return in_specs, out_specs, out_shape, scratch


def _in_proj(x2d, seq_len, in_w):
    rows = x2d.shape[0]
    tm = ROW_TILE
    in_specs, out_specs, out_shape, scratch = _in_proj_specs(tm, rows, seq_len)
    return pl.pallas_call(
        _in_proj_kernel,
        grid=(rows // tm,),
        in_specs=[_row_spec(tm, D_MODEL)] + in_specs,
        out_specs=out_specs,
        out_shape=out_shape,
        scratch_shapes=[scratch],
        compiler_params=pltpu.CompilerParams(dimension_semantics=("arbitrary",), vmem_limit_bytes=VMEM_LIMIT),
        name="in_proj",
    )(x2d, *in_w)


def _attn_kernel(sink_ref, bounded_ref, q_ref, ka_ref, kb_ref, va_ref, vb_ref, sg_ref, o_ref, *, n_blocks):
    refs = (sink_ref, q_ref, ka_ref, kb_ref, va_ref, vb_ref, sg_ref, o_ref)

    @pl.when(bounded_ref[0] == 1)
    def _():
        _attn_body(*refs, n_blocks=n_blocks, shift=False)

    @pl.when(bounded_ref[0] == 0)
    def _():
        _attn_body(*refs, n_blocks=n_blocks, shift=True)


def _attn_body(sink_ref, q_ref, ka_ref, kb_ref, va_ref, vb_ref, sg_ref, o_ref, *, n_blocks, shift):
    i = pl.program_id(1)
    blocks_per_step = ATTN_TILE // BLOCK
    row = jax.lax.broadcasted_iota(jnp.int32, (BLOCK, 2 * BLOCK), 0)
    col = jax.lax.broadcasted_iota(jnp.int32, (BLOCK, 2 * BLOCK), 1) % BLOCK
    lane = jax.lax.broadcasted_iota(jnp.int32, (1, LANES), 1)
    low = lane < HEAD_DIM
    zero = jnp.zeros((), BF16)
    lane_full = jax.lax.broadcasted_iota(jnp.int32, (BLOCK, LANES), 1)
    ones_low = jnp.where(lane_full < HEAD_DIM, 1.0, 0.0).astype(BF16)
    ones_high = jnp.where(lane_full < HEAD_DIM, 0.0, 1.0).astype(BF16)

    for jb in range(blocks_per_step):
        ib = i * blocks_per_step + jb
        rows = slice(jb * BLOCK, (jb + 1) * BLOCK)
        edge_prev = jnp.where(ib == 0, NEG, 0.0)
        edge_next = jnp.where(ib == n_blocks - 1, NEG, 0.0)
        bias_prev = jnp.where(col >= row, 0.0, NEG) + edge_prev
        bias_next = jnp.where(col <= row, 0.0, NEG) + edge_next
        starts = [pl.multiple_of(jnp.clip(ib + c, 0, n_blocks - 1) * BLOCK, BLOCK) for c in (-1, 0, 1)]
        for kvh, (k_ref, v_ref) in enumerate(((ka_ref, va_ref), (kb_ref, vb_ref))):
            kbd, vbd = [], []
            for st in starts:
                kblk = k_ref[0, pl.ds(st, BLOCK), :]
                vblk = v_ref[0, pl.ds(st, BLOCK), :]
                kbd.append(jnp.concatenate([jnp.where(low, kblk, zero), jnp.where(low, zero, kblk)], axis=0))
                vbd.append(jnp.concatenate([
                    jnp.concatenate([jnp.where(low, vblk, zero), ones_low], axis=1),
                    jnp.concatenate([jnp.where(low, zero, vblk), ones_high], axis=1)], axis=0))
            vbd = jnp.concatenate(vbd, axis=0)
            for pair in range(2):
                pidx = kvh * 2 + pair
                lanes = slice(pidx * LANES, (pidx + 1) * LANES)
                qp = q_ref[0, rows, lanes]
                scores = [_dot_nt(qp, kb) for kb in kbd]
                sink_e = sink_ref[2 * pidx]
                sink_o = sink_ref[2 * pidx + 1]
                if shift:
                    scores = [scores[0] + bias_prev, scores[1], scores[2] + bias_next]
                    smax = jnp.maximum(jnp.maximum(scores[0], scores[1]), scores[2])
                    m_e = jnp.maximum(jnp.max(smax[:, :BLOCK], axis=-1, keepdims=True), sink_e)
                    m_o = jnp.maximum(jnp.max(smax[:, BLOCK:], axis=-1, keepdims=True), sink_o)
                    m_both = jnp.concatenate([jnp.broadcast_to(m_e, (BLOCK, BLOCK)),
                                              jnp.broadcast_to(m_o, (BLOCK, BLOCK))], axis=1)
                    probs = [jnp.exp2(s - m_both).astype(BF16) for s in scores]
                    p_sink = jnp.where(low, jnp.exp2(sink_e - m_e), jnp.exp2(sink_o - m_o))
                else:
                    scores = [s.astype(BF16) for s in scores]
                    scores = [scores[0] + bias_prev.astype(BF16), scores[1], scores[2] + bias_next.astype(BF16)]
                    probs = [jnp.exp2(s) for s in scores]
                    p_sink = jnp.exp2(jnp.where(low, sink_e, sink_o))
                acc = _dot(jnp.concatenate(probs, axis=1), vbd)
                out = acc[:, :LANES] / (acc[:, LANES:] + p_sink) * sg_ref[0, rows, lanes].astype(F32)
                o_ref[0, rows, lanes] = out.astype(o_ref.dtype)


def _attention(sink2, bounded, q, ka, kb, va, vb, sga):
    bsz, seq_len, _ = q.shape
    tq = ATTN_TILE
    q_spec = pl.BlockSpec((1, tq, ATTN_WIDTH), lambda b, i, s, f: (b, i, 0))
    kv_spec = pl.BlockSpec((1, seq_len, LANES), lambda b, i, s, f: (b, 0, 0))
    grid_spec = pltpu.PrefetchScalarGridSpec(
        num_scalar_prefetch=2,
        grid=(bsz, seq_len // tq),
        in_specs=[q_spec, kv_spec, kv_spec, kv_spec, kv_spec, q_spec],
        out_specs=q_spec,
    )
    return pl.pallas_call(
        functools.partial(_attn_kernel, n_blocks=seq_len // BLOCK),
        grid_spec=grid_spec,
        out_shape=jax.ShapeDtypeStruct((bsz, seq_len, ATTN_WIDTH), BF16),
        compiler_params=pltpu.CompilerParams(dimension_semantics=("arbitrary", "arbitrary"),
                                             vmem_limit_bytes=VMEM_LIMIT),
        name="band_attention",
    )(sink2, bounded, q, ka, kb, va, vb, sga)


def _seq_dft_kernel(a_ref, b_ref, m_ref, sg_ref, o_ref, xy_ref, *, chunk):
    combos = (
        lambda a, b: ((a[0] + a[2]) + (a[1] + a[3]), (b[0] + b[2]) + (b[1] + b[3])),
        lambda a, b: ((a[0] - a[2]) - (b[1] - b[3]), (b[0] - b[2]) + (a[1] - a[3])),
        lambda a, b: ((a[0] + a[2]) - (a[1] + a[3]), (b[0] + b[2]) - (b[1] + b[3])),
        lambda a, b: ((a[0] - a[2]) + (b[1] - b[3]), (b[0] - b[2]) - (a[1] - a[3])),
    )

    def combine(r):
        for rb in range(chunk // DFT_PREP_ROWS):
            rows = lambda q: slice(q * chunk + rb * DFT_PREP_ROWS, q * chunk + (rb + 1) * DFT_PREP_ROWS)
            a = [a_ref[0, rows(q), :].astype(F32) for q in range(RADIX)]
            b = [b_ref[0, rows(q), :].astype(F32) for q in range(RADIX)]
            x, y = combos[r](a, b)
            xy_ref[r, rows(0), :] = x.astype(xy_ref.dtype)
            xy_ref[r, rows(1), :] = y.astype(xy_ref.dtype)

    combine(0)
    for r in range(RADIX):
        f = _dot(m_ref[r], xy_ref[r])
        if r + 1 < RADIX:
            combine(r + 1)
        cols = slice(r * DFT_COLS, (r + 1) * DFT_COLS)
        o_ref[0, :, cols] = (f * sg_ref[0, :, cols].astype(F32)).astype(o_ref.dtype)


def _dft_matrix(seq_len):
    chunk = seq_len // RADIX
    split = 32
    period = seq_len // (RADIX * split)
    m = jnp.arange(chunk, dtype=jnp.int32)[None, :]
    ang_hi = ((jnp.arange(chunk // split, dtype=jnp.int32)[:, None] * m) % period).astype(F32) * (2.0 * math.pi / period)
    ang_lo = ((jnp.arange(RADIX * split, dtype=jnp.int32)[:, None] * m) % seq_len).astype(F32) * (2.0 * math.pi / seq_len)
    c_hi, s_hi = jnp.cos(ang_hi)[None, :, None, :], jnp.sin(ang_hi)[None, :, None, :]
    lo = lambda t: t.reshape(split, RADIX, chunk).transpose(1, 0, 2)[:, None, :, :]
    c_lo, s_lo = lo(jnp.cos(ang_lo)), lo(jnp.sin(ang_lo))
    cos = (c_hi * c_lo - s_hi * s_lo).reshape(RADIX, chunk, chunk)
    sin = (s_hi * c_lo + c_hi * s_lo).reshape(RADIX, chunk, chunk)
    return jnp.concatenate([cos, -sin], axis=2).astype(BF16)


def _seq_dft(a, b, dft_m, sg_perm):
    bsz, seq_len, width = a.shape
    chunk = seq_len // RADIX
    ab_spec = pl.BlockSpec((1, seq_len, DFT_COLS), lambda bi, h: (bi, 0, h))
    m_spec = pl.BlockSpec((RADIX, chunk, 2 * chunk), lambda bi, h: (0, 0, 0), pipeline_mode=pl.Buffered(1))
    io_spec = pl.BlockSpec((1, chunk, RADIX * DFT_COLS), lambda bi, h: (bi, 0, h))
    return pl.pallas_call(
        functools.partial(_seq_dft_kernel, chunk=chunk),
        grid=(bsz, width // DFT_COLS),
        in_specs=[ab_spec, ab_spec, m_spec, io_spec],
        out_specs=io_spec,
        out_shape=jax.ShapeDtypeStruct((bsz, chunk, RADIX * width), BF16),
        scratch_shapes=[pltpu.VMEM((RADIX, 2 * chunk, DFT_COLS), BF16)],
        compiler_params=pltpu.CompilerParams(dimension_semantics=("arbitrary", "arbitrary"),
                                             vmem_limit_bytes=VMEM_LIMIT),
        name="seq_dft",
    )(a, b, dft_m, sg_perm)


def _out_proj_rows(sub, x_ref, ma_ref, mf_ref, w_ref, o_ref, acc_ref):
    n_cb = D_MODEL // LANES
    rows = slice(sub * SUB_ROWS, (sub + 1) * SUB_ROWS)
    perm_rows = slice(sub * (SUB_ROWS // RADIX), (sub + 1) * (SUB_ROWS // RADIX))
    y = x_ref[rows, :] + _dot(ma_ref[rows, :], w_ref[:ATTN_WIDTH, :])
    for cb in range(n_cb):
        acc_ref[sub, cb] = y[:, cb * LANES:(cb + 1) * LANES]
    for r in range(RADIX):
        mf_r = jnp.concatenate([mf_ref[perm_rows, _perm_col(r, f):_perm_col(r, f) + DFT_COLS]
                                for f in range(0, FOURIER_WIDTH, DFT_COLS)], axis=1)
        y = _dot(mf_r, w_ref[ATTN_WIDTH:, :])
        for cb in range(n_cb):
            acc_ref[sub, cb, pl.ds(r, SUB_ROWS // RADIX, stride=RADIX), :] += y[:, cb * LANES:(cb + 1) * LANES]
    for cb in range(n_cb):
        o_ref[rows, cb * LANES:(cb + 1) * LANES] = acc_ref[sub, cb]


def _out_proj_kernel(x_ref, ma_ref, mf_ref, w_ref, o_ref, acc_ref):
    for sub in range(x_ref.shape[0] // SUB_ROWS):
        _out_proj_rows(sub, x_ref, ma_ref, mf_ref, w_ref, o_ref, acc_ref)


def _out_in_proj_kernel(x_ref, ma_ref, mf_ref, wo_ref, *refs):
    in_refs = refs[:N_IN_PROJ_INPUTS]
    y_ref = refs[N_IN_PROJ_INPUTS]
    out_refs = refs[N_IN_PROJ_INPUTS + 1:N_IN_PROJ_INPUTS + 1 + N_IN_PROJ_OUTPUTS]
    acc_ref, perm_ref = refs[-2:]
    for sub in range(x_ref.shape[0] // SUB_ROWS):
        _out_proj_rows(sub, x_ref, ma_ref, mf_ref, wo_ref, y_ref, acc_ref)
        _in_proj_rows(y_ref[sub * SUB_ROWS:(sub + 1) * SUB_ROWS, :], sub, *in_refs, *out_refs, perm_ref)


def _out_proj_specs(tm):
    in_specs = [_row_spec(tm, D_MODEL), _row_spec(tm, ATTN_WIDTH), _row_spec(tm // RADIX, RADIX * FOURIER_WIDTH),
                _resident_spec((D_MODEL, D_MODEL))]
    scratch = pltpu.VMEM((tm // SUB_ROWS, D_MODEL // LANES, SUB_ROWS, LANES), F32)
    return in_specs, scratch


def _out_proj(x2d, ma, mf_perm, w_bf):
    rows = x2d.shape[0]
    tm = ROW_TILE
    in_specs, scratch = _out_proj_specs(tm)
    return pl.pallas_call(
        _out_proj_kernel,
        grid=(rows // tm,),
        in_specs=in_specs,
        out_specs=_row_spec(tm, D_MODEL),
        out_shape=jax.ShapeDtypeStruct((rows, D_MODEL), F32),
        scratch_shapes=[scratch],
        compiler_params=pltpu.CompilerParams(dimension_semantics=("arbitrary",), vmem_limit_bytes=VMEM_LIMIT),
        name="out_proj",
    )(x2d, ma, mf_perm, w_bf)


def _out_in_proj(x2d, ma, mf_perm, w_bf, seq_len, in_w):
    rows = x2d.shape[0]
    tm = FUSED_ROW_TILE
    o_in_specs, acc_scratch = _out_proj_specs(tm)
    i_in_specs, i_out_specs, i_out_shape, perm_scratch = _in_proj_specs(tm, rows, seq_len)
    outs = pl.pallas_call(
        _out_in_proj_kernel,
        grid=(rows // tm,),
        in_specs=o_in_specs + i_in_specs,
        out_specs=[_row_spec(tm, D_MODEL)] + i_out_specs,
        out_shape=[jax.ShapeDtypeStruct((rows, D_MODEL), F32)] + i_out_shape,
        scratch_shapes=[acc_scratch, perm_scratch],
        compiler_params=pltpu.CompilerParams(dimension_semantics=("arbitrary",), vmem_limit_bytes=FUSED_VMEM_LIMIT),
        name="out_in_proj",
    )(x2d, ma, mf_perm, w_bf, *in_w)
    return outs[0], outs[1:]


def _rope_tables(seq_len):
    half = HEAD_DIM // 2
    inv_freq = 1.0 / (ROPE_THETA ** (jnp.arange(half, dtype=F32) / half))
    ang = jnp.arange(seq_len, dtype=F32)[:, None] * inv_freq[None, :]
    cos = jnp.cos(ang)
    sin = jnp.sin(ang)
    cos_t = jnp.concatenate([cos, cos, cos, cos], axis=1)
    sin_t = jnp.concatenate([-sin, sin, -sin, sin], axis=1)
    return cos_t, sin_t


def _rope_gains(q_gain, k_gain):
    pair = lambda g: jnp.tile(g, LANES // HEAD_DIM)
    rot = lambda g: jnp.roll(g, HEAD_DIM // 2)
    gq = q_gain.astype(F32) * (HEAD_DIM ** -0.5 * LOG2E)
    gk = k_gain.astype(F32)
    return jnp.stack([pair(gq), pair(rot(gq)), pair(gk), pair(rot(gk))])


def _head_mean_matrix(width):
    head = jnp.arange(width, dtype=jnp.int32) // HEAD_DIM
    return jnp.where(head[:, None] == head[None, :], 1.0 / HEAD_DIM, 0.0).astype(BF16)


def _mixers(proj, bsz, seq_len, dft_m, sink2, bounded):
    q, ka, kb, va, vb, sga, a, b, sgf = proj
    r3 = lambda t: t.reshape(bsz, seq_len, t.shape[-1])
    ma = _attention(sink2, bounded, r3(q), r3(ka), r3(kb), r3(va), r3(vb), r3(sga))
    perm_rows = seq_len // RADIX
    mf = _seq_dft(r3(a), r3(b), dft_m, sgf.reshape(bsz, perm_rows, RADIX * FOURIER_WIDTH))
    return ma.reshape(bsz * seq_len, ATTN_WIDTH), mf.reshape(bsz * perm_rows, RADIX * FOURIER_WIDTH)


def _trunk(x, layers, dft_m):
    bsz, seq_len, _ = x.shape
    x2d = x.reshape(bsz * seq_len, D_MODEL)
    proj = _in_proj(x2d, seq_len, layers[0]["in_w"])
    for l, layer in enumerate(layers):
        ma, mf = _mixers(proj, bsz, seq_len, dft_m, layer["sink2"], layer["bounded"])
        if l + 1 < len(layers):
            x2d, proj = _out_in_proj(x2d, ma, mf, layer["w_out"], seq_len, layers[l + 1]["in_w"])
        else:
            x2d = _out_proj(x2d, ma, mf, layer["w_out"])
    return x2d.reshape(bsz, seq_len, D_MODEL)


def kernel(x_prompt, x_sample, norm_gain, w_in, q_norm_gain, k_norm_gain, sink_logit, w_fourier, w_out):
    depth = norm_gain.shape[0]
    seq_p, seq_s = x_prompt.shape[1], x_sample.shape[1]
    assert seq_p == seq_s
    cos_t, sin_t = _rope_tables(seq_p)
    hs = _head_mean_matrix(2 * LANES)
    layers = []
    for l in range(depth):
        sink2 = sink_logit[l].astype(F32) * LOG2E
        logit_bound = (HEAD_DIM ** 0.5 * LOG2E) * jnp.max(jnp.abs(q_norm_gain[l])) * jnp.max(jnp.abs(k_norm_gain[l]))
        bounded = jnp.maximum(logit_bound, jnp.max(jnp.abs(sink2))) <= MAX_UNSHIFTED_LOGIT2
        in_w = (norm_gain[l].astype(F32)[None, :], w_in[l].astype(BF16), cos_t, sin_t,
                _rope_gains(q_norm_gain[l], k_norm_gain[l]), hs, _fourier_weights(w_fourier[l], seq_p))
        layers.append(dict(in_w=in_w, sink2=sink2, bounded=bounded.astype(jnp.int32)[None],
                           w_out=w_out[l].astype(BF16)))
    dft_m = _dft_matrix(seq_p)
    return (_trunk(x_prompt, layers, dft_m), _trunk(x_sample, layers, dft_m))
```

```python
import functools
import math

import jax
import jax.numpy as jnp
from jax.experimental import pallas as pl
from jax.experimental.pallas import tpu as pltpu

D_MODEL = 1024
HEAD_DIM = 64
N_Q_HEADS = 8
N_KV_HEADS = 2
ATTN_WIDTH = N_Q_HEADS * HEAD_DIM
KV_WIDTH = N_KV_HEADS * HEAD_DIM
FOURIER_WIDTH = D_MODEL - ATTN_WIDTH
N_GROUPS = 4
GROUP_DIM = FOURIER_WIDTH // N_GROUPS
IN_WIDTH = 2 * ATTN_WIDTH + 2 * KV_WIDTH + 2 * FOURIER_WIDTH
Q_OFF = 0
K_OFF = ATTN_WIDTH
V_OFF = K_OFF + KV_WIDTH
GA_OFF = V_OFF + KV_WIDTH
U_OFF = GA_OFF + ATTN_WIDTH
GF_OFF = U_OFF + FOURIER_WIDTH
BLOCK = 128
ROPE_THETA = 10000.0
EPS = 1e-6
NEG = -1e30
LANES = 128
RADIX = 4
DFT_COLS = 256
DFT_PREP_ROWS = 128
LOG2E = math.log2(math.e)
MAX_UNSHIFTED_LOGIT2 = 100.0

ROW_TILE = 1024
FUSED_ROW_TILE = 1024
FUSED_SUB_ROWS = 512
ATTN_TILE = 1024
VMEM_LIMIT = 48 * 1024 * 1024
FUSED_VMEM_LIMIT = 54 * 1024 * 1024

BF16 = jnp.bfloat16
F32 = jnp.float32


def _compiler_params(grid_rank, vmem_limit=VMEM_LIMIT):
    return pltpu.CompilerParams(dimension_semantics=("arbitrary",) * grid_rank, vmem_limit_bytes=vmem_limit)


def _dot(a, b):
    return jnp.dot(a, b, preferred_element_type=F32)


def _dot_narrow(a, b):
    half = a.shape[0] // 2
    return jnp.concatenate([_dot(a[:half], b), _dot(a[half:], b)], axis=0)


def _dot_nt(a, b):
    return jax.lax.dot_general(a, b, (((1,), (1,)), ((), ())), preferred_element_type=F32)


def _silu(x):
    return x / (1.0 + jnp.exp(-x))


def _rotate_half(t, first_half):
    width = t.shape[-1]
    fwd = pltpu.roll(t, HEAD_DIM // 2, axis=1)
    bwd = pltpu.roll(t, width - HEAD_DIM // 2, axis=1)
    return jnp.where(first_half, bwd, fwd)


def _fourier_weight_kernel(cs_ref, w_ref, o_ref):
    for g in range(N_GROUPS):
        w = w_ref[g]
        c = jnp.dot(cs_ref[0], w, preferred_element_type=F32, precision=jax.lax.Precision.HIGHEST)
        s = jnp.dot(cs_ref[1], w, preferred_element_type=F32, precision=jax.lax.Precision.HIGHEST)
        o_ref[g] = jnp.concatenate([c, s], axis=1).astype(o_ref.dtype)


def _fourier_weights(w_four, seq_len):
    idx = (jnp.arange(GROUP_DIM, dtype=jnp.int32)[:, None] * jnp.arange(GROUP_DIM, dtype=jnp.int32)[None, :]) % GROUP_DIM
    ang = idx.astype(F32) * (2.0 * math.pi / GROUP_DIM)
    scale = 1.0 / math.sqrt(seq_len * GROUP_DIM)
    cs = jnp.stack([jnp.cos(ang), jnp.sin(ang)]) * scale
    return pl.pallas_call(
        _fourier_weight_kernel,
        out_shape=jax.ShapeDtypeStruct((N_GROUPS, GROUP_DIM, 2 * GROUP_DIM), BF16),
        name="fourier_weight_prep",
    )(cs, w_four)


def _perm_col(r, feature):
    return ((feature // DFT_COLS) * RADIX + r) * DFT_COLS + feature % DFT_COLS


def _in_proj_rows(x, sub, gn_ref, w_ref, cos_ref, sin_ref, gains_ref, hs_ref, wab_ref,
                  q_ref, ka_ref, kb_ref, va_ref, vb_ref, sga_ref, a_ref, b_ref, sgf_ref, perm_ref):
    sub_rows = perm_ref.shape[2]
    rows = slice(sub * sub_rows, (sub + 1) * sub_rows)
    xb = (x * gn_ref[...]).astype(BF16)
    cos = cos_ref[rows, :]
    sin = sin_ref[rows, :]
    qcos, qsin = cos * gains_ref[0:1, :], sin * gains_ref[1:2, :]
    kcos, ksin = cos * gains_ref[2:3, :], sin * gains_ref[3:4, :]
    ms = jnp.mean(x * x, axis=-1, keepdims=True)
    r = jnp.broadcast_to(jax.lax.rsqrt(ms + EPS), (sub_rows, LANES))
    eps_z = jnp.broadcast_to(EPS * (ms + EPS), (sub_rows, LANES))
    wide = lambda t, width: jnp.concatenate([t] * (width // LANES), axis=1)

    lane = jax.lax.broadcasted_iota(jnp.int32, (1, LANES), 1)
    first_half_pair = (lane % HEAD_DIM) < (HEAD_DIM // 2)
    low_head = lane < HEAD_DIM
    hs = hs_ref[...]

    def head_mean_sq(z):
        return _dot_narrow((z * z).astype(BF16), hs)

    def silu_of_scaled(z, width):
        h = z * wide(0.5 * r, width)
        return h + h * jnp.tanh(h)

    zq = _dot(xb, w_ref[:, Q_OFF:Q_OFF + ATTN_WIDTH])
    half = ATTN_WIDTH // 2
    ssq = jnp.concatenate([head_mean_sq(zq[:, :half]), head_mean_sq(zq[:, half:])], axis=1)
    qr = (zq * wide(qcos, ATTN_WIDTH)
          + _rotate_half(zq, wide(first_half_pair, ATTN_WIDTH)) * wide(qsin, ATTN_WIDTH))
    q_ref[rows, :] = (qr * jax.lax.rsqrt(ssq + wide(eps_z, ATTN_WIDTH))).astype(q_ref.dtype)

    zkv = _dot_narrow(xb, w_ref[:, K_OFF:K_OFF + 2 * KV_WIDTH])
    zk = zkv[:, :KV_WIDTH]
    ssk = head_mean_sq(zkv)[:, :KV_WIDTH]
    kr = (zk * kcos + _rotate_half(zk, first_half_pair) * ksin) * jax.lax.rsqrt(ssk + eps_z)
    kr_sw = pltpu.roll(kr, HEAD_DIM, axis=1)
    ka_ref[rows, :] = jnp.where(low_head, kr, kr_sw).astype(ka_ref.dtype)
    kb_ref[rows, :] = jnp.where(low_head, kr_sw, kr).astype(kb_ref.dtype)
    zv = zkv[:, KV_WIDTH:] * r
    zv_sw = pltpu.roll(zv, HEAD_DIM, axis=1)
    va_ref[rows, :] = jnp.where(low_head, zv, zv_sw).astype(va_ref.dtype)
    vb_ref[rows, :] = jnp.where(low_head, zv_sw, zv).astype(vb_ref.dtype)

    zga = _dot(xb, w_ref[:, GA_OFF:GA_OFF + ATTN_WIDTH])
    sga_ref[rows, :] = silu_of_scaled(zga, ATTN_WIDTH).astype(sga_ref.dtype)
    sgf = silu_of_scaled(_dot(xb, w_ref[:, GF_OFF:GF_OFF + FOURIER_WIDTH]), FOURIER_WIDTH)
    for cb in range(FOURIER_WIDTH // LANES):
        perm_ref[sub, cb] = sgf[:, cb * LANES:(cb + 1) * LANES]
    perm_rows = slice(sub * (sub_rows // RADIX), (sub + 1) * (sub_rows // RADIX))
    for rr in range(RADIX):
        for cb in range(FOURIER_WIDTH // LANES):
            col = _perm_col(rr, cb * LANES)
            sgf_ref[perm_rows, col:col + LANES] = (
                perm_ref[sub, cb, pl.ds(rr, sub_rows // RADIX, stride=RADIX), :].astype(sgf_ref.dtype))

    zu = (_dot(xb, w_ref[:, U_OFF:U_OFF + FOURIER_WIDTH]) * wide(r, FOURIER_WIDTH)).astype(BF16)
    for g in range(N_GROUPS):
        ab = _dot_narrow(zu[:, g * GROUP_DIM:(g + 1) * GROUP_DIM], wab_ref[g])
        a_ref[rows, g * GROUP_DIM:(g + 1) * GROUP_DIM] = ab[:, :GROUP_DIM].astype(a_ref.dtype)
        b_ref[rows, g * GROUP_DIM:(g + 1) * GROUP_DIM] = ab[:, GROUP_DIM:].astype(b_ref.dtype)


N_IN_PROJ_INPUTS = 7
N_IN_PROJ_OUTPUTS = 9


def _in_proj_kernel(x_ref, *refs):
    sub_rows = refs[-1].shape[2]
    for sub in range(x_ref.shape[0] // sub_rows):
        _in_proj_rows(x_ref[sub * sub_rows:(sub + 1) * sub_rows, :], sub, *refs)


def _row_spec(tm, width):
    return pl.BlockSpec((tm, width), lambda i: (i, 0))


def _resident_spec(shape):
    return pl.BlockSpec(shape, lambda i: (0,) * len(shape), pipeline_mode=pl.Buffered(1))


def _in_proj_specs(tm, sub_rows, rows, seq_len):
    steps_per_seq = seq_len // tm
    tab_spec = pl.BlockSpec((tm, LANES), lambda i: (i % steps_per_seq, 0))
    in_specs = [_resident_spec((1, D_MODEL)), _resident_spec((D_MODEL, IN_WIDTH)), tab_spec, tab_spec,
                _resident_spec((4, LANES)), _resident_spec((2 * LANES, 2 * LANES)),
                _resident_spec((N_GROUPS, GROUP_DIM, 2 * GROUP_DIM))]
    out_widths = (ATTN_WIDTH, LANES, LANES, LANES, LANES, ATTN_WIDTH, FOURIER_WIDTH, FOURIER_WIDTH)
    out_specs = [_row_spec(tm, w) for w in out_widths] + [_row_spec(tm // RADIX, RADIX * FOURIER_WIDTH)]
    out_shape = ([jax.ShapeDtypeStruct((rows, w), BF16) for w in out_widths]
                 + [jax.ShapeDtypeStruct((rows // RADIX, RADIX * FOURIER_WIDTH), BF16)])
    scratch = pltpu.VMEM((tm // sub_rows, FOURIER_WIDTH // LANES, sub_rows, LANES), F32)
    return in_specs, out_specs, out_shape, scratch


def _in_proj(x2d, seq_len, in_w):
    rows = x2d.shape[0]
    tm = ROW_TILE
    in_specs, out_specs, out_shape, scratch = _in_proj_specs(tm, tm, rows, seq_len)
    return pl.pallas_call(
        _in_proj_kernel,
        grid=(rows // tm,),
        in_specs=[_row_spec(tm, D_MODEL)] + in_specs,
        out_specs=out_specs,
        out_shape=out_shape,
        scratch_shapes=[scratch],
        compiler_params=_compiler_params(1),
        name="in_proj",
    )(x2d, *in_w)


def _attn_kernel(sink_ref, bounded_ref, q_ref, ka_ref, kb_ref, va_ref, vb_ref, sg_ref, o_ref, *, n_blocks):
    refs = (sink_ref, q_ref, ka_ref, kb_ref, va_ref, vb_ref, sg_ref, o_ref)

    @pl.when(bounded_ref[0] == 1)
    def _():
        _attn_body(*refs, n_blocks=n_blocks, shift=False)

    @pl.when(bounded_ref[0] == 0)
    def _():
        _attn_body(*refs, n_blocks=n_blocks, shift=True)


def _attn_body(sink_ref, q_ref, ka_ref, kb_ref, va_ref, vb_ref, sg_ref, o_ref, *, n_blocks, shift):
    i = pl.program_id(1)
    blocks_per_step = ATTN_TILE // BLOCK
    row = jax.lax.broadcasted_iota(jnp.int32, (BLOCK, 2 * BLOCK), 0)
    col = jax.lax.broadcasted_iota(jnp.int32, (BLOCK, 2 * BLOCK), 1) % BLOCK
    lane = jax.lax.broadcasted_iota(jnp.int32, (1, LANES), 1)
    low = lane < HEAD_DIM
    zero = jnp.zeros((), BF16)
    lane_full = jax.lax.broadcasted_iota(jnp.int32, (BLOCK, LANES), 1)
    ones_low = jnp.where(lane_full < HEAD_DIM, 1.0, 0.0).astype(BF16)
    ones_high = jnp.where(lane_full < HEAD_DIM, 0.0, 1.0).astype(BF16)

    for jb in range(blocks_per_step):
        ib = i * blocks_per_step + jb
        rows = slice(jb * BLOCK, (jb + 1) * BLOCK)
        edge_prev = jnp.where(ib == 0, NEG, 0.0)
        edge_next = jnp.where(ib == n_blocks - 1, NEG, 0.0)
        bias_prev = jnp.where(col >= row, 0.0, NEG) + edge_prev
        bias_next = jnp.where(col <= row, 0.0, NEG) + edge_next
        starts = [pl.multiple_of(jnp.clip(ib + c, 0, n_blocks - 1) * BLOCK, BLOCK) for c in (-1, 0, 1)]
        for kvh, (k_ref, v_ref) in enumerate(((ka_ref, va_ref), (kb_ref, vb_ref))):
            kbd, vbd = [], []
            for st in starts:
                kblk = k_ref[0, pl.ds(st, BLOCK), :]
                vblk = v_ref[0, pl.ds(st, BLOCK), :]
                kbd.append(jnp.concatenate([jnp.where(low, kblk, zero), jnp.where(low, zero, kblk)], axis=0))
                vbd.append(jnp.concatenate([
                    jnp.concatenate([jnp.where(low, vblk, zero), ones_low], axis=1),
                    jnp.concatenate([jnp.where(low, zero, vblk), ones_high], axis=1)], axis=0))
            vbd = jnp.concatenate(vbd, axis=0)
            for pair in range(2):
                pidx = kvh * 2 + pair
                lanes = slice(pidx * LANES, (pidx + 1) * LANES)
                qp = q_ref[0, rows, lanes]
                scores = [_dot_nt(qp, kb) for kb in kbd]
                sink_e = sink_ref[2 * pidx]
                sink_o = sink_ref[2 * pidx + 1]
                if shift:
                    scores = [scores[0] + bias_prev, scores[1], scores[2] + bias_next]
                    smax = jnp.maximum(jnp.maximum(scores[0], scores[1]), scores[2])
                    m_e = jnp.maximum(jnp.max(smax[:, :BLOCK], axis=-1, keepdims=True), sink_e)
                    m_o = jnp.maximum(jnp.max(smax[:, BLOCK:], axis=-1, keepdims=True), sink_o)
                    m_both = jnp.concatenate([jnp.broadcast_to(m_e, (BLOCK, BLOCK)),
                                              jnp.broadcast_to(m_o, (BLOCK, BLOCK))], axis=1)
                    probs = [jnp.exp2(s - m_both).astype(BF16) for s in scores]
                    p_sink = jnp.where(low, jnp.exp2(sink_e - m_e), jnp.exp2(sink_o - m_o))
                else:
                    scores = [s.astype(BF16) for s in scores]
                    scores = [scores[0] + bias_prev.astype(BF16), scores[1], scores[2] + bias_next.astype(BF16)]
                    probs = [jnp.exp2(s) for s in scores]
                    p_sink = jnp.exp2(jnp.where(low, sink_e, sink_o))
                acc = _dot(jnp.concatenate(probs, axis=1), vbd)
                out = acc[:, :LANES] / (acc[:, LANES:] + p_sink) * sg_ref[0, rows, lanes].astype(F32)
                o_ref[0, rows, lanes] = out.astype(o_ref.dtype)


def _attention(sink2, bounded, q, ka, kb, va, vb, sga):
    bsz, seq_len, _ = q.shape
    tq = ATTN_TILE
    q_spec = pl.BlockSpec((1, tq, ATTN_WIDTH), lambda b, i, s, f: (b, i, 0))
    kv_spec = pl.BlockSpec((1, seq_len, LANES), lambda b, i, s, f: (b, 0, 0))
    grid_spec = pltpu.PrefetchScalarGridSpec(
        num_scalar_prefetch=2,
        grid=(bsz, seq_len // tq),
        in_specs=[q_spec, kv_spec, kv_spec, kv_spec, kv_spec, q_spec],
        out_specs=q_spec,
    )
    return pl.pallas_call(
        functools.partial(_attn_kernel, n_blocks=seq_len // BLOCK),
        grid_spec=grid_spec,
        out_shape=jax.ShapeDtypeStruct((bsz, seq_len, ATTN_WIDTH), BF16),
        compiler_params=_compiler_params(2),
        name="band_attention",
    )(sink2, bounded, q, ka, kb, va, vb, sga)


def _seq_dft_kernel(a_ref, b_ref, m_ref, sg_ref, o_ref, xy_ref, *, chunk):
    combos = (
        lambda a, b: ((a[0] + a[2]) + (a[1] + a[3]), (b[0] + b[2]) + (b[1] + b[3])),
        lambda a, b: ((a[0] - a[2]) - (b[1] - b[3]), (b[0] - b[2]) + (a[1] - a[3])),
        lambda a, b: ((a[0] + a[2]) - (a[1] + a[3]), (b[0] + b[2]) - (b[1] + b[3])),
        lambda a, b: ((a[0] - a[2]) + (b[1] - b[3]), (b[0] - b[2]) - (a[1] - a[3])),
    )

    def combine(r):
        for rb in range(chunk // DFT_PREP_ROWS):
            rows = lambda q: slice(q * chunk + rb * DFT_PREP_ROWS, q * chunk + (rb + 1) * DFT_PREP_ROWS)
            a = [a_ref[0, rows(q), :].astype(F32) for q in range(RADIX)]
            b = [b_ref[0, rows(q), :].astype(F32) for q in range(RADIX)]
            x, y = combos[r](a, b)
            xy_ref[r, rows(0), :] = x.astype(xy_ref.dtype)
            xy_ref[r, rows(1), :] = y.astype(xy_ref.dtype)

    combine(0)
    for r in range(RADIX):
        f = _dot(m_ref[r], xy_ref[r])
        if r + 1 < RADIX:
            combine(r + 1)
        cols = slice(r * DFT_COLS, (r + 1) * DFT_COLS)
        o_ref[0, :, cols] = (f * sg_ref[0, :, cols].astype(F32)).astype(o_ref.dtype)


def _dft_matrix(seq_len):
    chunk = seq_len // RADIX
    split = 32
    period = seq_len // (RADIX * split)
    m = jnp.arange(chunk, dtype=jnp.int32)[None, :]
    ang_hi = ((jnp.arange(chunk // split, dtype=jnp.int32)[:, None] * m) % period).astype(F32) * (2.0 * math.pi / period)
    ang_lo = ((jnp.arange(RADIX * split, dtype=jnp.int32)[:, None] * m) % seq_len).astype(F32) * (2.0 * math.pi / seq_len)
    c_hi, s_hi = jnp.cos(ang_hi)[None, :, None, :], jnp.sin(ang_hi)[None, :, None, :]
    lo = lambda t: t.reshape(split, RADIX, chunk).transpose(1, 0, 2)[:, None, :, :]
    c_lo, s_lo = lo(jnp.cos(ang_lo)), lo(jnp.sin(ang_lo))
    cos = (c_hi * c_lo - s_hi * s_lo).reshape(RADIX, chunk, chunk)
    sin = (s_hi * c_lo + c_hi * s_lo).reshape(RADIX, chunk, chunk)
    return jnp.concatenate([cos, -sin], axis=2).astype(BF16)


def _seq_dft(a, b, dft_m, sg_perm):
    bsz, seq_len, width = a.shape
    chunk = seq_len // RADIX
    ab_spec = pl.BlockSpec((1, seq_len, DFT_COLS), lambda bi, h: (bi, 0, h))
    m_spec = pl.BlockSpec((RADIX, chunk, 2 * chunk), lambda bi, h: (0, 0, 0), pipeline_mode=pl.Buffered(1))
    io_spec = pl.BlockSpec((1, chunk, RADIX * DFT_COLS), lambda bi, h: (bi, 0, h))
    return pl.pallas_call(
        functools.partial(_seq_dft_kernel, chunk=chunk),
        grid=(bsz, width // DFT_COLS),
        in_specs=[ab_spec, ab_spec, m_spec, io_spec],
        out_specs=io_spec,
        out_shape=jax.ShapeDtypeStruct((bsz, chunk, RADIX * width), BF16),
        scratch_shapes=[pltpu.VMEM((RADIX, 2 * chunk, DFT_COLS), BF16)],
        compiler_params=_compiler_params(2),
        name="seq_dft",
    )(a, b, dft_m, sg_perm)


def _out_proj_rows(sub, x_ref, ma_ref, mf_ref, w_ref, o_ref, acc_ref):
    n_cb = D_MODEL // LANES
    sub_rows = acc_ref.shape[2]
    rows = slice(sub * sub_rows, (sub + 1) * sub_rows)
    perm_rows = slice(sub * (sub_rows // RADIX), (sub + 1) * (sub_rows // RADIX))
    y = x_ref[rows, :] + _dot(ma_ref[rows, :], w_ref[:ATTN_WIDTH, :])
    for cb in range(n_cb):
        acc_ref[sub, cb] = y[:, cb * LANES:(cb + 1) * LANES]
    for r in range(RADIX):
        mf_r = jnp.concatenate([mf_ref[perm_rows, _perm_col(r, f):_perm_col(r, f) + DFT_COLS]
                                for f in range(0, FOURIER_WIDTH, DFT_COLS)], axis=1)
        y = _dot(mf_r, w_ref[ATTN_WIDTH:, :])
        for cb in range(n_cb):
            acc_ref[sub, cb, pl.ds(r, sub_rows // RADIX, stride=RADIX), :] += y[:, cb * LANES:(cb + 1) * LANES]
    for cb in range(n_cb):
        o_ref[rows, cb * LANES:(cb + 1) * LANES] = acc_ref[sub, cb]


def _out_proj_kernel(x_ref, ma_ref, mf_ref, w_ref, o_ref, acc_ref):
    for sub in range(x_ref.shape[0] // acc_ref.shape[2]):
        _out_proj_rows(sub, x_ref, ma_ref, mf_ref, w_ref, o_ref, acc_ref)


def _out_in_proj_kernel(x_ref, ma_ref, mf_ref, wo_ref, *refs):
    in_refs = refs[:N_IN_PROJ_INPUTS]
    y_ref = refs[N_IN_PROJ_INPUTS]
    out_refs = refs[N_IN_PROJ_INPUTS + 1:N_IN_PROJ_INPUTS + 1 + N_IN_PROJ_OUTPUTS]
    acc_ref, perm_ref = refs[-2:]
    sub_rows = acc_ref.shape[2]
    for sub in range(x_ref.shape[0] // sub_rows):
        _out_proj_rows(sub, x_ref, ma_ref, mf_ref, wo_ref, y_ref, acc_ref)
        _in_proj_rows(y_ref[sub * sub_rows:(sub + 1) * sub_rows, :], sub, *in_refs, *out_refs, perm_ref)


def _out_proj_specs(tm, sub_rows):
    in_specs = [_row_spec(tm, D_MODEL), _row_spec(tm, ATTN_WIDTH), _row_spec(tm // RADIX, RADIX * FOURIER_WIDTH),
                _resident_spec((D_MODEL, D_MODEL))]
    scratch = pltpu.VMEM((tm // sub_rows, D_MODEL // LANES, sub_rows, LANES), F32)
    return in_specs, scratch


def _out_proj(x2d, ma, mf_perm, w_bf):
    rows = x2d.shape[0]
    tm = ROW_TILE
    in_specs, scratch = _out_proj_specs(tm, tm)
    return pl.pallas_call(
        _out_proj_kernel,
        grid=(rows // tm,),
        in_specs=in_specs,
        out_specs=_row_spec(tm, D_MODEL),
        out_shape=jax.ShapeDtypeStruct((rows, D_MODEL), F32),
        scratch_shapes=[scratch],
        compiler_params=_compiler_params(1),
        name="out_proj",
    )(x2d, ma, mf_perm, w_bf)


def _out_in_proj(x2d, ma, mf_perm, w_bf, seq_len, in_w):
    rows = x2d.shape[0]
    tm = FUSED_ROW_TILE
    o_in_specs, acc_scratch = _out_proj_specs(tm, FUSED_SUB_ROWS)
    i_in_specs, i_out_specs, i_out_shape, perm_scratch = _in_proj_specs(tm, FUSED_SUB_ROWS, rows, seq_len)
    outs = pl.pallas_call(
        _out_in_proj_kernel,
        grid=(rows // tm,),
        in_specs=o_in_specs + i_in_specs,
        out_specs=[_row_spec(tm, D_MODEL)] + i_out_specs,
        out_shape=[jax.ShapeDtypeStruct((rows, D_MODEL), F32)] + i_out_shape,
        scratch_shapes=[acc_scratch, perm_scratch],
        compiler_params=_compiler_params(1, FUSED_VMEM_LIMIT),
        name="out_in_proj",
    )(x2d, ma, mf_perm, w_bf, *in_w)
    return outs[0], outs[1:]


def _rope_tables(seq_len):
    half = HEAD_DIM // 2
    inv_freq = 1.0 / (ROPE_THETA ** (jnp.arange(half, dtype=F32) / half))
    ang = jnp.arange(seq_len, dtype=F32)[:, None] * inv_freq[None, :]
    cos = jnp.cos(ang)
    sin = jnp.sin(ang)
    cos_t = jnp.concatenate([cos, cos, cos, cos], axis=1)
    sin_t = jnp.concatenate([-sin, sin, -sin, sin], axis=1)
    return cos_t, sin_t


def _rope_gains(q_gain, k_gain):
    pair = lambda g: jnp.tile(g, LANES // HEAD_DIM)
    rot = lambda g: jnp.roll(g, HEAD_DIM // 2)
    gq = q_gain.astype(F32) * (HEAD_DIM ** -0.5 * LOG2E)
    gk = k_gain.astype(F32)
    return jnp.stack([pair(gq), pair(rot(gq)), pair(gk), pair(rot(gk))])


def _head_mean_matrix(width):
    head = jnp.arange(width, dtype=jnp.int32) // HEAD_DIM
    return jnp.where(head[:, None] == head[None, :], 1.0 / HEAD_DIM, 0.0).astype(BF16)


def _mixers(proj, bsz, seq_len, dft_m, sink2, bounded):
    q, ka, kb, va, vb, sga, a, b, sgf = proj
    r3 = lambda t: t.reshape(bsz, seq_len, t.shape[-1])
    ma = _attention(sink2, bounded, r3(q), r3(ka), r3(kb), r3(va), r3(vb), r3(sga))
    perm_rows = seq_len // RADIX
    mf = _seq_dft(r3(a), r3(b), dft_m, sgf.reshape(bsz, perm_rows, RADIX * FOURIER_WIDTH))
    return ma.reshape(bsz * seq_len, ATTN_WIDTH), mf.reshape(bsz * perm_rows, RADIX * FOURIER_WIDTH)


def _trunk(x, layers, dft_m):
    bsz, seq_len, _ = x.shape
    x2d = x.reshape(bsz * seq_len, D_MODEL)
    proj = _in_proj(x2d, seq_len, layers[0]["in_w"])
    for l, layer in enumerate(layers):
        ma, mf = _mixers(proj, bsz, seq_len, dft_m, layer["sink2"], layer["bounded"])
        if l + 1 < len(layers):
            x2d, proj = _out_in_proj(x2d, ma, mf, layer["w_out"], seq_len, layers[l + 1]["in_w"])
        else:
            x2d = _out_proj(x2d, ma, mf, layer["w_out"])
    return x2d.reshape(bsz, seq_len, D_MODEL)


def kernel(x_prompt, x_sample, norm_gain, w_in, q_norm_gain, k_norm_gain, sink_logit, w_fourier, w_out):
    depth = norm_gain.shape[0]
    seq_p, seq_s = x_prompt.shape[1], x_sample.shape[1]
    assert seq_p == seq_s
    cos_t, sin_t = _rope_tables(seq_p)
    hs = _head_mean_matrix(2 * LANES)
    layers = []
    for l in range(depth):
        sink2 = sink_logit[l].astype(F32) * LOG2E
        logit_bound = (HEAD_DIM ** 0.5 * LOG2E) * jnp.max(jnp.abs(q_norm_gain[l])) * jnp.max(jnp.abs(k_norm_gain[l]))
        bounded = jnp.maximum(logit_bound, jnp.max(jnp.abs(sink2))) <= MAX_UNSHIFTED_LOGIT2
        in_w = (norm_gain[l].astype(F32)[None, :], w_in[l].astype(BF16), cos_t, sin_t,
                _rope_gains(q_norm_gain[l], k_norm_gain[l]), hs, _fourier_weights(w_fourier[l], seq_p))
        layers.append(dict(in_w=in_w, sink2=sink2, bounded=bounded.astype(jnp.int32)[None],
                           w_out=w_out[l].astype(BF16)))
    dft_m = _dft_matrix(seq_p)
    return (_trunk(x_prompt, layers, dft_m), _trunk(x_sample, layers, dft_m))
```

```python
import functools
import math

import jax
import jax.numpy as jnp
from jax.experimental import pallas as pl
from jax.experimental.pallas import tpu as pltpu

D_MODEL = 1024
HEAD_DIM = 64
N_Q_HEADS = 8
N_KV_HEADS = 2
ATTN_WIDTH = N_Q_HEADS * HEAD_DIM
KV_WIDTH = N_KV_HEADS * HEAD_DIM
FOURIER_WIDTH = D_MODEL - ATTN_WIDTH
N_GROUPS = 4
GROUP_DIM = FOURIER_WIDTH // N_GROUPS
IN_WIDTH = 2 * ATTN_WIDTH + 2 * KV_WIDTH + 2 * FOURIER_WIDTH
Q_OFF = 0
K_OFF = ATTN_WIDTH
V_OFF = K_OFF + KV_WIDTH
GA_OFF = V_OFF + KV_WIDTH
U_OFF = GA_OFF + ATTN_WIDTH
GF_OFF = U_OFF + FOURIER_WIDTH
BLOCK = 128
ROPE_THETA = 10000.0
EPS = 1e-6
NEG = -1e30
LANES = 128
RADIX = 8
DFT_COLS = 256
DFT_BLOCK_ROWS = 128
LOG2E = math.log2(math.e)
MAX_UNSHIFTED_LOGIT2 = 100.0

ROW_TILE = 1024
FUSED_ROW_TILE = 1024
FUSED_SUB_ROWS = 512
ATTN_TILE = 1024
VMEM_LIMIT = 48 * 1024 * 1024
FUSED_VMEM_LIMIT = 54 * 1024 * 1024

BF16 = jnp.bfloat16
F32 = jnp.float32


def _compiler_params(grid_rank, vmem_limit=VMEM_LIMIT):
    return pltpu.CompilerParams(dimension_semantics=("arbitrary",) * grid_rank, vmem_limit_bytes=vmem_limit)


def _dot(a, b):
    return jnp.dot(a, b, preferred_element_type=F32)


def _dot_narrow(a, b):
    half = a.shape[0] // 2
    return jnp.concatenate([_dot(a[:half], b), _dot(a[half:], b)], axis=0)


def _dot_nt(a, b):
    return jax.lax.dot_general(a, b, (((1,), (1,)), ((), ())), preferred_element_type=F32)


def _silu(x):
    return x / (1.0 + jnp.exp(-x))


def _rotate_half(t, first_half):
    width = t.shape[-1]
    fwd = pltpu.roll(t, HEAD_DIM // 2, axis=1)
    bwd = pltpu.roll(t, width - HEAD_DIM // 2, axis=1)
    return jnp.where(first_half, bwd, fwd)


def _fourier_weight_kernel(cs_ref, w_ref, o_ref):
    for g in range(N_GROUPS):
        w = w_ref[g]
        c = jnp.dot(cs_ref[0], w, preferred_element_type=F32, precision=jax.lax.Precision.HIGHEST)
        s = jnp.dot(cs_ref[1], w, preferred_element_type=F32, precision=jax.lax.Precision.HIGHEST)
        o_ref[g] = jnp.concatenate([c, s], axis=1).astype(o_ref.dtype)


def _fourier_weights(w_four, seq_len):
    idx = (jnp.arange(GROUP_DIM, dtype=jnp.int32)[:, None] * jnp.arange(GROUP_DIM, dtype=jnp.int32)[None, :]) % GROUP_DIM
    ang = idx.astype(F32) * (2.0 * math.pi / GROUP_DIM)
    scale = 1.0 / math.sqrt(seq_len * GROUP_DIM)
    cs = jnp.stack([jnp.cos(ang), jnp.sin(ang)]) * scale
    return pl.pallas_call(
        _fourier_weight_kernel,
        out_shape=jax.ShapeDtypeStruct((N_GROUPS, GROUP_DIM, 2 * GROUP_DIM), BF16),
        name="fourier_weight_prep",
    )(cs, w_four)


def _in_proj_rows(x, rows, gn_ref, w_ref, cos_ref, sin_ref, gains_ref, hs_ref, wab_ref,
                  q_ref, ka_ref, kb_ref, va_ref, vb_ref, sga_ref, a_ref, b_ref, sgf_ref):
    sub_rows = x.shape[0]
    xb = (x * gn_ref[...]).astype(BF16)
    cos = cos_ref[rows, :]
    sin = sin_ref[rows, :]
    qcos, qsin = cos * gains_ref[0:1, :], sin * gains_ref[1:2, :]
    kcos, ksin = cos * gains_ref[2:3, :], sin * gains_ref[3:4, :]
    ms = jnp.mean(x * x, axis=-1, keepdims=True)
    r = jnp.broadcast_to(jax.lax.rsqrt(ms + EPS), (sub_rows, LANES))
    eps_z = jnp.broadcast_to(EPS * (ms + EPS), (sub_rows, LANES))
    wide = lambda t, width: jnp.concatenate([t] * (width // LANES), axis=1)

    lane = jax.lax.broadcasted_iota(jnp.int32, (1, LANES), 1)
    first_half_pair = (lane % HEAD_DIM) < (HEAD_DIM // 2)
    low_head = lane < HEAD_DIM
    hs = hs_ref[...]

    def head_mean_sq(z):
        return _dot_narrow((z * z).astype(BF16), hs)

    def silu_of_scaled(z, width):
        h = z * wide(0.5 * r, width)
        return h + h * jnp.tanh(h)

    zq = _dot(xb, w_ref[:, Q_OFF:Q_OFF + ATTN_WIDTH])
    half = ATTN_WIDTH // 2
    ssq = jnp.concatenate([head_mean_sq(zq[:, :half]), head_mean_sq(zq[:, half:])], axis=1)
    qr = (zq * wide(qcos, ATTN_WIDTH)
          + _rotate_half(zq, wide(first_half_pair, ATTN_WIDTH)) * wide(qsin, ATTN_WIDTH))
    q_ref[rows, :] = (qr * jax.lax.rsqrt(ssq + wide(eps_z, ATTN_WIDTH))).astype(q_ref.dtype)

    zkv = _dot_narrow(xb, w_ref[:, K_OFF:K_OFF + 2 * KV_WIDTH])
    zk = zkv[:, :KV_WIDTH]
    ssk = head_mean_sq(zkv)[:, :KV_WIDTH]
    kr = (zk * kcos + _rotate_half(zk, first_half_pair) * ksin) * jax.lax.rsqrt(ssk + eps_z)
    kr_sw = pltpu.roll(kr, HEAD_DIM, axis=1)
    ka_ref[rows, :] = jnp.where(low_head, kr, kr_sw).astype(ka_ref.dtype)
    kb_ref[rows, :] = jnp.where(low_head, kr_sw, kr).astype(kb_ref.dtype)
    zv = zkv[:, KV_WIDTH:] * r
    zv_sw = pltpu.roll(zv, HEAD_DIM, axis=1)
    va_ref[rows, :] = jnp.where(low_head, zv, zv_sw).astype(va_ref.dtype)
    vb_ref[rows, :] = jnp.where(low_head, zv_sw, zv).astype(vb_ref.dtype)

    zga = _dot(xb, w_ref[:, GA_OFF:GA_OFF + ATTN_WIDTH])
    sga_ref[rows, :] = silu_of_scaled(zga, ATTN_WIDTH).astype(sga_ref.dtype)
    zgf = _dot(xb, w_ref[:, GF_OFF:GF_OFF + FOURIER_WIDTH])
    sgf_ref[rows, :] = silu_of_scaled(zgf, FOURIER_WIDTH).astype(sgf_ref.dtype)

    zu = (_dot(xb, w_ref[:, U_OFF:U_OFF + FOURIER_WIDTH]) * wide(r, FOURIER_WIDTH)).astype(BF16)
    for g in range(N_GROUPS):
        ab = _dot_narrow(zu[:, g * GROUP_DIM:(g + 1) * GROUP_DIM], wab_ref[g])
        a_ref[rows, g * GROUP_DIM:(g + 1) * GROUP_DIM] = ab[:, :GROUP_DIM].astype(a_ref.dtype)
        b_ref[rows, g * GROUP_DIM:(g + 1) * GROUP_DIM] = ab[:, GROUP_DIM:].astype(b_ref.dtype)


N_IN_PROJ_INPUTS = 7


def _in_proj_kernel(x_ref, *refs):
    _in_proj_rows(x_ref[...], slice(None), *refs)


def _row_spec(tm, width):
    return pl.BlockSpec((tm, width), lambda i: (i, 0))


def _resident_spec(shape):
    return pl.BlockSpec(shape, lambda i: (0,) * len(shape), pipeline_mode=pl.Buffered(1))


def _in_proj_specs(tm, rows, seq_len):
    steps_per_seq = seq_len // tm
    tab_spec = pl.BlockSpec((tm, LANES), lambda i: (i % steps_per_seq, 0))
    in_specs = [_resident_spec((1, D_MODEL)), _resident_spec((D_MODEL, IN_WIDTH)), tab_spec, tab_spec,
                _resident_spec((4, LANES)), _resident_spec((2 * LANES, 2 * LANES)),
                _resident_spec((N_GROUPS, GROUP_DIM, 2 * GROUP_DIM))]
    out_widths = (ATTN_WIDTH, LANES, LANES, LANES, LANES, ATTN_WIDTH, FOURIER_WIDTH, FOURIER_WIDTH, FOURIER_WIDTH)
    out_specs = [_row_spec(tm, w) for w in out_widths]
    out_shape = [jax.ShapeDtypeStruct((rows, w), BF16) for w in out_widths]
    return in_specs, out_specs, out_shape


def _in_proj(x2d, seq_len, in_w):
    rows = x2d.shape[0]
    tm = ROW_TILE
    in_specs, out_specs, out_shape = _in_proj_specs(tm, rows, seq_len)
    return pl.pallas_call(
        _in_proj_kernel,
        grid=(rows // tm,),
        in_specs=[_row_spec(tm, D_MODEL)] + in_specs,
        out_specs=out_specs,
        out_shape=out_shape,
        compiler_params=_compiler_params(1),
        name="in_proj",
    )(x2d, *in_w)


def _attn_kernel(sink_ref, bounded_ref, q_ref, ka_ref, kb_ref, va_ref, vb_ref, sg_ref, o_ref, *, n_blocks):
    refs = (sink_ref, q_ref, ka_ref, kb_ref, va_ref, vb_ref, sg_ref, o_ref)

    @pl.when(bounded_ref[0] == 1)
    def _():
        _attn_body(*refs, n_blocks=n_blocks, shift=False)

    @pl.when(bounded_ref[0] == 0)
    def _():
        _attn_body(*refs, n_blocks=n_blocks, shift=True)


def _attn_body(sink_ref, q_ref, ka_ref, kb_ref, va_ref, vb_ref, sg_ref, o_ref, *, n_blocks, shift):
    i = pl.program_id(1)
    blocks_per_step = ATTN_TILE // BLOCK
    row = jax.lax.broadcasted_iota(jnp.int32, (BLOCK, 2 * BLOCK), 0)
    col = jax.lax.broadcasted_iota(jnp.int32, (BLOCK, 2 * BLOCK), 1) % BLOCK
    lane = jax.lax.broadcasted_iota(jnp.int32, (1, LANES), 1)
    low = lane < HEAD_DIM
    zero = jnp.zeros((), BF16)
    lane_full = jax.lax.broadcasted_iota(jnp.int32, (BLOCK, LANES), 1)
    ones_low = jnp.where(lane_full < HEAD_DIM, 1.0, 0.0).astype(BF16)
    ones_high = jnp.where(lane_full < HEAD_DIM, 0.0, 1.0).astype(BF16)

    for jb in range(blocks_per_step):
        ib = i * blocks_per_step + jb
        rows = slice(jb * BLOCK, (jb + 1) * BLOCK)
        edge_prev = jnp.where(ib == 0, NEG, 0.0)
        edge_next = jnp.where(ib == n_blocks - 1, NEG, 0.0)
        bias_prev = jnp.where(col >= row, 0.0, NEG) + edge_prev
        bias_next = jnp.where(col <= row, 0.0, NEG) + edge_next
        starts = [pl.multiple_of(jnp.clip(ib + c, 0, n_blocks - 1) * BLOCK, BLOCK) for c in (-1, 0, 1)]
        for kvh, (k_ref, v_ref) in enumerate(((ka_ref, va_ref), (kb_ref, vb_ref))):
            kbd, vbd = [], []
            for st in starts:
                kblk = k_ref[0, pl.ds(st, BLOCK), :]
                vblk = v_ref[0, pl.ds(st, BLOCK), :]
                kbd.append(jnp.concatenate([jnp.where(low, kblk, zero), jnp.where(low, zero, kblk)], axis=0))
                vbd.append(jnp.concatenate([
                    jnp.concatenate([jnp.where(low, vblk, zero), ones_low], axis=1),
                    jnp.concatenate([jnp.where(low, zero, vblk), ones_high], axis=1)], axis=0))
            vbd = jnp.concatenate(vbd, axis=0)
            for pair in range(2):
                pidx = kvh * 2 + pair
                lanes = slice(pidx * LANES, (pidx + 1) * LANES)
                qp = q_ref[0, rows, lanes]
                scores = [_dot_nt(qp, kb) for kb in kbd]
                sink_e = sink_ref[2 * pidx]
                sink_o = sink_ref[2 * pidx + 1]
                if shift:
                    scores = [scores[0] + bias_prev, scores[1], scores[2] + bias_next]
                    smax = jnp.maximum(jnp.maximum(scores[0], scores[1]), scores[2])
                    m_e = jnp.maximum(jnp.max(smax[:, :BLOCK], axis=-1, keepdims=True), sink_e)
                    m_o = jnp.maximum(jnp.max(smax[:, BLOCK:], axis=-1, keepdims=True), sink_o)
                    m_both = jnp.concatenate([jnp.broadcast_to(m_e, (BLOCK, BLOCK)),
                                              jnp.broadcast_to(m_o, (BLOCK, BLOCK))], axis=1)
                    probs = [jnp.exp2(s - m_both).astype(BF16) for s in scores]
                    p_sink = jnp.where(low, jnp.exp2(sink_e - m_e), jnp.exp2(sink_o - m_o))
                else:
                    scores = [s.astype(BF16) for s in scores]
                    scores = [scores[0] + bias_prev.astype(BF16), scores[1], scores[2] + bias_next.astype(BF16)]
                    probs = [jnp.exp2(s) for s in scores]
                    p_sink = jnp.exp2(jnp.where(low, sink_e, sink_o))
                acc = _dot(jnp.concatenate(probs, axis=1), vbd)
                out = acc[:, :LANES] / (acc[:, LANES:] + p_sink) * sg_ref[0, rows, lanes].astype(F32)
                o_ref[0, rows, lanes] = out.astype(o_ref.dtype)


def _attention(sink2, bounded, q, ka, kb, va, vb, sga):
    bsz, seq_len, _ = q.shape
    tq = ATTN_TILE
    q_spec = pl.BlockSpec((1, tq, ATTN_WIDTH), lambda b, i, s, f: (b, i, 0))
    kv_spec = pl.BlockSpec((1, seq_len, LANES), lambda b, i, s, f: (b, 0, 0))
    grid_spec = pltpu.PrefetchScalarGridSpec(
        num_scalar_prefetch=2,
        grid=(bsz, seq_len // tq),
        in_specs=[q_spec, kv_spec, kv_spec, kv_spec, kv_spec, q_spec],
        out_specs=q_spec,
    )
    return pl.pallas_call(
        functools.partial(_attn_kernel, n_blocks=seq_len // BLOCK),
        grid_spec=grid_spec,
        out_shape=jax.ShapeDtypeStruct((bsz, seq_len, ATTN_WIDTH), BF16),
        compiler_params=_compiler_params(2),
        name="band_attention",
    )(sink2, bounded, q, ka, kb, va, vb, sga)


def _seq_dft_kernel(a_ref, b_ref, m_ref, o_ref, zz_ref, nat_ref, *, chunk):
    half = RADIX // 2
    root_half = math.sqrt(0.5)
    cadd = lambda u, v: (u[0] + v[0], u[1] + v[1])
    csub = lambda u, v: (u[0] - v[0], u[1] - v[1])
    add_i = lambda u, v: (u[0] - v[1], u[1] + v[0])
    sub_i = lambda u, v: (u[0] + v[1], u[1] - v[0])

    def four_point(w):
        s02, d02, s13, d13 = cadd(w[0], w[2]), csub(w[0], w[2]), cadd(w[1], w[3]), csub(w[1], w[3])
        return [cadd(s02, s13), add_i(d02, d13), csub(s02, s13), sub_i(d02, d13)]

    def butterflies(parity):
        for rb in range(chunk // DFT_BLOCK_ROWS):
            for cb in range(DFT_COLS // LANES):
                rows = lambda q: slice(q * chunk + rb * DFT_BLOCK_ROWS, q * chunk + (rb + 1) * DFT_BLOCK_ROWS)
                cols = slice(cb * LANES, (cb + 1) * LANES)
                z = [(a_ref[0, rows(q), cols], b_ref[0, rows(q), cols]) for q in range(RADIX)]
                if parity == 0:
                    w = [cadd(z[q], z[q + half]) for q in range(half)]
                else:
                    d = [csub(z[q], z[q + half]) for q in range(half)]
                    w = [d[0],
                         ((d[1][0] - d[1][1]) * root_half, (d[1][0] + d[1][1]) * root_half),
                         (-d[2][1], d[2][0]),
                         ((-d[3][0] - d[3][1]) * root_half, (d[3][0] - d[3][1]) * root_half)]
                for t, (re, im) in enumerate(four_point(w)):
                    zz_ref[2 * t + parity, rows(0), cols] = re
                    zz_ref[2 * t + parity, rows(1), cols] = im

    def project(r):
        f = _dot(m_ref[r], zz_ref[r])
        for cb in range(DFT_COLS // LANES):
            nat_ref[cb, pl.ds(r, chunk, stride=RADIX), :] = f[:, cb * LANES:(cb + 1) * LANES]

    butterflies(0)
    for r in range(0, RADIX, 2):
        project(r)
    butterflies(1)
    for r in range(1, RADIX, 2):
        project(r)
    for cb in range(DFT_COLS // LANES):
        o_ref[0, :, cb * LANES:(cb + 1) * LANES] = nat_ref[cb].astype(o_ref.dtype)


def _dft_matrix(seq_len):
    chunk = seq_len // RADIX
    split = 32
    period = seq_len // (RADIX * split)
    m = jnp.arange(chunk, dtype=jnp.int32)[None, :]
    ang_hi = ((jnp.arange(chunk // split, dtype=jnp.int32)[:, None] * m) % period).astype(F32) * (2.0 * math.pi / period)
    ang_lo = ((jnp.arange(RADIX * split, dtype=jnp.int32)[:, None] * m) % seq_len).astype(F32) * (2.0 * math.pi / seq_len)
    c_hi, s_hi = jnp.cos(ang_hi)[None, :, None, :], jnp.sin(ang_hi)[None, :, None, :]
    lo = lambda t: t.reshape(split, RADIX, chunk).transpose(1, 0, 2)[:, None, :, :]
    c_lo, s_lo = lo(jnp.cos(ang_lo)), lo(jnp.sin(ang_lo))
    cos = (c_hi * c_lo - s_hi * s_lo).reshape(RADIX, chunk, chunk)
    sin = (s_hi * c_lo + c_hi * s_lo).reshape(RADIX, chunk, chunk)
    return jnp.concatenate([cos, -sin], axis=2).astype(BF16)


def _seq_dft(a, b, dft_m):
    bsz, seq_len, width = a.shape
    chunk = seq_len // RADIX
    col_spec = pl.BlockSpec((1, seq_len, DFT_COLS), lambda bi, h: (bi, 0, h))
    m_spec = pl.BlockSpec((RADIX, chunk, 2 * chunk), lambda bi, h: (0, 0, 0), pipeline_mode=pl.Buffered(1))
    slabs = pltpu.VMEM((DFT_COLS // LANES, seq_len, LANES), F32)
    return pl.pallas_call(
        functools.partial(_seq_dft_kernel, chunk=chunk),
        grid=(bsz, width // DFT_COLS),
        in_specs=[col_spec, col_spec, m_spec],
        out_specs=col_spec,
        out_shape=jax.ShapeDtypeStruct((bsz, seq_len, width), BF16),
        scratch_shapes=[pltpu.VMEM((RADIX, 2 * chunk, DFT_COLS), BF16), slabs],
        compiler_params=_compiler_params(2),
        name="seq_dft",
    )(a, b, dft_m)


def _out_proj_rows(rows, x_ref, ma_ref, f_ref, sgf_ref, w_ref, o_ref):
    mf = f_ref[rows, :] * sgf_ref[rows, :]
    o_ref[rows, :] = (x_ref[rows, :] + _dot(ma_ref[rows, :], w_ref[:ATTN_WIDTH, :])
                      + _dot(mf, w_ref[ATTN_WIDTH:, :]))


def _out_proj_kernel(x_ref, ma_ref, f_ref, sgf_ref, w_ref, o_ref):
    _out_proj_rows(slice(None), x_ref, ma_ref, f_ref, sgf_ref, w_ref, o_ref)


N_OUT_PROJ_INPUTS = 5


def _out_in_proj_kernel(*refs):
    out_in = refs[:N_OUT_PROJ_INPUTS]
    in_refs = refs[N_OUT_PROJ_INPUTS:N_OUT_PROJ_INPUTS + N_IN_PROJ_INPUTS]
    y_ref = refs[N_OUT_PROJ_INPUTS + N_IN_PROJ_INPUTS]
    out_refs = refs[N_OUT_PROJ_INPUTS + N_IN_PROJ_INPUTS + 1:]
    for sub in range(y_ref.shape[0] // FUSED_SUB_ROWS):
        rows = slice(sub * FUSED_SUB_ROWS, (sub + 1) * FUSED_SUB_ROWS)
        _out_proj_rows(rows, *out_in, y_ref)
        _in_proj_rows(y_ref[rows, :], rows, *in_refs, *out_refs)


def _out_proj_specs(tm):
    return [_row_spec(tm, D_MODEL), _row_spec(tm, ATTN_WIDTH), _row_spec(tm, FOURIER_WIDTH),
            _row_spec(tm, FOURIER_WIDTH), _resident_spec((D_MODEL, D_MODEL))]


def _out_proj(x2d, ma, f, sgf, w_bf):
    rows = x2d.shape[0]
    tm = ROW_TILE
    return pl.pallas_call(
        _out_proj_kernel,
        grid=(rows // tm,),
        in_specs=_out_proj_specs(tm),
        out_specs=_row_spec(tm, D_MODEL),
        out_shape=jax.ShapeDtypeStruct((rows, D_MODEL), F32),
        compiler_params=_compiler_params(1),
        name="out_proj",
    )(x2d, ma, f, sgf, w_bf)


def _out_in_proj(x2d, ma, f, sgf, w_bf, seq_len, in_w):
    rows = x2d.shape[0]
    tm = FUSED_ROW_TILE
    i_in_specs, i_out_specs, i_out_shape = _in_proj_specs(tm, rows, seq_len)
    outs = pl.pallas_call(
        _out_in_proj_kernel,
        grid=(rows // tm,),
        in_specs=_out_proj_specs(tm) + i_in_specs,
        out_specs=[_row_spec(tm, D_MODEL)] + i_out_specs,
        out_shape=[jax.ShapeDtypeStruct((rows, D_MODEL), F32)] + i_out_shape,
        compiler_params=_compiler_params(1, FUSED_VMEM_LIMIT),
        name="out_in_proj",
    )(x2d, ma, f, sgf, w_bf, *in_w)
    return outs[0], outs[1:]


def _rope_tables(seq_len):
    half = HEAD_DIM // 2
    inv_freq = 1.0 / (ROPE_THETA ** (jnp.arange(half, dtype=F32) / half))
    ang = jnp.arange(seq_len, dtype=F32)[:, None] * inv_freq[None, :]
    cos = jnp.cos(ang)
    sin = jnp.sin(ang)
    cos_t = jnp.concatenate([cos, cos, cos, cos], axis=1)
    sin_t = jnp.concatenate([-sin, sin, -sin, sin], axis=1)
    return cos_t, sin_t


def _rope_gains(q_gain, k_gain):
    pair = lambda g: jnp.tile(g, LANES // HEAD_DIM)
    rot = lambda g: jnp.roll(g, HEAD_DIM // 2)
    gq = q_gain.astype(F32) * (HEAD_DIM ** -0.5 * LOG2E)
    gk = k_gain.astype(F32)
    return jnp.stack([pair(gq), pair(rot(gq)), pair(gk), pair(rot(gk))])


def _head_mean_matrix(width):
    head = jnp.arange(width, dtype=jnp.int32) // HEAD_DIM
    return jnp.where(head[:, None] == head[None, :], 1.0 / HEAD_DIM, 0.0).astype(BF16)


def _mixers(proj, bsz, seq_len, dft_m, sink2, bounded):
    q, ka, kb, va, vb, sga, a, b, sgf = proj
    r3 = lambda t: t.reshape(bsz, seq_len, t.shape[-1])
    ma = _attention(sink2, bounded, r3(q), r3(ka), r3(kb), r3(va), r3(vb), r3(sga))
    f = _seq_dft(r3(a), r3(b), dft_m)
    return ma.reshape(bsz * seq_len, ATTN_WIDTH), f.reshape(bsz * seq_len, FOURIER_WIDTH), sgf


def _trunk(x, layers, dft_m):
    bsz, seq_len, _ = x.shape
    x2d = x.reshape(bsz * seq_len, D_MODEL)
    proj = _in_proj(x2d, seq_len, layers[0]["in_w"])
    for l, layer in enumerate(layers):
        ma, f, sgf = _mixers(proj, bsz, seq_len, dft_m, layer["sink2"], layer["bounded"])
        if l + 1 < len(layers):
            x2d, proj = _out_in_proj(x2d, ma, f, sgf, layer["w_out"], seq_len, layers[l + 1]["in_w"])
        else:
            x2d = _out_proj(x2d, ma, f, sgf, layer["w_out"])
    return x2d.reshape(bsz, seq_len, D_MODEL)


def kernel(x_prompt, x_sample, norm_gain, w_in, q_norm_gain, k_norm_gain, sink_logit, w_fourier, w_out):
    depth = norm_gain.shape[0]
    seq_p, seq_s = x_prompt.shape[1], x_sample.shape[1]
    assert seq_p == seq_s
    cos_t, sin_t = _rope_tables(seq_p)
    hs = _head_mean_matrix(2 * LANES)
    layers = []
    for l in range(depth):
        sink2 = sink_logit[l].astype(F32) * LOG2E
        logit_bound = (HEAD_DIM ** 0.5 * LOG2E) * jnp.max(jnp.abs(q_norm_gain[l])) * jnp.max(jnp.abs(k_norm_gain[l]))
        bounded = jnp.maximum(logit_bound, jnp.max(jnp.abs(sink2))) <= MAX_UNSHIFTED_LOGIT2
        in_w = (norm_gain[l].astype(F32)[None, :], w_in[l].astype(BF16), cos_t, sin_t,
                _rope_gains(q_norm_gain[l], k_norm_gain[l]), hs, _fourier_weights(w_fourier[l], seq_p))
        layers.append(dict(in_w=in_w, sink2=sink2, bounded=bounded.astype(jnp.int32)[None],
                           w_out=w_out[l].astype(BF16)))
    dft_m = _dft_matrix(seq_p)
    return (_trunk(x_prompt, layers, dft_m), _trunk(x_sample, layers, dft_m))
```

```python
import functools
import math

import jax
import jax.numpy as jnp
from jax.experimental import pallas as pl
from jax.experimental.pallas import tpu as pltpu

D_MODEL = 1024
HEAD_DIM = 64
N_Q_HEADS = 8
N_KV_HEADS = 2
ATTN_WIDTH = N_Q_HEADS * HEAD_DIM
KV_WIDTH = N_KV_HEADS * HEAD_DIM
FOURIER_WIDTH = D_MODEL - ATTN_WIDTH
N_GROUPS = 4
GROUP_DIM = FOURIER_WIDTH // N_GROUPS
IN_WIDTH = 2 * ATTN_WIDTH + 2 * KV_WIDTH + 2 * FOURIER_WIDTH
Q_OFF = 0
K_OFF = ATTN_WIDTH
V_OFF = K_OFF + KV_WIDTH
GA_OFF = V_OFF + KV_WIDTH
U_OFF = GA_OFF + ATTN_WIDTH
GF_OFF = U_OFF + FOURIER_WIDTH
BLOCK = 128
ROPE_THETA = 10000.0
EPS = 1e-6
NEG = -1e30
LANES = 128
RADIX = 8
DFT_COLS = 256
DFT_BLOCK_ROWS = 128
LOG2E = math.log2(math.e)
MAX_UNSHIFTED_LOGIT2 = 100.0

ROW_TILE = 1024
FUSED_ROW_TILE = 1024
FUSED_SUB_ROWS = 512
ATTN_TILE = 1024
VMEM_LIMIT = 48 * 1024 * 1024
FUSED_VMEM_LIMIT = 54 * 1024 * 1024

BF16 = jnp.bfloat16
F32 = jnp.float32


def _compiler_params(grid_rank, vmem_limit=VMEM_LIMIT):
    return pltpu.CompilerParams(dimension_semantics=("arbitrary",) * grid_rank, vmem_limit_bytes=vmem_limit)


def _dot(a, b):
    return jnp.dot(a, b, preferred_element_type=F32)


def _dot_narrow(a, b):
    half = a.shape[0] // 2
    return jnp.concatenate([_dot(a[:half], b), _dot(a[half:], b)], axis=0)


def _dot_nt(a, b):
    return jax.lax.dot_general(a, b, (((1,), (1,)), ((), ())), preferred_element_type=F32)


def _silu(x):
    return x / (1.0 + jnp.exp(-x))


def _rotate_half(t, first_half):
    width = t.shape[-1]
    fwd = pltpu.roll(t, HEAD_DIM // 2, axis=1)
    bwd = pltpu.roll(t, width - HEAD_DIM // 2, axis=1)
    return jnp.where(first_half, bwd, fwd)


def _fourier_weight_kernel(cs_ref, w_ref, o_ref):
    for g in range(N_GROUPS):
        w = w_ref[g]
        c = jnp.dot(cs_ref[0], w, preferred_element_type=F32, precision=jax.lax.Precision.HIGHEST)
        s = jnp.dot(cs_ref[1], w, preferred_element_type=F32, precision=jax.lax.Precision.HIGHEST)
        o_ref[g] = jnp.concatenate([c, s], axis=1).astype(o_ref.dtype)


def _fourier_weights(w_four, seq_len):
    idx = (jnp.arange(GROUP_DIM, dtype=jnp.int32)[:, None] * jnp.arange(GROUP_DIM, dtype=jnp.int32)[None, :]) % GROUP_DIM
    ang = idx.astype(F32) * (2.0 * math.pi / GROUP_DIM)
    scale = 1.0 / math.sqrt(seq_len * GROUP_DIM)
    cs = jnp.stack([jnp.cos(ang), jnp.sin(ang)]) * scale
    return pl.pallas_call(
        _fourier_weight_kernel,
        out_shape=jax.ShapeDtypeStruct((N_GROUPS, GROUP_DIM, 2 * GROUP_DIM), BF16),
        name="fourier_weight_prep",
    )(cs, w_four)


def _in_proj_rows(x, rows, gn_ref, w_ref, cos_ref, sin_ref, gains_ref, hs_ref, wab_ref,
                  q_ref, ka_ref, kb_ref, va_ref, vb_ref, sga_ref, a_ref, b_ref, sgf_ref):
    sub_rows = x.shape[0]
    xb = (x * gn_ref[...]).astype(BF16)
    cos = cos_ref[rows, :]
    sin = sin_ref[rows, :]
    qcos, qsin = cos * gains_ref[0:1, :], sin * gains_ref[1:2, :]
    kcos, ksin = cos * gains_ref[2:3, :], sin * gains_ref[3:4, :]
    ms = jnp.mean(x * x, axis=-1, keepdims=True)
    r = jnp.broadcast_to(jax.lax.rsqrt(ms + EPS), (sub_rows, LANES))
    eps_z = jnp.broadcast_to(EPS * (ms + EPS), (sub_rows, LANES))
    wide = lambda t, width: jnp.concatenate([t] * (width // LANES), axis=1)

    lane = jax.lax.broadcasted_iota(jnp.int32, (1, LANES), 1)
    first_half_pair = (lane % HEAD_DIM) < (HEAD_DIM // 2)
    low_head = lane < HEAD_DIM
    hs = hs_ref[...]

    def head_mean_sq(z):
        return _dot_narrow((z * z).astype(BF16), hs)

    def silu_of_scaled(z, width):
        h = z * wide(0.5 * r, width)
        return h + h * jnp.tanh(h)

    zq = _dot(xb, w_ref[:, Q_OFF:Q_OFF + ATTN_WIDTH])
    half = ATTN_WIDTH // 2
    ssq = jnp.concatenate([head_mean_sq(zq[:, :half]), head_mean_sq(zq[:, half:])], axis=1)
    qr = (zq * wide(qcos, ATTN_WIDTH)
          + _rotate_half(zq, wide(first_half_pair, ATTN_WIDTH)) * wide(qsin, ATTN_WIDTH))
    q_ref[rows, :] = (qr * jax.lax.rsqrt(ssq + wide(eps_z, ATTN_WIDTH))).astype(q_ref.dtype)

    zkv = _dot_narrow(xb, w_ref[:, K_OFF:K_OFF + 2 * KV_WIDTH])
    zk = zkv[:, :KV_WIDTH]
    ssk = head_mean_sq(zkv)[:, :KV_WIDTH]
    kr = (zk * kcos + _rotate_half(zk, first_half_pair) * ksin) * jax.lax.rsqrt(ssk + eps_z)
    kr_sw = pltpu.roll(kr, HEAD_DIM, axis=1)
    ka_ref[rows, :] = jnp.where(low_head, kr, kr_sw).astype(ka_ref.dtype)
    kb_ref[rows, :] = jnp.where(low_head, kr_sw, kr).astype(kb_ref.dtype)
    zv = zkv[:, KV_WIDTH:] * r
    zv_sw = pltpu.roll(zv, HEAD_DIM, axis=1)
    va_ref[rows, :] = jnp.where(low_head, zv, zv_sw).astype(va_ref.dtype)
    vb_ref[rows, :] = jnp.where(low_head, zv_sw, zv).astype(vb_ref.dtype)

    zga = _dot(xb, w_ref[:, GA_OFF:GA_OFF + ATTN_WIDTH])
    sga_ref[rows, :] = silu_of_scaled(zga, ATTN_WIDTH).astype(sga_ref.dtype)
    zgf = _dot(xb, w_ref[:, GF_OFF:GF_OFF + FOURIER_WIDTH])
    sgf_ref[rows, :] = silu_of_scaled(zgf, FOURIER_WIDTH).astype(sgf_ref.dtype)

    zu = (_dot(xb, w_ref[:, U_OFF:U_OFF + FOURIER_WIDTH]) * wide(r, FOURIER_WIDTH)).astype(BF16)
    for g in range(N_GROUPS):
        ab = _dot_narrow(zu[:, g * GROUP_DIM:(g + 1) * GROUP_DIM], wab_ref[g])
        half_idx, col = divmod(g * GROUP_DIM, DFT_COLS)
        a_ref[half_idx, rows, col:col + GROUP_DIM] = ab[:, :GROUP_DIM].astype(a_ref.dtype)
        b_ref[half_idx, rows, col:col + GROUP_DIM] = ab[:, GROUP_DIM:].astype(b_ref.dtype)


N_IN_PROJ_INPUTS = 7


def _in_proj_kernel(x_ref, *refs):
    _in_proj_rows(x_ref[...], slice(None), *refs)


def _row_spec(tm, width):
    return pl.BlockSpec((tm, width), lambda i: (i, 0))


def _split_spec(tm):
    return pl.BlockSpec((FOURIER_WIDTH // DFT_COLS, tm, DFT_COLS), lambda i: (0, i, 0))


def _resident_spec(shape):
    return pl.BlockSpec(shape, lambda i: (0,) * len(shape), pipeline_mode=pl.Buffered(1))


def _in_proj_specs(tm, rows, seq_len):
    steps_per_seq = seq_len // tm
    tab_spec = pl.BlockSpec((tm, LANES), lambda i: (i % steps_per_seq, 0))
    in_specs = [_resident_spec((1, D_MODEL)), _resident_spec((D_MODEL, IN_WIDTH)), tab_spec, tab_spec,
                _resident_spec((4, LANES)), _resident_spec((2 * LANES, 2 * LANES)),
                _resident_spec((N_GROUPS, GROUP_DIM, 2 * GROUP_DIM))]
    row_widths = (ATTN_WIDTH, LANES, LANES, LANES, LANES, ATTN_WIDTH)
    out_specs = ([_row_spec(tm, w) for w in row_widths] + [_split_spec(tm)] * 2
                 + [_row_spec(tm, FOURIER_WIDTH)])
    out_shape = ([jax.ShapeDtypeStruct((rows, w), BF16) for w in row_widths]
                 + [jax.ShapeDtypeStruct((FOURIER_WIDTH // DFT_COLS, rows, DFT_COLS), BF16)] * 2
                 + [jax.ShapeDtypeStruct((rows, FOURIER_WIDTH), BF16)])
    return in_specs, out_specs, out_shape


def _in_proj(x2d, seq_len, in_w):
    rows = x2d.shape[0]
    tm = ROW_TILE
    in_specs, out_specs, out_shape = _in_proj_specs(tm, rows, seq_len)
    return pl.pallas_call(
        _in_proj_kernel,
        grid=(rows // tm,),
        in_specs=[_row_spec(tm, D_MODEL)] + in_specs,
        out_specs=out_specs,
        out_shape=out_shape,
        compiler_params=_compiler_params(1),
        name="in_proj",
    )(x2d, *in_w)


def _attn_kernel(sink_ref, bounded_ref, q_ref, ka_ref, kb_ref, va_ref, vb_ref, sg_ref, o_ref, *, n_blocks):
    refs = (sink_ref, q_ref, ka_ref, kb_ref, va_ref, vb_ref, sg_ref, o_ref)

    @pl.when(bounded_ref[0] == 1)
    def _():
        _attn_body(*refs, n_blocks=n_blocks, shift=False)

    @pl.when(bounded_ref[0] == 0)
    def _():
        _attn_body(*refs, n_blocks=n_blocks, shift=True)


def _attn_body(sink_ref, q_ref, ka_ref, kb_ref, va_ref, vb_ref, sg_ref, o_ref, *, n_blocks, shift):
    i = pl.program_id(1)
    blocks_per_step = ATTN_TILE // BLOCK
    row = jax.lax.broadcasted_iota(jnp.int32, (BLOCK, 2 * BLOCK), 0)
    col = jax.lax.broadcasted_iota(jnp.int32, (BLOCK, 2 * BLOCK), 1) % BLOCK
    lane = jax.lax.broadcasted_iota(jnp.int32, (1, LANES), 1)
    low = lane < HEAD_DIM
    zero = jnp.zeros((), BF16)
    lane_full = jax.lax.broadcasted_iota(jnp.int32, (BLOCK, LANES), 1)
    ones_low = jnp.where(lane_full < HEAD_DIM, 1.0, 0.0).astype(BF16)
    ones_high = jnp.where(lane_full < HEAD_DIM, 0.0, 1.0).astype(BF16)

    for jb in range(blocks_per_step):
        ib = i * blocks_per_step + jb
        rows = slice(jb * BLOCK, (jb + 1) * BLOCK)
        edge_prev = jnp.where(ib == 0, NEG, 0.0)
        edge_next = jnp.where(ib == n_blocks - 1, NEG, 0.0)
        bias_prev = jnp.where(col >= row, 0.0, NEG) + edge_prev
        bias_next = jnp.where(col <= row, 0.0, NEG) + edge_next
        starts = [pl.multiple_of(jnp.clip(ib + c, 0, n_blocks - 1) * BLOCK, BLOCK) for c in (-1, 0, 1)]
        for kvh, (k_ref, v_ref) in enumerate(((ka_ref, va_ref), (kb_ref, vb_ref))):
            kbd, vbd = [], []
            for st in starts:
                kblk = k_ref[0, pl.ds(st, BLOCK), :]
                vblk = v_ref[0, pl.ds(st, BLOCK), :]
                kbd.append(jnp.concatenate([jnp.where(low, kblk, zero), jnp.where(low, zero, kblk)], axis=0))
                vbd.append(jnp.concatenate([
                    jnp.concatenate([jnp.where(low, vblk, zero), ones_low], axis=1),
                    jnp.concatenate([jnp.where(low, zero, vblk), ones_high], axis=1)], axis=0))
            vbd = jnp.concatenate(vbd, axis=0)
            for pair in range(2):
                pidx = kvh * 2 + pair
                lanes = slice(pidx * LANES, (pidx + 1) * LANES)
                qp = q_ref[0, rows, lanes]
                scores = [_dot_nt(qp, kb) for kb in kbd]
                sink_e = sink_ref[2 * pidx]
                sink_o = sink_ref[2 * pidx + 1]
                if shift:
                    scores = [scores[0] + bias_prev, scores[1], scores[2] + bias_next]
                    smax = jnp.maximum(jnp.maximum(scores[0], scores[1]), scores[2])
                    m_e = jnp.maximum(jnp.max(smax[:, :BLOCK], axis=-1, keepdims=True), sink_e)
                    m_o = jnp.maximum(jnp.max(smax[:, BLOCK:], axis=-1, keepdims=True), sink_o)
                    m_both = jnp.concatenate([jnp.broadcast_to(m_e, (BLOCK, BLOCK)),
                                              jnp.broadcast_to(m_o, (BLOCK, BLOCK))], axis=1)
                    probs = [jnp.exp2(s - m_both).astype(BF16) for s in scores]
                    p_sink = jnp.where(low, jnp.exp2(sink_e - m_e), jnp.exp2(sink_o - m_o))
                else:
                    scores = [s.astype(BF16) for s in scores]
                    scores = [scores[0] + bias_prev.astype(BF16), scores[1], scores[2] + bias_next.astype(BF16)]
                    probs = [jnp.exp2(s) for s in scores]
                    p_sink = jnp.exp2(jnp.where(low, sink_e, sink_o))
                acc = _dot(jnp.concatenate(probs, axis=1), vbd)
                out = acc[:, :LANES] / (acc[:, LANES:] + p_sink) * sg_ref[0, rows, lanes].astype(F32)
                o_ref[0, rows, lanes] = out.astype(o_ref.dtype)


def _attention(sink2, bounded, q, ka, kb, va, vb, sga):
    bsz, seq_len, _ = q.shape
    tq = ATTN_TILE
    q_spec = pl.BlockSpec((1, tq, ATTN_WIDTH), lambda b, i, s, f: (b, i, 0))
    kv_spec = pl.BlockSpec((1, seq_len, LANES), lambda b, i, s, f: (b, 0, 0))
    grid_spec = pltpu.PrefetchScalarGridSpec(
        num_scalar_prefetch=2,
        grid=(bsz, seq_len // tq),
        in_specs=[q_spec, kv_spec, kv_spec, kv_spec, kv_spec, q_spec],
        out_specs=q_spec,
    )
    return pl.pallas_call(
        functools.partial(_attn_kernel, n_blocks=seq_len // BLOCK),
        grid_spec=grid_spec,
        out_shape=jax.ShapeDtypeStruct((bsz, seq_len, ATTN_WIDTH), BF16),
        compiler_params=_compiler_params(2),
        name="band_attention",
    )(sink2, bounded, q, ka, kb, va, vb, sga)


def _seq_dft_kernel(a_ref, b_ref, m_ref, o_ref, zz_ref, nat_ref, *, chunk):
    half = RADIX // 2
    root_half = math.sqrt(0.5)
    cadd = lambda u, v: (u[0] + v[0], u[1] + v[1])
    csub = lambda u, v: (u[0] - v[0], u[1] - v[1])
    add_i = lambda u, v: (u[0] - v[1], u[1] + v[0])
    sub_i = lambda u, v: (u[0] + v[1], u[1] - v[0])

    def four_point(w):
        s02, d02, s13, d13 = cadd(w[0], w[2]), csub(w[0], w[2]), cadd(w[1], w[3]), csub(w[1], w[3])
        return [cadd(s02, s13), add_i(d02, d13), csub(s02, s13), sub_i(d02, d13)]

    def butterflies(parity):
        for rb in range(chunk // DFT_BLOCK_ROWS):
            for cb in range(DFT_COLS // LANES):
                rows = lambda q: slice(q * chunk + rb * DFT_BLOCK_ROWS, q * chunk + (rb + 1) * DFT_BLOCK_ROWS)
                cols = slice(cb * LANES, (cb + 1) * LANES)
                z = [(a_ref[0, 0, rows(q), cols], b_ref[0, 0, rows(q), cols]) for q in range(RADIX)]
                if parity == 0:
                    w = [cadd(z[q], z[q + half]) for q in range(half)]
                else:
                    d = [csub(z[q], z[q + half]) for q in range(half)]
                    w = [d[0],
                         ((d[1][0] - d[1][1]) * root_half, (d[1][0] + d[1][1]) * root_half),
                         (-d[2][1], d[2][0]),
                         ((-d[3][0] - d[3][1]) * root_half, (d[3][0] - d[3][1]) * root_half)]
                for t, (re, im) in enumerate(four_point(w)):
                    zz_ref[2 * t + parity, rows(0), cols] = re
                    zz_ref[2 * t + parity, rows(1), cols] = im

    def project(r):
        f = _dot(m_ref[r], zz_ref[r])
        for cb in range(DFT_COLS // LANES):
            nat_ref[cb, pl.ds(r, chunk, stride=RADIX), :] = f[:, cb * LANES:(cb + 1) * LANES]

    butterflies(0)
    for r in range(0, RADIX, 2):
        project(r)
    butterflies(1)
    for r in range(1, RADIX, 2):
        project(r)
    for cb in range(DFT_COLS // LANES):
        o_ref[0, 0, :, cb * LANES:(cb + 1) * LANES] = nat_ref[cb].astype(o_ref.dtype)


def _dft_matrix(seq_len):
    chunk = seq_len // RADIX
    split = 32
    period = seq_len // (RADIX * split)
    m = jnp.arange(chunk, dtype=jnp.int32)[None, :]
    ang_hi = ((jnp.arange(chunk // split, dtype=jnp.int32)[:, None] * m) % period).astype(F32) * (2.0 * math.pi / period)
    ang_lo = ((jnp.arange(RADIX * split, dtype=jnp.int32)[:, None] * m) % seq_len).astype(F32) * (2.0 * math.pi / seq_len)
    c_hi, s_hi = jnp.cos(ang_hi)[None, :, None, :], jnp.sin(ang_hi)[None, :, None, :]
    lo = lambda t: t.reshape(split, RADIX, chunk).transpose(1, 0, 2)[:, None, :, :]
    c_lo, s_lo = lo(jnp.cos(ang_lo)), lo(jnp.sin(ang_lo))
    cos = (c_hi * c_lo - s_hi * s_lo).reshape(RADIX, chunk, chunk)
    sin = (s_hi * c_lo + c_hi * s_lo).reshape(RADIX, chunk, chunk)
    return jnp.concatenate([cos, -sin], axis=2).astype(BF16)


def _seq_dft(a, b, dft_m):
    n_halves, bsz, seq_len, _ = a.shape
    chunk = seq_len // RADIX
    col_spec = pl.BlockSpec((1, 1, seq_len, DFT_COLS), lambda bi, h: (h, bi, 0, 0))
    m_spec = pl.BlockSpec((RADIX, chunk, 2 * chunk), lambda bi, h: (0, 0, 0), pipeline_mode=pl.Buffered(1))
    slabs = pltpu.VMEM((DFT_COLS // LANES, seq_len, LANES), F32)
    return pl.pallas_call(
        functools.partial(_seq_dft_kernel, chunk=chunk),
        grid=(bsz, n_halves),
        in_specs=[col_spec, col_spec, m_spec],
        out_specs=col_spec,
        out_shape=jax.ShapeDtypeStruct(a.shape, BF16),
        scratch_shapes=[pltpu.VMEM((RADIX, 2 * chunk, DFT_COLS), BF16), slabs],
        compiler_params=_compiler_params(2),
        name="seq_dft",
    )(a, b, dft_m)


def _out_proj_rows(rows, x_ref, ma_ref, f_ref, sgf_ref, w_ref, o_ref):
    f = jnp.concatenate([f_ref[h, rows, :] for h in range(FOURIER_WIDTH // DFT_COLS)], axis=1)
    mf = f * sgf_ref[rows, :]
    o_ref[rows, :] = (x_ref[rows, :] + _dot(ma_ref[rows, :], w_ref[:ATTN_WIDTH, :])
                      + _dot(mf, w_ref[ATTN_WIDTH:, :]))


def _out_proj_kernel(x_ref, ma_ref, f_ref, sgf_ref, w_ref, o_ref):
    _out_proj_rows(slice(None), x_ref, ma_ref, f_ref, sgf_ref, w_ref, o_ref)


N_OUT_PROJ_INPUTS = 5


def _out_in_proj_kernel(*refs):
    out_in = refs[:N_OUT_PROJ_INPUTS]
    in_refs = refs[N_OUT_PROJ_INPUTS:N_OUT_PROJ_INPUTS + N_IN_PROJ_INPUTS]
    y_ref = refs[N_OUT_PROJ_INPUTS + N_IN_PROJ_INPUTS]
    out_refs = refs[N_OUT_PROJ_INPUTS + N_IN_PROJ_INPUTS + 1:]
    for sub in range(y_ref.shape[0] // FUSED_SUB_ROWS):
        rows = slice(sub * FUSED_SUB_ROWS, (sub + 1) * FUSED_SUB_ROWS)
        _out_proj_rows(rows, *out_in, y_ref)
        _in_proj_rows(y_ref[rows, :], rows, *in_refs, *out_refs)


def _out_proj_specs(tm):
    return [_row_spec(tm, D_MODEL), _row_spec(tm, ATTN_WIDTH), _split_spec(tm),
            _row_spec(tm, FOURIER_WIDTH), _resident_spec((D_MODEL, D_MODEL))]


def _out_proj(x2d, ma, f, sgf, w_bf):
    rows = x2d.shape[0]
    tm = ROW_TILE
    return pl.pallas_call(
        _out_proj_kernel,
        grid=(rows // tm,),
        in_specs=_out_proj_specs(tm),
        out_specs=_row_spec(tm, D_MODEL),
        out_shape=jax.ShapeDtypeStruct((rows, D_MODEL), F32),
        compiler_params=_compiler_params(1),
        name="out_proj",
    )(x2d, ma, f, sgf, w_bf)


def _out_in_proj(x2d, ma, f, sgf, w_bf, seq_len, in_w):
    rows = x2d.shape[0]
    tm = FUSED_ROW_TILE
    i_in_specs, i_out_specs, i_out_shape = _in_proj_specs(tm, rows, seq_len)
    outs = pl.pallas_call(
        _out_in_proj_kernel,
        grid=(rows // tm,),
        in_specs=_out_proj_specs(tm) + i_in_specs,
        out_specs=[_row_spec(tm, D_MODEL)] + i_out_specs,
        out_shape=[jax.ShapeDtypeStruct((rows, D_MODEL), F32)] + i_out_shape,
        compiler_params=_compiler_params(1, FUSED_VMEM_LIMIT),
        name="out_in_proj",
    )(x2d, ma, f, sgf, w_bf, *in_w)
    return outs[0], outs[1:]


def _rope_tables(seq_len):
    half = HEAD_DIM // 2
    inv_freq = 1.0 / (ROPE_THETA ** (jnp.arange(half, dtype=F32) / half))
    ang = jnp.arange(seq_len, dtype=F32)[:, None] * inv_freq[None, :]
    cos = jnp.cos(ang)
    sin = jnp.sin(ang)
    cos_t = jnp.concatenate([cos, cos, cos, cos], axis=1)
    sin_t = jnp.concatenate([-sin, sin, -sin, sin], axis=1)
    return cos_t, sin_t


def _rope_gains(q_gain, k_gain):
    pair = lambda g: jnp.tile(g, LANES // HEAD_DIM)
    rot = lambda g: jnp.roll(g, HEAD_DIM // 2)
    gq = q_gain.astype(F32) * (HEAD_DIM ** -0.5 * LOG2E)
    gk = k_gain.astype(F32)
    return jnp.stack([pair(gq), pair(rot(gq)), pair(gk), pair(rot(gk))])


def _head_mean_matrix(width):
    head = jnp.arange(width, dtype=jnp.int32) // HEAD_DIM
    return jnp.where(head[:, None] == head[None, :], 1.0 / HEAD_DIM, 0.0).astype(BF16)


def _mixers(proj, bsz, seq_len, dft_m, sink2, bounded):
    q, ka, kb, va, vb, sga, a, b, sgf = proj
    r3 = lambda t: t.reshape(bsz, seq_len, t.shape[-1])
    ma = _attention(sink2, bounded, r3(q), r3(ka), r3(kb), r3(va), r3(vb), r3(sga))
    split4 = lambda t: t.reshape(t.shape[0], bsz, seq_len, DFT_COLS)
    f = _seq_dft(split4(a), split4(b), dft_m)
    return ma.reshape(bsz * seq_len, ATTN_WIDTH), f.reshape(f.shape[0], bsz * seq_len, DFT_COLS), sgf


def _trunk(x, layers, dft_m):
    bsz, seq_len, _ = x.shape
    x2d = x.reshape(bsz * seq_len, D_MODEL)
    proj = _in_proj(x2d, seq_len, layers[0]["in_w"])
    for l, layer in enumerate(layers):
        ma, f, sgf = _mixers(proj, bsz, seq_len, dft_m, layer["sink2"], layer["bounded"])
        if l + 1 < len(layers):
            x2d, proj = _out_in_proj(x2d, ma, f, sgf, layer["w_out"], seq_len, layers[l + 1]["in_w"])
        else:
            x2d = _out_proj(x2d, ma, f, sgf, layer["w_out"])
    return x2d.reshape(bsz, seq_len, D_MODEL)


def kernel(x_prompt, x_sample, norm_gain, w_in, q_norm_gain, k_norm_gain, sink_logit, w_fourier, w_out):
    depth = norm_gain.shape[0]
    seq_p, seq_s = x_prompt.shape[1], x_sample.shape[1]
    assert seq_p == seq_s
    cos_t, sin_t = _rope_tables(seq_p)
    hs = _head_mean_matrix(2 * LANES)
    layers = []
    for l in range(depth):
        sink2 = sink_logit[l].astype(F32) * LOG2E
        logit_bound = (HEAD_DIM ** 0.5 * LOG2E) * jnp.max(jnp.abs(q_norm_gain[l])) * jnp.max(jnp.abs(k_norm_gain[l]))
        bounded = jnp.maximum(logit_bound, jnp.max(jnp.abs(sink2))) <= MAX_UNSHIFTED_LOGIT2
        in_w = (norm_gain[l].astype(F32)[None, :], w_in[l].astype(BF16), cos_t, sin_t,
                _rope_gains(q_norm_gain[l], k_norm_gain[l]), hs, _fourier_weights(w_fourier[l], seq_p))
        layers.append(dict(in_w=in_w, sink2=sink2, bounded=bounded.astype(jnp.int32)[None],
                           w_out=w_out[l].astype(BF16)))
    dft_m = _dft_matrix(seq_p)
    return (_trunk(x_prompt, layers, dft_m), _trunk(x_sample, layers, dft_m))
```

```python
import functools
import math

import jax
import jax.numpy as jnp
from jax.experimental import pallas as pl
from jax.experimental.pallas import tpu as pltpu

D_MODEL = 1024
HEAD_DIM = 64
N_Q_HEADS = 8
N_KV_HEADS = 2
ATTN_WIDTH = N_Q_HEADS * HEAD_DIM
KV_WIDTH = N_KV_HEADS * HEAD_DIM
FOURIER_WIDTH = D_MODEL - ATTN_WIDTH
N_GROUPS = 4
GROUP_DIM = FOURIER_WIDTH // N_GROUPS
IN_WIDTH = 2 * ATTN_WIDTH + 2 * KV_WIDTH + 2 * FOURIER_WIDTH
Q_OFF = 0
K_OFF = ATTN_WIDTH
V_OFF = K_OFF + KV_WIDTH
GA_OFF = V_OFF + KV_WIDTH
U_OFF = GA_OFF + ATTN_WIDTH
GF_OFF = U_OFF + FOURIER_WIDTH
BLOCK = 128
ROPE_THETA = 10000.0
EPS = 1e-6
NEG = -1e30
LANES = 128
RADIX = 8
DFT_COLS = 256
DFT_BLOCK_ROWS = 128
LOG2E = math.log2(math.e)
MAX_UNSHIFTED_LOGIT2 = 100.0

ROW_TILE = 1024
FUSED_ROW_TILE = 1024
FUSED_SUB_ROWS = 512
ATTN_TILE = 1024
VMEM_LIMIT = 48 * 1024 * 1024
FUSED_VMEM_LIMIT = 54 * 1024 * 1024

BF16 = jnp.bfloat16
F32 = jnp.float32


def _compiler_params(grid_rank, vmem_limit=VMEM_LIMIT):
    return pltpu.CompilerParams(dimension_semantics=("arbitrary",) * grid_rank, vmem_limit_bytes=vmem_limit)


def _dot(a, b):
    return jnp.dot(a, b, preferred_element_type=F32)


def _dot_narrow(a, b):
    half = a.shape[0] // 2
    return jnp.concatenate([_dot(a[:half], b), _dot(a[half:], b)], axis=0)


def _dot_nt(a, b):
    return jax.lax.dot_general(a, b, (((1,), (1,)), ((), ())), preferred_element_type=F32)


def _silu(x):
    return x / (1.0 + jnp.exp(-x))


def _rotate_half(t, first_half):
    width = t.shape[-1]
    fwd = pltpu.roll(t, HEAD_DIM // 2, axis=1)
    bwd = pltpu.roll(t, width - HEAD_DIM // 2, axis=1)
    return jnp.where(first_half, bwd, fwd)


def _fourier_weight_kernel(cs_ref, w_ref, o_ref):
    for g in range(N_GROUPS):
        w = w_ref[g]
        c = jnp.dot(cs_ref[0], w, preferred_element_type=F32, precision=jax.lax.Precision.HIGHEST)
        s = jnp.dot(cs_ref[1], w, preferred_element_type=F32, precision=jax.lax.Precision.HIGHEST)
        o_ref[g] = jnp.concatenate([c, s], axis=1).astype(o_ref.dtype)


def _fourier_weights(w_four, seq_len):
    idx = (jnp.arange(GROUP_DIM, dtype=jnp.int32)[:, None] * jnp.arange(GROUP_DIM, dtype=jnp.int32)[None, :]) % GROUP_DIM
    ang = idx.astype(F32) * (2.0 * math.pi / GROUP_DIM)
    scale = 1.0 / math.sqrt(seq_len * GROUP_DIM)
    cs = jnp.stack([jnp.cos(ang), jnp.sin(ang)]) * scale
    return pl.pallas_call(
        _fourier_weight_kernel,
        out_shape=jax.ShapeDtypeStruct((N_GROUPS, GROUP_DIM, 2 * GROUP_DIM), BF16),
        name="fourier_weight_prep",
    )(cs, w_four)


def _in_proj_rows(x, rows, gn_ref, w_ref, cos_ref, sin_ref, gains_ref, hs_ref, wab_ref,
                  q_ref, ka_ref, kb_ref, va_ref, vb_ref, sga_ref, a_ref, b_ref, sgf_ref):
    sub_rows = x.shape[0]
    xb = (x * gn_ref[...]).astype(BF16)
    cos = cos_ref[rows, :]
    sin = sin_ref[rows, :]
    qcos, qsin = cos * gains_ref[0:1, :], sin * gains_ref[1:2, :]
    kcos, ksin = cos * gains_ref[2:3, :], sin * gains_ref[3:4, :]
    ms = jnp.mean(x * x, axis=-1, keepdims=True)
    r = jnp.broadcast_to(jax.lax.rsqrt(ms + EPS), (sub_rows, LANES))
    eps_z = jnp.broadcast_to(EPS * (ms + EPS), (sub_rows, LANES))
    wide = lambda t, width: jnp.concatenate([t] * (width // LANES), axis=1)

    lane = jax.lax.broadcasted_iota(jnp.int32, (1, LANES), 1)
    first_half_pair = (lane % HEAD_DIM) < (HEAD_DIM // 2)
    low_head = lane < HEAD_DIM
    hs = hs_ref[...]

    def head_mean_sq(z):
        return _dot_narrow((z * z).astype(BF16), hs)

    def silu_of_scaled(z, width):
        h = z * wide(0.5 * r, width)
        return h + h * jnp.tanh(h)

    zq = _dot(xb, w_ref[:, Q_OFF:Q_OFF + ATTN_WIDTH])
    half = ATTN_WIDTH // 2
    ssq = jnp.concatenate([head_mean_sq(zq[:, :half]), head_mean_sq(zq[:, half:])], axis=1)
    qr = (zq * wide(qcos, ATTN_WIDTH)
          + _rotate_half(zq, wide(first_half_pair, ATTN_WIDTH)) * wide(qsin, ATTN_WIDTH))
    q_ref[rows, :] = (qr * jax.lax.rsqrt(ssq + wide(eps_z, ATTN_WIDTH))).astype(q_ref.dtype)

    zkv = _dot_narrow(xb, w_ref[:, K_OFF:K_OFF + 2 * KV_WIDTH])
    zk = zkv[:, :KV_WIDTH]
    ssk = head_mean_sq(zkv)[:, :KV_WIDTH]
    kr = (zk * kcos + _rotate_half(zk, first_half_pair) * ksin) * jax.lax.rsqrt(ssk + eps_z)
    kr_sw = pltpu.roll(kr, HEAD_DIM, axis=1)
    ka_ref[rows, :] = jnp.where(low_head, kr, kr_sw).astype(ka_ref.dtype)
    kb_ref[rows, :] = jnp.where(low_head, kr_sw, kr).astype(kb_ref.dtype)
    zv = zkv[:, KV_WIDTH:] * r
    zv_sw = pltpu.roll(zv, HEAD_DIM, axis=1)
    va_ref[rows, :] = jnp.where(low_head, zv, zv_sw).astype(va_ref.dtype)
    vb_ref[rows, :] = jnp.where(low_head, zv_sw, zv).astype(vb_ref.dtype)

    zga = _dot(xb, w_ref[:, GA_OFF:GA_OFF + ATTN_WIDTH])
    sga_ref[rows, :] = silu_of_scaled(zga, ATTN_WIDTH).astype(sga_ref.dtype)
    zgf = _dot(xb, w_ref[:, GF_OFF:GF_OFF + FOURIER_WIDTH])
    sgf_ref[rows, :] = silu_of_scaled(zgf, FOURIER_WIDTH).astype(sgf_ref.dtype)

    zu = (_dot(xb, w_ref[:, U_OFF:U_OFF + FOURIER_WIDTH]) * wide(r, FOURIER_WIDTH)).astype(BF16)
    for g in range(N_GROUPS):
        ab = _dot_narrow(zu[:, g * GROUP_DIM:(g + 1) * GROUP_DIM], wab_ref[g])
        half_idx, col = divmod(g * GROUP_DIM, DFT_COLS)
        a_ref[half_idx, rows, col:col + GROUP_DIM] = ab[:, :GROUP_DIM].astype(a_ref.dtype)
        b_ref[half_idx, rows, col:col + GROUP_DIM] = ab[:, GROUP_DIM:].astype(b_ref.dtype)


N_IN_PROJ_INPUTS = 7


def _in_proj_kernel(x_ref, *refs):
    _in_proj_rows(x_ref[...], slice(None), *refs)


def _row_spec(tm, width):
    return pl.BlockSpec((tm, width), lambda i: (i, 0))


def _split_spec(tm):
    return pl.BlockSpec((FOURIER_WIDTH // DFT_COLS, tm, DFT_COLS), lambda i: (0, i, 0))


def _resident_spec(shape):
    return pl.BlockSpec(shape, lambda i: (0,) * len(shape), pipeline_mode=pl.Buffered(1))


def _in_proj_specs(tm, rows, seq_len):
    steps_per_seq = seq_len // tm
    tab_spec = pl.BlockSpec((tm, LANES), lambda i: (i % steps_per_seq, 0))
    in_specs = [_resident_spec((1, D_MODEL)), _resident_spec((D_MODEL, IN_WIDTH)), tab_spec, tab_spec,
                _resident_spec((4, LANES)), _resident_spec((2 * LANES, 2 * LANES)),
                _resident_spec((N_GROUPS, GROUP_DIM, 2 * GROUP_DIM))]
    row_widths = (ATTN_WIDTH, LANES, LANES, LANES, LANES, ATTN_WIDTH)
    out_specs = ([_row_spec(tm, w) for w in row_widths] + [_split_spec(tm)] * 2
                 + [_row_spec(tm, FOURIER_WIDTH)])
    out_shape = ([jax.ShapeDtypeStruct((rows, w), BF16) for w in row_widths]
                 + [jax.ShapeDtypeStruct((FOURIER_WIDTH // DFT_COLS, rows, DFT_COLS), BF16)] * 2
                 + [jax.ShapeDtypeStruct((rows, FOURIER_WIDTH), BF16)])
    return in_specs, out_specs, out_shape


def _in_proj(x2d, seq_len, in_w):
    rows = x2d.shape[0]
    tm = ROW_TILE
    in_specs, out_specs, out_shape = _in_proj_specs(tm, rows, seq_len)
    return pl.pallas_call(
        _in_proj_kernel,
        grid=(rows // tm,),
        in_specs=[_row_spec(tm, D_MODEL)] + in_specs,
        out_specs=out_specs,
        out_shape=out_shape,
        compiler_params=_compiler_params(1),
        name="in_proj",
    )(x2d, *in_w)


def _attn_kernel(sink_ref, bounded_ref, q_ref, ka_ref, kb_ref, va_ref, vb_ref, sg_ref, o_ref, *, n_blocks):
    refs = (sink_ref, q_ref, ka_ref, kb_ref, va_ref, vb_ref, sg_ref, o_ref)

    @pl.when(bounded_ref[0] == 1)
    def _():
        _attn_body(*refs, n_blocks=n_blocks, shift=False)

    @pl.when(bounded_ref[0] == 0)
    def _():
        _attn_body(*refs, n_blocks=n_blocks, shift=True)


def _attn_body(sink_ref, q_ref, ka_ref, kb_ref, va_ref, vb_ref, sg_ref, o_ref, *, n_blocks, shift):
    i = pl.program_id(1)
    blocks_per_step = ATTN_TILE // BLOCK
    row = jax.lax.broadcasted_iota(jnp.int32, (BLOCK, 2 * BLOCK), 0)
    col = jax.lax.broadcasted_iota(jnp.int32, (BLOCK, 2 * BLOCK), 1) % BLOCK
    lane = jax.lax.broadcasted_iota(jnp.int32, (1, LANES), 1)
    low = lane < HEAD_DIM
    zero = jnp.zeros((), BF16)
    lane_full = jax.lax.broadcasted_iota(jnp.int32, (BLOCK, LANES), 1)
    ones_low = jnp.where(lane_full < HEAD_DIM, 1.0, 0.0).astype(BF16)
    ones_high = jnp.where(lane_full < HEAD_DIM, 0.0, 1.0).astype(BF16)

    for jb in range(blocks_per_step):
        ib = i * blocks_per_step + jb
        rows = slice(jb * BLOCK, (jb + 1) * BLOCK)
        edge_prev = jnp.where(ib == 0, NEG, 0.0)
        edge_next = jnp.where(ib == n_blocks - 1, NEG, 0.0)
        bias_prev = jnp.where(col >= row, 0.0, NEG) + edge_prev
        bias_next = jnp.where(col <= row, 0.0, NEG) + edge_next
        starts = [pl.multiple_of(jnp.clip(ib + c, 0, n_blocks - 1) * BLOCK, BLOCK) for c in (-1, 0, 1)]
        for kvh, (k_ref, v_ref) in enumerate(((ka_ref, va_ref), (kb_ref, vb_ref))):
            kbd, vbd = [], []
            for st in starts:
                kblk = k_ref[0, pl.ds(st, BLOCK), :]
                vblk = v_ref[0, pl.ds(st, BLOCK), :]
                kbd.append(jnp.concatenate([jnp.where(low, kblk, zero), jnp.where(low, zero, kblk)], axis=0))
                vbd.append(jnp.concatenate([
                    jnp.concatenate([jnp.where(low, vblk, zero), ones_low], axis=1),
                    jnp.concatenate([jnp.where(low, zero, vblk), ones_high], axis=1)], axis=0))
            vbd = jnp.concatenate(vbd, axis=0)
            for pair in range(2):
                pidx = kvh * 2 + pair
                lanes = slice(pidx * LANES, (pidx + 1) * LANES)
                qp = q_ref[0, rows, lanes]
                scores = [_dot_nt(qp, kb) for kb in kbd]
                sink_e = sink_ref[2 * pidx]
                sink_o = sink_ref[2 * pidx + 1]
                if shift:
                    scores = [scores[0] + bias_prev, scores[1], scores[2] + bias_next]
                    smax = jnp.maximum(jnp.maximum(scores[0], scores[1]), scores[2])
                    m_e = jnp.maximum(jnp.max(smax[:, :BLOCK], axis=-1, keepdims=True), sink_e)
                    m_o = jnp.maximum(jnp.max(smax[:, BLOCK:], axis=-1, keepdims=True), sink_o)
                    m_both = jnp.concatenate([jnp.broadcast_to(m_e, (BLOCK, BLOCK)),
                                              jnp.broadcast_to(m_o, (BLOCK, BLOCK))], axis=1)
                    probs = [jnp.exp2(s - m_both).astype(BF16) for s in scores]
                    p_sink = jnp.where(low, jnp.exp2(sink_e - m_e), jnp.exp2(sink_o - m_o))
                else:
                    scores = [s.astype(BF16) for s in scores]
                    scores = [scores[0] + bias_prev.astype(BF16), scores[1], scores[2] + bias_next.astype(BF16)]
                    probs = [jnp.exp2(s) for s in scores]
                    p_sink = jnp.exp2(jnp.where(low, sink_e, sink_o))
                acc = _dot(jnp.concatenate(probs, axis=1), vbd)
                out = acc[:, :LANES] / (acc[:, LANES:] + p_sink) * sg_ref[0, rows, lanes].astype(F32)
                o_ref[0, rows, lanes] = out.astype(o_ref.dtype)


def _attn_out_kernel(sink_ref, bounded_ref, q_ref, ka_ref, kb_ref, va_ref, vb_ref, sg_ref,
                     x_ref, f_ref, sgf_ref, w_ref, y_ref, ma_ref, *, n_blocks):
    _attn_kernel(sink_ref, bounded_ref, q_ref, ka_ref, kb_ref, va_ref, vb_ref, sg_ref, ma_ref, n_blocks=n_blocks)
    f = jnp.concatenate([f_ref[h, 0] for h in range(FOURIER_WIDTH // DFT_COLS)], axis=1)
    y_ref[0] = (x_ref[0] + _dot(ma_ref[0], w_ref[:ATTN_WIDTH, :])
                + _dot(f * sgf_ref[0], w_ref[ATTN_WIDTH:, :]))


def _attention_out_proj(sink2, bounded, q, ka, kb, va, vb, sga, x, f, sgf, w_bf):
    bsz, seq_len, _ = q.shape
    tq = ATTN_TILE
    tile = lambda width: pl.BlockSpec((1, tq, width), lambda b, i, s, fl: (b, i, 0))
    kv_spec = pl.BlockSpec((1, seq_len, LANES), lambda b, i, s, fl: (b, 0, 0))
    f_spec = pl.BlockSpec((FOURIER_WIDTH // DFT_COLS, 1, tq, DFT_COLS), lambda b, i, s, fl: (0, b, i, 0))
    w_spec = pl.BlockSpec((D_MODEL, D_MODEL), lambda b, i, s, fl: (0, 0), pipeline_mode=pl.Buffered(1))
    grid_spec = pltpu.PrefetchScalarGridSpec(
        num_scalar_prefetch=2,
        grid=(bsz, seq_len // tq),
        in_specs=[tile(ATTN_WIDTH), kv_spec, kv_spec, kv_spec, kv_spec, tile(ATTN_WIDTH),
                  tile(D_MODEL), f_spec, tile(FOURIER_WIDTH), w_spec],
        out_specs=tile(D_MODEL),
        scratch_shapes=[pltpu.VMEM((1, tq, ATTN_WIDTH), BF16)],
    )
    return pl.pallas_call(
        functools.partial(_attn_out_kernel, n_blocks=seq_len // BLOCK),
        grid_spec=grid_spec,
        out_shape=jax.ShapeDtypeStruct((bsz, seq_len, D_MODEL), F32),
        compiler_params=_compiler_params(2),
        name="attention_out_proj",
    )(sink2, bounded, q, ka, kb, va, vb, sga, x, f, sgf, w_bf)


def _attention(sink2, bounded, q, ka, kb, va, vb, sga):
    bsz, seq_len, _ = q.shape
    tq = ATTN_TILE
    q_spec = pl.BlockSpec((1, tq, ATTN_WIDTH), lambda b, i, s, f: (b, i, 0))
    kv_spec = pl.BlockSpec((1, seq_len, LANES), lambda b, i, s, f: (b, 0, 0))
    grid_spec = pltpu.PrefetchScalarGridSpec(
        num_scalar_prefetch=2,
        grid=(bsz, seq_len // tq),
        in_specs=[q_spec, kv_spec, kv_spec, kv_spec, kv_spec, q_spec],
        out_specs=q_spec,
    )
    return pl.pallas_call(
        functools.partial(_attn_kernel, n_blocks=seq_len // BLOCK),
        grid_spec=grid_spec,
        out_shape=jax.ShapeDtypeStruct((bsz, seq_len, ATTN_WIDTH), BF16),
        compiler_params=_compiler_params(2),
        name="band_attention",
    )(sink2, bounded, q, ka, kb, va, vb, sga)


def _seq_dft_kernel(a_ref, b_ref, m_ref, o_ref, zz_ref, nat_ref, *, chunk):
    half = RADIX // 2
    root_half = math.sqrt(0.5)
    cadd = lambda u, v: (u[0] + v[0], u[1] + v[1])
    csub = lambda u, v: (u[0] - v[0], u[1] - v[1])
    add_i = lambda u, v: (u[0] - v[1], u[1] + v[0])
    sub_i = lambda u, v: (u[0] + v[1], u[1] - v[0])

    def four_point(w):
        s02, d02, s13, d13 = cadd(w[0], w[2]), csub(w[0], w[2]), cadd(w[1], w[3]), csub(w[1], w[3])
        return [cadd(s02, s13), add_i(d02, d13), csub(s02, s13), sub_i(d02, d13)]

    def butterflies(parity):
        for rb in range(chunk // DFT_BLOCK_ROWS):
            for cb in range(DFT_COLS // LANES):
                rows = lambda q: slice(q * chunk + rb * DFT_BLOCK_ROWS, q * chunk + (rb + 1) * DFT_BLOCK_ROWS)
                cols = slice(cb * LANES, (cb + 1) * LANES)
                z = [(a_ref[0, 0, rows(q), cols], b_ref[0, 0, rows(q), cols]) for q in range(RADIX)]
                if parity == 0:
                    w = [cadd(z[q], z[q + half]) for q in range(half)]
                else:
                    d = [csub(z[q], z[q + half]) for q in range(half)]
                    w = [d[0],
                         ((d[1][0] - d[1][1]) * root_half, (d[1][0] + d[1][1]) * root_half),
                         (-d[2][1], d[2][0]),
                         ((-d[3][0] - d[3][1]) * root_half, (d[3][0] - d[3][1]) * root_half)]
                for t, (re, im) in enumerate(four_point(w)):
                    zz_ref[2 * t + parity, rows(0), cols] = re
                    zz_ref[2 * t + parity, rows(1), cols] = im

    def project(r):
        f = _dot(m_ref[r], zz_ref[r])
        for cb in range(DFT_COLS // LANES):
            nat_ref[cb, pl.ds(r, chunk, stride=RADIX), :] = f[:, cb * LANES:(cb + 1) * LANES]

    butterflies(0)
    for r in range(0, RADIX, 2):
        project(r)
    butterflies(1)
    for r in range(1, RADIX, 2):
        project(r)
    for cb in range(DFT_COLS // LANES):
        o_ref[0, 0, :, cb * LANES:(cb + 1) * LANES] = nat_ref[cb].astype(o_ref.dtype)


def _dft_matrix(seq_len):
    chunk = seq_len // RADIX
    split = 32
    period = seq_len // (RADIX * split)
    m = jnp.arange(chunk, dtype=jnp.int32)[None, :]
    ang_hi = ((jnp.arange(chunk // split, dtype=jnp.int32)[:, None] * m) % period).astype(F32) * (2.0 * math.pi / period)
    ang_lo = ((jnp.arange(RADIX * split, dtype=jnp.int32)[:, None] * m) % seq_len).astype(F32) * (2.0 * math.pi / seq_len)
    c_hi, s_hi = jnp.cos(ang_hi)[None, :, None, :], jnp.sin(ang_hi)[None, :, None, :]
    lo = lambda t: t.reshape(split, RADIX, chunk).transpose(1, 0, 2)[:, None, :, :]
    c_lo, s_lo = lo(jnp.cos(ang_lo)), lo(jnp.sin(ang_lo))
    cos = (c_hi * c_lo - s_hi * s_lo).reshape(RADIX, chunk, chunk)
    sin = (s_hi * c_lo + c_hi * s_lo).reshape(RADIX, chunk, chunk)
    return jnp.concatenate([cos, -sin], axis=2).astype(BF16)


def _seq_dft(a, b, dft_m):
    n_halves, bsz, seq_len, _ = a.shape
    chunk = seq_len // RADIX
    col_spec = pl.BlockSpec((1, 1, seq_len, DFT_COLS), lambda bi, h: (h, bi, 0, 0))
    m_spec = pl.BlockSpec((RADIX, chunk, 2 * chunk), lambda bi, h: (0, 0, 0), pipeline_mode=pl.Buffered(1))
    slabs = pltpu.VMEM((DFT_COLS // LANES, seq_len, LANES), F32)
    return pl.pallas_call(
        functools.partial(_seq_dft_kernel, chunk=chunk),
        grid=(bsz, n_halves),
        in_specs=[col_spec, col_spec, m_spec],
        out_specs=col_spec,
        out_shape=jax.ShapeDtypeStruct(a.shape, BF16),
        scratch_shapes=[pltpu.VMEM((RADIX, 2 * chunk, DFT_COLS), BF16), slabs],
        compiler_params=_compiler_params(2),
        name="seq_dft",
    )(a, b, dft_m)


def _out_proj_rows(rows, x_ref, ma_ref, f_ref, sgf_ref, w_ref, o_ref):
    f = jnp.concatenate([f_ref[h, rows, :] for h in range(FOURIER_WIDTH // DFT_COLS)], axis=1)
    mf = f * sgf_ref[rows, :]
    o_ref[rows, :] = (x_ref[rows, :] + _dot(ma_ref[rows, :], w_ref[:ATTN_WIDTH, :])
                      + _dot(mf, w_ref[ATTN_WIDTH:, :]))


def _out_proj_kernel(x_ref, ma_ref, f_ref, sgf_ref, w_ref, o_ref):
    _out_proj_rows(slice(None), x_ref, ma_ref, f_ref, sgf_ref, w_ref, o_ref)


N_OUT_PROJ_INPUTS = 5


def _out_in_proj_kernel(*refs):
    out_in = refs[:N_OUT_PROJ_INPUTS]
    in_refs = refs[N_OUT_PROJ_INPUTS:N_OUT_PROJ_INPUTS + N_IN_PROJ_INPUTS]
    y_ref = refs[N_OUT_PROJ_INPUTS + N_IN_PROJ_INPUTS]
    out_refs = refs[N_OUT_PROJ_INPUTS + N_IN_PROJ_INPUTS + 1:]
    for sub in range(y_ref.shape[0] // FUSED_SUB_ROWS):
        rows = slice(sub * FUSED_SUB_ROWS, (sub + 1) * FUSED_SUB_ROWS)
        _out_proj_rows(rows, *out_in, y_ref)
        _in_proj_rows(y_ref[rows, :], rows, *in_refs, *out_refs)


def _out_proj_specs(tm):
    return [_row_spec(tm, D_MODEL), _row_spec(tm, ATTN_WIDTH), _split_spec(tm),
            _row_spec(tm, FOURIER_WIDTH), _resident_spec((D_MODEL, D_MODEL))]


def _out_proj(x2d, ma, f, sgf, w_bf):
    rows = x2d.shape[0]
    tm = ROW_TILE
    return pl.pallas_call(
        _out_proj_kernel,
        grid=(rows // tm,),
        in_specs=_out_proj_specs(tm),
        out_specs=_row_spec(tm, D_MODEL),
        out_shape=jax.ShapeDtypeStruct((rows, D_MODEL), F32),
        compiler_params=_compiler_params(1),
        name="out_proj",
    )(x2d, ma, f, sgf, w_bf)


def _out_in_proj(x2d, ma, f, sgf, w_bf, seq_len, in_w):
    rows = x2d.shape[0]
    tm = FUSED_ROW_TILE
    i_in_specs, i_out_specs, i_out_shape = _in_proj_specs(tm, rows, seq_len)
    outs = pl.pallas_call(
        _out_in_proj_kernel,
        grid=(rows // tm,),
        in_specs=_out_proj_specs(tm) + i_in_specs,
        out_specs=[_row_spec(tm, D_MODEL)] + i_out_specs,
        out_shape=[jax.ShapeDtypeStruct((rows, D_MODEL), F32)] + i_out_shape,
        compiler_params=_compiler_params(1, FUSED_VMEM_LIMIT),
        name="out_in_proj",
    )(x2d, ma, f, sgf, w_bf, *in_w)
    return outs[0], outs[1:]


def _rope_tables(seq_len):
    half = HEAD_DIM // 2
    inv_freq = 1.0 / (ROPE_THETA ** (jnp.arange(half, dtype=F32) / half))
    ang = jnp.arange(seq_len, dtype=F32)[:, None] * inv_freq[None, :]
    cos = jnp.cos(ang)
    sin = jnp.sin(ang)
    cos_t = jnp.concatenate([cos, cos, cos, cos], axis=1)
    sin_t = jnp.concatenate([-sin, sin, -sin, sin], axis=1)
    return cos_t, sin_t


def _rope_gains(q_gain, k_gain):
    pair = lambda g: jnp.tile(g, LANES // HEAD_DIM)
    rot = lambda g: jnp.roll(g, HEAD_DIM // 2)
    gq = q_gain.astype(F32) * (HEAD_DIM ** -0.5 * LOG2E)
    gk = k_gain.astype(F32)
    return jnp.stack([pair(gq), pair(rot(gq)), pair(gk), pair(rot(gk))])


def _head_mean_matrix(width):
    head = jnp.arange(width, dtype=jnp.int32) // HEAD_DIM
    return jnp.where(head[:, None] == head[None, :], 1.0 / HEAD_DIM, 0.0).astype(BF16)


def _trunk(x, layers, dft_m):
    bsz, seq_len, _ = x.shape
    x2d = x.reshape(bsz * seq_len, D_MODEL)
    r3 = lambda t: t.reshape(bsz, seq_len, t.shape[-1])
    split4 = lambda t: t.reshape(t.shape[0], bsz, seq_len, DFT_COLS)
    proj = _in_proj(x2d, seq_len, layers[0]["in_w"])
    for l, layer in enumerate(layers):
        q, ka, kb, va, vb, sga, a, b, sgf = proj
        f = _seq_dft(split4(a), split4(b), dft_m)
        attn_args = (layer["sink2"], layer["bounded"], r3(q), r3(ka), r3(kb), r3(va), r3(vb), r3(sga))
        if l + 1 < len(layers):
            ma = _attention(*attn_args).reshape(bsz * seq_len, ATTN_WIDTH)
            x2d, proj = _out_in_proj(x2d, ma, f.reshape(f.shape[0], bsz * seq_len, DFT_COLS), sgf,
                                     layer["w_out"], seq_len, layers[l + 1]["in_w"])
        else:
            return _attention_out_proj(*attn_args, r3(x2d), f, r3(sgf), layer["w_out"])


def kernel(x_prompt, x_sample, norm_gain, w_in, q_norm_gain, k_norm_gain, sink_logit, w_fourier, w_out):
    depth = norm_gain.shape[0]
    seq_p, seq_s = x_prompt.shape[1], x_sample.shape[1]
    assert seq_p == seq_s
    cos_t, sin_t = _rope_tables(seq_p)
    hs = _head_mean_matrix(2 * LANES)
    layers = []
    for l in range(depth):
        sink2 = sink_logit[l].astype(F32) * LOG2E
        logit_bound = (HEAD_DIM ** 0.5 * LOG2E) * jnp.max(jnp.abs(q_norm_gain[l])) * jnp.max(jnp.abs(k_norm_gain[l]))
        bounded = jnp.maximum(logit_bound, jnp.max(jnp.abs(sink2))) <= MAX_UNSHIFTED_LOGIT2
        in_w = (norm_gain[l].astype(F32)[None, :], w_in[l].astype(BF16), cos_t, sin_t,
                _rope_gains(q_norm_gain[l], k_norm_gain[l]), hs, _fourier_weights(w_fourier[l], seq_p))
        layers.append(dict(in_w=in_w, sink2=sink2, bounded=bounded.astype(jnp.int32)[None],
                           w_out=w_out[l].astype(BF16)))
    dft_m = _dft_matrix(seq_p)
    return (_trunk(x_prompt, layers, dft_m), _trunk(x_sample, layers, dft_m))
```

```python
import functools
import math

import jax
import jax.numpy as jnp
from jax.experimental import pallas as pl
from jax.experimental.pallas import tpu as pltpu

D_MODEL = 1024
HEAD_DIM = 64
N_Q_HEADS = 8
N_KV_HEADS = 2
ATTN_WIDTH = N_Q_HEADS * HEAD_DIM
KV_WIDTH = N_KV_HEADS * HEAD_DIM
FOURIER_WIDTH = D_MODEL - ATTN_WIDTH
N_GROUPS = 4
GROUP_DIM = FOURIER_WIDTH // N_GROUPS
IN_WIDTH = 2 * ATTN_WIDTH + 2 * KV_WIDTH + 2 * FOURIER_WIDTH
Q_OFF = 0
K_OFF = ATTN_WIDTH
V_OFF = K_OFF + KV_WIDTH
GA_OFF = V_OFF + KV_WIDTH
U_OFF = GA_OFF + ATTN_WIDTH
GF_OFF = U_OFF + FOURIER_WIDTH
BLOCK = 128
ROPE_THETA = 10000.0
EPS = 1e-6
NEG = -1e30
LANES = 128
RADIX = 8
DFT_COLS = 256
DFT_BLOCK_ROWS = 128
LOG2E = math.log2(math.e)
MAX_UNSHIFTED_LOGIT2 = 100.0

ROW_TILE = 1024
FUSED_ROW_TILE = 1024
FUSED_SUB_ROWS = 512
ATTN_TILE = 1024
VMEM_LIMIT = 48 * 1024 * 1024
FUSED_VMEM_LIMIT = 54 * 1024 * 1024

BF16 = jnp.bfloat16
F32 = jnp.float32


def _compiler_params(grid_rank, vmem_limit=VMEM_LIMIT):
    return pltpu.CompilerParams(dimension_semantics=("arbitrary",) * grid_rank, vmem_limit_bytes=vmem_limit)


def _dot(a, b):
    return jnp.dot(a, b, preferred_element_type=F32)


def _dot_narrow(a, b):
    half = a.shape[0] // 2
    return jnp.concatenate([_dot(a[:half], b), _dot(a[half:], b)], axis=0)


def _dot_nt(a, b):
    return jax.lax.dot_general(a, b, (((1,), (1,)), ((), ())), preferred_element_type=F32)


def _silu(x):
    return x / (1.0 + jnp.exp(-x))


def _pack_rows(x):
    return pltpu.bitcast(x.astype(BF16), jnp.uint32)


def _unpack_rows(bits):
    return pltpu.bitcast(bits, BF16)


def _half_rows(rows):
    return rows if rows == slice(None) else slice(rows.start // 2, rows.stop // 2)


def _rotate_half(t, first_half):
    width = t.shape[-1]
    fwd = pltpu.roll(t, HEAD_DIM // 2, axis=1)
    bwd = pltpu.roll(t, width - HEAD_DIM // 2, axis=1)
    return jnp.where(first_half, bwd, fwd)


def _fourier_weight_kernel(cs_ref, w_ref, o_ref):
    for g in range(N_GROUPS):
        w = w_ref[g]
        c = jnp.dot(cs_ref[0], w, preferred_element_type=F32, precision=jax.lax.Precision.HIGHEST)
        s = jnp.dot(cs_ref[1], w, preferred_element_type=F32, precision=jax.lax.Precision.HIGHEST)
        o_ref[g] = jnp.concatenate([c, s], axis=1).astype(o_ref.dtype)


def _fourier_weights(w_four, seq_len):
    idx = (jnp.arange(GROUP_DIM, dtype=jnp.int32)[:, None] * jnp.arange(GROUP_DIM, dtype=jnp.int32)[None, :]) % GROUP_DIM
    ang = idx.astype(F32) * (2.0 * math.pi / GROUP_DIM)
    scale = 1.0 / math.sqrt(seq_len * GROUP_DIM)
    cs = jnp.stack([jnp.cos(ang), jnp.sin(ang)]) * scale
    return pl.pallas_call(
        _fourier_weight_kernel,
        out_shape=jax.ShapeDtypeStruct((N_GROUPS, GROUP_DIM, 2 * GROUP_DIM), BF16),
        name="fourier_weight_prep",
    )(cs, w_four)


def _in_proj_rows(x, rows, gn_ref, w_ref, cos_ref, sin_ref, gains_ref, hs_ref, wab_ref,
                  q_ref, ka_ref, kb_ref, va_ref, vb_ref, sga_ref, a_ref, b_ref, sgf_ref):
    sub_rows = x.shape[0]
    xb = (x * gn_ref[...]).astype(BF16)
    cos = cos_ref[rows, :]
    sin = sin_ref[rows, :]
    qcos, qsin = cos * gains_ref[0:1, :], sin * gains_ref[1:2, :]
    kcos, ksin = cos * gains_ref[2:3, :], sin * gains_ref[3:4, :]
    ms = jnp.mean(x * x, axis=-1, keepdims=True)
    r = jnp.broadcast_to(jax.lax.rsqrt(ms + EPS), (sub_rows, LANES))
    eps_z = jnp.broadcast_to(EPS * (ms + EPS), (sub_rows, LANES))
    wide = lambda t, width: jnp.concatenate([t] * (width // LANES), axis=1)

    lane = jax.lax.broadcasted_iota(jnp.int32, (1, LANES), 1)
    first_half_pair = (lane % HEAD_DIM) < (HEAD_DIM // 2)
    low_head = lane < HEAD_DIM
    hs = hs_ref[...]

    def head_mean_sq(z):
        return _dot_narrow((z * z).astype(BF16), hs)

    def silu_of_scaled(z, width):
        h = z * wide(0.5 * r, width)
        return h + h * jnp.tanh(h)

    zq = _dot(xb, w_ref[:, Q_OFF:Q_OFF + ATTN_WIDTH])
    half = ATTN_WIDTH // 2
    ssq = jnp.concatenate([head_mean_sq(zq[:, :half]), head_mean_sq(zq[:, half:])], axis=1)
    qr = (zq * wide(qcos, ATTN_WIDTH)
          + _rotate_half(zq, wide(first_half_pair, ATTN_WIDTH)) * wide(qsin, ATTN_WIDTH))
    q_ref[rows, :] = (qr * jax.lax.rsqrt(ssq + wide(eps_z, ATTN_WIDTH))).astype(q_ref.dtype)

    zkv = _dot_narrow(xb, w_ref[:, K_OFF:K_OFF + 2 * KV_WIDTH])
    zk = zkv[:, :KV_WIDTH]
    ssk = head_mean_sq(zkv)[:, :KV_WIDTH]
    kr = (zk * kcos + _rotate_half(zk, first_half_pair) * ksin) * jax.lax.rsqrt(ssk + eps_z)
    kr_sw = pltpu.roll(kr, HEAD_DIM, axis=1)
    ka_ref[rows, :] = jnp.where(low_head, kr, kr_sw).astype(ka_ref.dtype)
    kb_ref[rows, :] = jnp.where(low_head, kr_sw, kr).astype(kb_ref.dtype)
    zv = zkv[:, KV_WIDTH:] * r
    zv_sw = pltpu.roll(zv, HEAD_DIM, axis=1)
    va_ref[rows, :] = jnp.where(low_head, zv, zv_sw).astype(va_ref.dtype)
    vb_ref[rows, :] = jnp.where(low_head, zv_sw, zv).astype(vb_ref.dtype)

    zga = _dot(xb, w_ref[:, GA_OFF:GA_OFF + ATTN_WIDTH])
    sga_ref[rows, :] = silu_of_scaled(zga, ATTN_WIDTH).astype(sga_ref.dtype)
    zgf = _dot(xb, w_ref[:, GF_OFF:GF_OFF + FOURIER_WIDTH])
    sgf_ref[rows, :] = silu_of_scaled(zgf, FOURIER_WIDTH).astype(sgf_ref.dtype)

    zu = (_dot(xb, w_ref[:, U_OFF:U_OFF + FOURIER_WIDTH]) * wide(r, FOURIER_WIDTH)).astype(BF16)
    for g in range(N_GROUPS):
        ab = _dot_narrow(zu[:, g * GROUP_DIM:(g + 1) * GROUP_DIM], wab_ref[g])
        half_idx, col = divmod(g * GROUP_DIM, DFT_COLS)
        a_ref[half_idx, _half_rows(rows), col:col + GROUP_DIM] = _pack_rows(ab[:, :GROUP_DIM])
        b_ref[half_idx, _half_rows(rows), col:col + GROUP_DIM] = _pack_rows(ab[:, GROUP_DIM:])


N_IN_PROJ_INPUTS = 7


def _in_proj_kernel(x_ref, *refs):
    _in_proj_rows(x_ref[...], slice(None), *refs)


def _row_spec(tm, width):
    return pl.BlockSpec((tm, width), lambda i: (i, 0))


def _split_spec(tm):
    return pl.BlockSpec((FOURIER_WIDTH // DFT_COLS, tm // 2, DFT_COLS), lambda i: (0, i, 0))


def _resident_spec(shape):
    return pl.BlockSpec(shape, lambda i: (0,) * len(shape), pipeline_mode=pl.Buffered(1))


def _in_proj_specs(tm, rows, seq_len):
    steps_per_seq = seq_len // tm
    tab_spec = pl.BlockSpec((tm, LANES), lambda i: (i % steps_per_seq, 0))
    in_specs = [_resident_spec((1, D_MODEL)), _resident_spec((D_MODEL, IN_WIDTH)), tab_spec, tab_spec,
                _resident_spec((4, LANES)), _resident_spec((2 * LANES, 2 * LANES)),
                _resident_spec((N_GROUPS, GROUP_DIM, 2 * GROUP_DIM))]
    row_widths = (ATTN_WIDTH, LANES, LANES, LANES, LANES, ATTN_WIDTH)
    out_specs = ([_row_spec(tm, w) for w in row_widths] + [_split_spec(tm)] * 2
                 + [_row_spec(tm, FOURIER_WIDTH)])
    out_shape = ([jax.ShapeDtypeStruct((rows, w), BF16) for w in row_widths]
                 + [jax.ShapeDtypeStruct((FOURIER_WIDTH // DFT_COLS, rows // 2, DFT_COLS), jnp.uint32)] * 2
                 + [jax.ShapeDtypeStruct((rows, FOURIER_WIDTH), BF16)])
    return in_specs, out_specs, out_shape


def _in_proj(x2d, seq_len, in_w):
    rows = x2d.shape[0]
    tm = ROW_TILE
    in_specs, out_specs, out_shape = _in_proj_specs(tm, rows, seq_len)
    return pl.pallas_call(
        _in_proj_kernel,
        grid=(rows // tm,),
        in_specs=[_row_spec(tm, D_MODEL)] + in_specs,
        out_specs=out_specs,
        out_shape=out_shape,
        compiler_params=_compiler_params(1),
        name="in_proj",
    )(x2d, *in_w)


def _attn_kernel(sink_ref, bounded_ref, q_ref, ka_ref, kb_ref, va_ref, vb_ref, sg_ref, o_ref, *, n_blocks):
    refs = (sink_ref, q_ref, ka_ref, kb_ref, va_ref, vb_ref, sg_ref, o_ref)

    @pl.when(bounded_ref[0] == 1)
    def _():
        _attn_body(*refs, n_blocks=n_blocks, shift=False)

    @pl.when(bounded_ref[0] == 0)
    def _():
        _attn_body(*refs, n_blocks=n_blocks, shift=True)


def _attn_body(sink_ref, q_ref, ka_ref, kb_ref, va_ref, vb_ref, sg_ref, o_ref, *, n_blocks, shift):
    i = pl.program_id(1)
    blocks_per_step = ATTN_TILE // BLOCK
    row = jax.lax.broadcasted_iota(jnp.int32, (BLOCK, 2 * BLOCK), 0)
    col = jax.lax.broadcasted_iota(jnp.int32, (BLOCK, 2 * BLOCK), 1) % BLOCK
    lane = jax.lax.broadcasted_iota(jnp.int32, (1, LANES), 1)
    low = lane < HEAD_DIM
    zero = jnp.zeros((), BF16)
    lane_full = jax.lax.broadcasted_iota(jnp.int32, (BLOCK, LANES), 1)
    ones_low = jnp.where(lane_full < HEAD_DIM, 1.0, 0.0).astype(BF16)
    ones_high = jnp.where(lane_full < HEAD_DIM, 0.0, 1.0).astype(BF16)

    for jb in range(blocks_per_step):
        ib = i * blocks_per_step + jb
        rows = slice(jb * BLOCK, (jb + 1) * BLOCK)
        edge_prev = jnp.where(ib == 0, NEG, 0.0)
        edge_next = jnp.where(ib == n_blocks - 1, NEG, 0.0)
        bias_prev = jnp.where(col >= row, 0.0, NEG) + edge_prev
        bias_next = jnp.where(col <= row, 0.0, NEG) + edge_next
        starts = [pl.multiple_of(jnp.clip(ib + c, 0, n_blocks - 1) * BLOCK, BLOCK) for c in (-1, 0, 1)]
        for kvh, (k_ref, v_ref) in enumerate(((ka_ref, va_ref), (kb_ref, vb_ref))):
            kbd, vbd = [], []
            for st in starts:
                kblk = k_ref[0, pl.ds(st, BLOCK), :]
                vblk = v_ref[0, pl.ds(st, BLOCK), :]
                kbd.append(jnp.concatenate([jnp.where(low, kblk, zero), jnp.where(low, zero, kblk)], axis=0))
                vbd.append(jnp.concatenate([
                    jnp.concatenate([jnp.where(low, vblk, zero), ones_low], axis=1),
                    jnp.concatenate([jnp.where(low, zero, vblk), ones_high], axis=1)], axis=0))
            vbd = jnp.concatenate(vbd, axis=0)
            for pair in range(2):
                pidx = kvh * 2 + pair
                lanes = slice(pidx * LANES, (pidx + 1) * LANES)
                qp = q_ref[0, rows, lanes]
                scores = [_dot_nt(qp, kb) for kb in kbd]
                sink_e = sink_ref[2 * pidx]
                sink_o = sink_ref[2 * pidx + 1]
                if shift:
                    scores = [scores[0] + bias_prev, scores[1], scores[2] + bias_next]
                    smax = jnp.maximum(jnp.maximum(scores[0], scores[1]), scores[2])
                    m_e = jnp.maximum(jnp.max(smax[:, :BLOCK], axis=-1, keepdims=True), sink_e)
                    m_o = jnp.maximum(jnp.max(smax[:, BLOCK:], axis=-1, keepdims=True), sink_o)
                    m_both = jnp.concatenate([jnp.broadcast_to(m_e, (BLOCK, BLOCK)),
                                              jnp.broadcast_to(m_o, (BLOCK, BLOCK))], axis=1)
                    probs = [jnp.exp2(s - m_both).astype(BF16) for s in scores]
                    p_sink = jnp.where(low, jnp.exp2(sink_e - m_e), jnp.exp2(sink_o - m_o))
                else:
                    scores = [s.astype(BF16) for s in scores]
                    scores = [scores[0] + bias_prev.astype(BF16), scores[1], scores[2] + bias_next.astype(BF16)]
                    probs = [jnp.exp2(s) for s in scores]
                    p_sink = jnp.exp2(jnp.where(low, sink_e, sink_o))
                acc = _dot(jnp.concatenate(probs, axis=1), vbd)
                out = acc[:, :LANES] / (acc[:, LANES:] + p_sink) * sg_ref[0, rows, lanes].astype(F32)
                o_ref[0, rows, lanes] = out.astype(o_ref.dtype)


def _attn_out_kernel(sink_ref, bounded_ref, q_ref, ka_ref, kb_ref, va_ref, vb_ref, sg_ref,
                     x_ref, f_ref, sgf_ref, w_ref, y_ref, ma_ref, *, n_blocks):
    _attn_kernel(sink_ref, bounded_ref, q_ref, ka_ref, kb_ref, va_ref, vb_ref, sg_ref, ma_ref, n_blocks=n_blocks)
    f = jnp.concatenate([_unpack_rows(f_ref[h, 0]) for h in range(FOURIER_WIDTH // DFT_COLS)], axis=1)
    y_ref[0] =(x_ref[0] + _dot(ma_ref[0], w_ref[:ATTN_WIDTH, :])
                + _dot(f * sgf_ref[0], w_ref[ATTN_WIDTH:, :]))


def _attention_out_proj(sink2, bounded, q, ka, kb, va, vb, sga, x, f, sgf, w_bf):
    bsz, seq_len, _ = q.shape
    tq = ATTN_TILE
    tile = lambda width: pl.BlockSpec((1, tq, width), lambda b, i, s, fl: (b, i, 0))
    kv_spec = pl.BlockSpec((1, seq_len, LANES), lambda b, i, s, fl: (b, 0, 0))
    f_spec = pl.BlockSpec((FOURIER_WIDTH // DFT_COLS, 1, tq // 2, DFT_COLS), lambda b, i, s, fl: (0, b, i, 0))
    w_spec = pl.BlockSpec((D_MODEL, D_MODEL), lambda b, i, s, fl: (0, 0), pipeline_mode=pl.Buffered(1))
    grid_spec = pltpu.PrefetchScalarGridSpec(
        num_scalar_prefetch=2,
        grid=(bsz, seq_len // tq),
        in_specs=[tile(ATTN_WIDTH), kv_spec, kv_spec, kv_spec, kv_spec, tile(ATTN_WIDTH),
                  tile(D_MODEL), f_spec, tile(FOURIER_WIDTH), w_spec],
        out_specs=tile(D_MODEL),
        scratch_shapes=[pltpu.VMEM((1, tq, ATTN_WIDTH), BF16)],
    )
    return pl.pallas_call(
        functools.partial(_attn_out_kernel, n_blocks=seq_len // BLOCK),
        grid_spec=grid_spec,
        out_shape=jax.ShapeDtypeStruct((bsz, seq_len, D_MODEL), F32),
        compiler_params=_compiler_params(2),
        name="attention_out_proj",
    )(sink2, bounded, q, ka, kb, va, vb, sga, x, f, sgf, w_bf)


def _attention(sink2, bounded, q, ka, kb, va, vb, sga):
    bsz, seq_len, _ = q.shape
    tq = ATTN_TILE
    q_spec = pl.BlockSpec((1, tq, ATTN_WIDTH), lambda b, i, s, f: (b, i, 0))
    kv_spec = pl.BlockSpec((1, seq_len, LANES), lambda b, i, s, f: (b, 0, 0))
    grid_spec = pltpu.PrefetchScalarGridSpec(
        num_scalar_prefetch=2,
        grid=(bsz, seq_len // tq),
        in_specs=[q_spec, kv_spec, kv_spec, kv_spec, kv_spec, q_spec],
        out_specs=q_spec,
    )
    return pl.pallas_call(
        functools.partial(_attn_kernel, n_blocks=seq_len // BLOCK),
        grid_spec=grid_spec,
        out_shape=jax.ShapeDtypeStruct((bsz, seq_len, ATTN_WIDTH), BF16),
        compiler_params=_compiler_params(2),
        name="band_attention",
    )(sink2, bounded, q, ka, kb, va, vb, sga)


def _seq_dft_kernel(a_ref, b_ref, m_ref, o_ref, zz_ref, nat_ref, *, chunk):
    half = RADIX // 2
    root_half = math.sqrt(0.5)
    cadd = lambda u, v: (u[0] + v[0], u[1] + v[1])
    csub = lambda u, v: (u[0] - v[0], u[1] - v[1])
    add_i = lambda u, v: (u[0] - v[1], u[1] + v[0])
    sub_i = lambda u, v: (u[0] + v[1], u[1] - v[0])

    def four_point(w):
        s02, d02, s13, d13 = cadd(w[0], w[2]), csub(w[0], w[2]), cadd(w[1], w[3]), csub(w[1], w[3])
        return [cadd(s02, s13), add_i(d02, d13), csub(s02, s13), sub_i(d02, d13)]

    def butterflies(parity):
        for rb in range(chunk // DFT_BLOCK_ROWS):
            for cb in range(DFT_COLS // LANES):
                rows = lambda q: slice(q * chunk + rb * DFT_BLOCK_ROWS, q * chunk + (rb + 1) * DFT_BLOCK_ROWS)
                cols = slice(cb * LANES, (cb + 1) * LANES)
                z = [(_unpack_rows(a_ref[0, 0, _half_rows(rows(q)), cols]),
                      _unpack_rows(b_ref[0, 0, _half_rows(rows(q)), cols])) for q in range(RADIX)]
                if parity == 0:
                    w = [cadd(z[q], z[q + half]) for q in range(half)]
                else:
                    d = [csub(z[q], z[q + half]) for q in range(half)]
                    w = [d[0],
                         ((d[1][0] - d[1][1]) * root_half, (d[1][0] + d[1][1]) * root_half),
                         (-d[2][1], d[2][0]),
                         ((-d[3][0] - d[3][1]) * root_half, (d[3][0] - d[3][1]) * root_half)]
                for t, (re, im) in enumerate(four_point(w)):
                    zz_ref[2 * t + parity, rows(0), cols] = re
                    zz_ref[2 * t + parity, rows(1), cols] = im

    def project(r):
        f = _dot(m_ref[r], zz_ref[r])
        for cb in range(DFT_COLS // LANES):
            nat_ref[cb, pl.ds(r, chunk, stride=RADIX), :] = f[:, cb * LANES:(cb + 1) * LANES]

    butterflies(0)
    for r in range(0, RADIX, 2):
        project(r)
    butterflies(1)
    for r in range(1, RADIX, 2):
        project(r)
    for cb in range(DFT_COLS // LANES):
        o_ref[0, 0, :, cb * LANES:(cb + 1) * LANES] = _pack_rows(nat_ref[cb])


def _dft_matrix(seq_len):
    chunk = seq_len // RADIX
    split = 32
    period = seq_len // (RADIX * split)
    m = jnp.arange(chunk, dtype=jnp.int32)[None, :]
    ang_hi = ((jnp.arange(chunk // split, dtype=jnp.int32)[:, None] * m) % period).astype(F32) * (2.0 * math.pi / period)
    ang_lo = ((jnp.arange(RADIX * split, dtype=jnp.int32)[:, None] * m) % seq_len).astype(F32) * (2.0 * math.pi / seq_len)
    c_hi, s_hi = jnp.cos(ang_hi)[None, :, None, :], jnp.sin(ang_hi)[None, :, None, :]
    lo = lambda t: t.reshape(split, RADIX, chunk).transpose(1, 0, 2)[:, None, :, :]
    c_lo, s_lo = lo(jnp.cos(ang_lo)), lo(jnp.sin(ang_lo))
    cos = (c_hi * c_lo - s_hi * s_lo).reshape(RADIX, chunk, chunk)
    sin = (s_hi * c_lo + c_hi * s_lo).reshape(RADIX, chunk, chunk)
    return jnp.concatenate([cos, -sin], axis=2).astype(BF16)


def _seq_dft(a, b, dft_m):
    n_halves, bsz, half_len, _ = a.shape
    seq_len = 2 * half_len
    chunk = seq_len // RADIX
    col_spec = pl.BlockSpec((1, 1, half_len, DFT_COLS), lambda bi, h: (h, bi, 0, 0))
    m_spec = pl.BlockSpec((RADIX, chunk, 2 * chunk), lambda bi, h: (0, 0, 0), pipeline_mode=pl.Buffered(1))
    slabs = pltpu.VMEM((DFT_COLS // LANES, seq_len, LANES), F32)
    return pl.pallas_call(
        functools.partial(_seq_dft_kernel, chunk=chunk),
        grid=(bsz, n_halves),
        in_specs=[col_spec, col_spec, m_spec],
        out_specs=col_spec,
        out_shape=jax.ShapeDtypeStruct(a.shape, jnp.uint32),
        scratch_shapes=[pltpu.VMEM((RADIX, 2 * chunk, DFT_COLS), BF16), slabs],
        compiler_params=_compiler_params(2),
        name="seq_dft",
    )(a, b, dft_m)


def _out_proj_rows(rows, x_ref, ma_ref, f_ref, sgf_ref, w_ref, o_ref):
    f = jnp.concatenate([_unpack_rows(f_ref[h, _half_rows(rows), :])
                         for h in range(FOURIER_WIDTH // DFT_COLS)], axis=1)
    mf = f * sgf_ref[rows, :]
    o_ref[rows, :] = (x_ref[rows, :] + _dot(ma_ref[rows, :], w_ref[:ATTN_WIDTH, :])
                      + _dot(mf, w_ref[ATTN_WIDTH:, :]))


def _out_proj_kernel(x_ref, ma_ref, f_ref, sgf_ref, w_ref, o_ref):
    _out_proj_rows(slice(None), x_ref, ma_ref, f_ref, sgf_ref, w_ref, o_ref)


N_OUT_PROJ_INPUTS = 5


def _out_in_proj_kernel(*refs):
    out_in = refs[:N_OUT_PROJ_INPUTS]
    in_refs = refs[N_OUT_PROJ_INPUTS:N_OUT_PROJ_INPUTS + N_IN_PROJ_INPUTS]
    y_ref = refs[N_OUT_PROJ_INPUTS + N_IN_PROJ_INPUTS]
    out_refs = refs[N_OUT_PROJ_INPUTS + N_IN_PROJ_INPUTS + 1:]
    for sub in range(y_ref.shape[0] // FUSED_SUB_ROWS):
        rows = slice(sub * FUSED_SUB_ROWS, (sub + 1) * FUSED_SUB_ROWS)
        _out_proj_rows(rows, *out_in, y_ref)
        _in_proj_rows(y_ref[rows, :], rows, *in_refs, *out_refs)


def _out_proj_specs(tm):
    return [_row_spec(tm, D_MODEL), _row_spec(tm, ATTN_WIDTH), _split_spec(tm),
            _row_spec(tm, FOURIER_WIDTH), _resident_spec((D_MODEL, D_MODEL))]


def _out_proj(x2d, ma, f, sgf, w_bf):
    rows = x2d.shape[0]
    tm = ROW_TILE
    return pl.pallas_call(
        _out_proj_kernel,
        grid=(rows // tm,),
        in_specs=_out_proj_specs(tm),
        out_specs=_row_spec(tm, D_MODEL),
        out_shape=jax.ShapeDtypeStruct((rows, D_MODEL), F32),
        compiler_params=_compiler_params(1),
        name="out_proj",
    )(x2d, ma, f, sgf, w_bf)


def _out_in_proj(x2d, ma, f, sgf, w_bf, seq_len, in_w):
    rows = x2d.shape[0]
    tm = FUSED_ROW_TILE
    i_in_specs, i_out_specs, i_out_shape = _in_proj_specs(tm, rows, seq_len)
    outs = pl.pallas_call(
        _out_in_proj_kernel,
        grid=(rows // tm,),
        in_specs=_out_proj_specs(tm) + i_in_specs,
        out_specs=[_row_spec(tm, D_MODEL)] + i_out_specs,
        out_shape=[jax.ShapeDtypeStruct((rows, D_MODEL), F32)] + i_out_shape,
        compiler_params=_compiler_params(1, FUSED_VMEM_LIMIT),
        name="out_in_proj",
    )(x2d, ma, f, sgf, w_bf, *in_w)
    return outs[0], outs[1:]


def _rope_tables(seq_len):
    half = HEAD_DIM // 2
    inv_freq = 1.0 / (ROPE_THETA ** (jnp.arange(half, dtype=F32) / half))
    ang = jnp.arange(seq_len, dtype=F32)[:, None] * inv_freq[None, :]
    cos = jnp.cos(ang)
    sin = jnp.sin(ang)
    cos_t = jnp.concatenate([cos, cos, cos, cos], axis=1)
    sin_t = jnp.concatenate([-sin, sin, -sin, sin], axis=1)
    return cos_t, sin_t


def _rope_gains(q_gain, k_gain):
    pair = lambda g: jnp.tile(g, LANES // HEAD_DIM)
    rot = lambda g: jnp.roll(g, HEAD_DIM // 2)
    gq = q_gain.astype(F32) * (HEAD_DIM ** -0.5 * LOG2E)
    gk = k_gain.astype(F32)
    return jnp.stack([pair(gq), pair(rot(gq)), pair(gk), pair(rot(gk))])


def _head_mean_matrix(width):
    head = jnp.arange(width, dtype=jnp.int32) // HEAD_DIM
    return jnp.where(head[:, None] == head[None, :], 1.0 / HEAD_DIM, 0.0).astype(BF16)


def _trunk(x, layers, dft_m):
    bsz, seq_len, _ = x.shape
    x2d = x.reshape(bsz * seq_len, D_MODEL)
    r3 = lambda t: t.reshape(bsz, seq_len, t.shape[-1])
    split4 = lambda t: t.reshape(t.shape[0], bsz, seq_len // 2, DFT_COLS)
    proj = _in_proj(x2d, seq_len, layers[0]["in_w"])
    for l, layer in enumerate(layers):
        q, ka, kb, va, vb, sga, a, b, sgf = proj
        f = _seq_dft(split4(a), split4(b), dft_m)
        attn_args = (layer["sink2"], layer["bounded"], r3(q), r3(ka), r3(kb), r3(va), r3(vb), r3(sga))
        if l + 1 < len(layers):
            ma = _attention(*attn_args).reshape(bsz * seq_len, ATTN_WIDTH)
            x2d, proj = _out_in_proj(x2d, ma, f.reshape(f.shape[0], bsz * seq_len // 2, DFT_COLS), sgf,
                                     layer["w_out"], seq_len, layers[l + 1]["in_w"])
        else:
            return _attention_out_proj(*attn_args, r3(x2d), f, r3(sgf), layer["w_out"])


def kernel(x_prompt, x_sample, norm_gain, w_in, q_norm_gain, k_norm_gain, sink_logit, w_fourier, w_out):
    depth = norm_gain.shape[0]
    seq_p, seq_s = x_prompt.shape[1], x_sample.shape[1]
    assert seq_p == seq_s
    cos_t, sin_t = _rope_tables(seq_p)
    hs = _head_mean_matrix(2 * LANES)
    layers = []
    for l in range(depth):
        sink2 = sink_logit[l].astype(F32) * LOG2E
        logit_bound = (HEAD_DIM ** 0.5 * LOG2E) * jnp.max(jnp.abs(q_norm_gain[l])) * jnp.max(jnp.abs(k_norm_gain[l]))
        bounded = jnp.maximum(logit_bound, jnp.max(jnp.abs(sink2))) <= MAX_UNSHIFTED_LOGIT2
        in_w = (norm_gain[l].astype(F32)[None, :], w_in[l].astype(BF16), cos_t, sin_t,
                _rope_gains(q_norm_gain[l], k_norm_gain[l]), hs, _fourier_weights(w_fourier[l], seq_p))
        layers.append(dict(in_w=in_w, sink2=sink2, bounded=bounded.astype(jnp.int32)[None],
                           w_out=w_out[l].astype(BF16)))
    dft_m = _dft_matrix(seq_p)
    return (_trunk(x_prompt, layers, dft_m), _trunk(x_sample, layers, dft_m))
```

```python
import functools
import math

import jax
import jax.numpy as jnp
from jax.experimental import pallas as pl
from jax.experimental.pallas import tpu as pltpu

D_MODEL = 1024
HEAD_DIM = 64
N_Q_HEADS = 8
N_KV_HEADS = 2
ATTN_WIDTH = N_Q_HEADS * HEAD_DIM
KV_WIDTH = N_KV_HEADS * HEAD_DIM
FOURIER_WIDTH = D_MODEL - ATTN_WIDTH
N_GROUPS = 4
GROUP_DIM = FOURIER_WIDTH // N_GROUPS
IN_WIDTH = 2 * ATTN_WIDTH + 2 * KV_WIDTH + 2 * FOURIER_WIDTH
Q_OFF = 0
K_OFF = ATTN_WIDTH
V_OFF = K_OFF + KV_WIDTH
GA_OFF = V_OFF + KV_WIDTH
U_OFF = GA_OFF + ATTN_WIDTH
GF_OFF = U_OFF + FOURIER_WIDTH
BLOCK = 128
ROPE_THETA = 10000.0
EPS = 1e-6
NEG = -1e30
LANES = 128
RADIX = 8
DFT_COLS = 256
DFT_BLOCK_ROWS = 128
LOG2E = math.log2(math.e)
MAX_UNSHIFTED_LOGIT2 = 100.0

ROW_TILE = 1024
FUSED_ROW_TILE = 1024
FUSED_SUB_ROWS = 512
ATTN_TILE = 1024
VMEM_LIMIT = 48 * 1024 * 1024
FUSED_VMEM_LIMIT = 54 * 1024 * 1024

BF16 = jnp.bfloat16
F32 = jnp.float32


def _compiler_params(grid_rank, vmem_limit=VMEM_LIMIT):
    return pltpu.CompilerParams(dimension_semantics=("arbitrary",) * grid_rank, vmem_limit_bytes=vmem_limit)


def _dot(a, b):
    return jnp.dot(a, b, preferred_element_type=F32)


def _dot_narrow(a, b):
    half = a.shape[0] // 2
    return jnp.concatenate([_dot(a[:half], b), _dot(a[half:], b)], axis=0)


def _dot_nt(a, b):
    return jax.lax.dot_general(a, b, (((1,), (1,)), ((), ())), preferred_element_type=F32)


def _silu(x):
    return x / (1.0 + jnp.exp(-x))


def _pack_rows(x):
    return pltpu.bitcast(x.astype(BF16), jnp.uint32)


def _unpack_rows(bits):
    return pltpu.bitcast(bits, BF16)


def _half_rows(rows):
    return rows if rows == slice(None) else slice(rows.start // 2, rows.stop // 2)


def _rotate_half(t, first_half):
    width = t.shape[-1]
    fwd = pltpu.roll(t, HEAD_DIM // 2, axis=1)
    bwd = pltpu.roll(t, width - HEAD_DIM // 2, axis=1)
    return jnp.where(first_half, bwd, fwd)


def _fourier_weight_kernel(cs_ref, w_ref, o_ref):
    for g in range(N_GROUPS):
        w = w_ref[g]
        c = jnp.dot(cs_ref[0], w, preferred_element_type=F32, precision=jax.lax.Precision.HIGHEST)
        s = jnp.dot(cs_ref[1], w, preferred_element_type=F32, precision=jax.lax.Precision.HIGHEST)
        o_ref[g] = jnp.concatenate([c, s], axis=1).astype(o_ref.dtype)


def _fourier_weights(w_four, seq_len):
    idx = (jnp.arange(GROUP_DIM, dtype=jnp.int32)[:, None] * jnp.arange(GROUP_DIM, dtype=jnp.int32)[None, :]) % GROUP_DIM
    ang = idx.astype(F32) * (2.0 * math.pi / GROUP_DIM)
    scale = 1.0 / math.sqrt(seq_len * GROUP_DIM)
    cs = jnp.stack([jnp.cos(ang), jnp.sin(ang)]) * scale
    return pl.pallas_call(
        _fourier_weight_kernel,
        out_shape=jax.ShapeDtypeStruct((N_GROUPS, GROUP_DIM, 2 * GROUP_DIM), BF16),
        name="fourier_weight_prep",
    )(cs, w_four)


def _in_proj_rows(x, rows, gn_ref, w_ref, cos_ref, sin_ref, gains_ref, hs_ref, wab_ref,
                  q_ref, ka_ref, kb_ref, va_ref, vb_ref, sga_ref, a_ref, b_ref, sgf_ref):
    sub_rows = x.shape[0]
    xb = (x * gn_ref[...]).astype(BF16)
    cos = cos_ref[rows, :]
    sin = sin_ref[rows, :]
    qcos, qsin = cos * gains_ref[0:1, :], sin * gains_ref[1:2, :]
    kcos, ksin = cos * gains_ref[2:3, :], sin * gains_ref[3:4, :]
    ms = jnp.mean(x * x, axis=-1, keepdims=True)
    r = jnp.broadcast_to(jax.lax.rsqrt(ms + EPS), (sub_rows, LANES))
    eps_z = jnp.broadcast_to(EPS * (ms + EPS), (sub_rows, LANES))
    wide = lambda t, width: jnp.concatenate([t] * (width // LANES), axis=1)

    lane = jax.lax.broadcasted_iota(jnp.int32, (1, LANES), 1)
    first_half_pair = (lane % HEAD_DIM) < (HEAD_DIM // 2)
    low_head = lane < HEAD_DIM
    hs = hs_ref[...]

    def head_mean_sq(z):
        return _dot_narrow((z * z).astype(BF16), hs)

    def silu_of_scaled(z, width):
        h = z * wide(0.5 * r, width)
        return h + h * jnp.tanh(h)

    zq = _dot(xb, w_ref[:, Q_OFF:Q_OFF + ATTN_WIDTH])
    half = ATTN_WIDTH // 2
    ssq = jnp.concatenate([head_mean_sq(zq[:, :half]), head_mean_sq(zq[:, half:])], axis=1)
    qr = (zq * wide(qcos, ATTN_WIDTH)
          + _rotate_half(zq, wide(first_half_pair, ATTN_WIDTH)) * wide(qsin, ATTN_WIDTH))
    q_ref[rows, :] = (qr * jax.lax.rsqrt(ssq + wide(eps_z, ATTN_WIDTH))).astype(q_ref.dtype)

    zkv = _dot_narrow(xb, w_ref[:, K_OFF:K_OFF + 2 * KV_WIDTH])
    zk = zkv[:, :KV_WIDTH]
    ssk = head_mean_sq(zkv)[:, :KV_WIDTH]
    kr = (zk * kcos + _rotate_half(zk, first_half_pair) * ksin) * jax.lax.rsqrt(ssk + eps_z)
    kr_sw = pltpu.roll(kr, HEAD_DIM, axis=1)
    ka_ref[rows, :] = jnp.where(low_head, kr, kr_sw).astype(ka_ref.dtype)
    kb_ref[rows, :] = jnp.where(low_head, kr_sw, kr).astype(kb_ref.dtype)
    zv = zkv[:, KV_WIDTH:] * r
    zv_sw = pltpu.roll(zv, HEAD_DIM, axis=1)
    va_ref[rows, :] = jnp.where(low_head, zv, zv_sw).astype(va_ref.dtype)
    vb_ref[rows, :] = jnp.where(low_head, zv_sw, zv).astype(vb_ref.dtype)

    zga = _dot(xb, w_ref[:, GA_OFF:GA_OFF + ATTN_WIDTH])
    sga_ref[rows, :] = silu_of_scaled(zga, ATTN_WIDTH).astype(sga_ref.dtype)
    zgf = _dot(xb, w_ref[:, GF_OFF:GF_OFF + FOURIER_WIDTH])
    sgf_ref[rows, :] = silu_of_scaled(zgf, FOURIER_WIDTH).astype(sgf_ref.dtype)

    zu = (_dot(xb, w_ref[:, U_OFF:U_OFF + FOURIER_WIDTH]) * wide(r, FOURIER_WIDTH)).astype(BF16)
    for g in range(N_GROUPS):
        ab = _dot_narrow(zu[:, g * GROUP_DIM:(g + 1) * GROUP_DIM], wab_ref[g])
        half_idx, col = divmod(g * GROUP_DIM, DFT_COLS)
        a_ref[half_idx, _half_rows(rows), col:col + GROUP_DIM] = _pack_rows(ab[:, :GROUP_DIM])
        b_ref[half_idx, _half_rows(rows), col:col + GROUP_DIM] = _pack_rows(ab[:, GROUP_DIM:])


N_IN_PROJ_INPUTS = 7


def _in_proj_kernel(x_ref, *refs):
    _in_proj_rows(x_ref[...], slice(None), *refs)


def _row_spec(tm, width):
    return pl.BlockSpec((tm, width), lambda i: (i, 0))


def _split_spec(tm, width=DFT_COLS):
    return pl.BlockSpec((FOURIER_WIDTH // width, tm // 2, width), lambda i: (0, i, 0))


def _resident_spec(shape):
    return pl.BlockSpec(shape, lambda i: (0,) * len(shape), pipeline_mode=pl.Buffered(1))


def _in_proj_specs(tm, rows, seq_len):
    steps_per_seq = seq_len // tm
    tab_spec = pl.BlockSpec((tm, LANES), lambda i: (i % steps_per_seq, 0))
    in_specs = [_resident_spec((1, D_MODEL)), _resident_spec((D_MODEL, IN_WIDTH)), tab_spec, tab_spec,
                _resident_spec((4, LANES)), _resident_spec((2 * LANES, 2 * LANES)),
                _resident_spec((N_GROUPS, GROUP_DIM, 2 * GROUP_DIM))]
    row_widths = (ATTN_WIDTH, LANES, LANES, LANES, LANES, ATTN_WIDTH)
    out_specs = ([_row_spec(tm, w) for w in row_widths] + [_split_spec(tm)] * 2
                 + [_row_spec(tm, FOURIER_WIDTH)])
    out_shape = ([jax.ShapeDtypeStruct((rows, w), BF16) for w in row_widths]
                 + [jax.ShapeDtypeStruct((FOURIER_WIDTH // DFT_COLS, rows // 2, DFT_COLS), jnp.uint32)] * 2
                 + [jax.ShapeDtypeStruct((rows, FOURIER_WIDTH), BF16)])
    return in_specs, out_specs, out_shape


def _in_proj(x2d, seq_len, in_w):
    rows = x2d.shape[0]
    tm = ROW_TILE
    in_specs, out_specs, out_shape = _in_proj_specs(tm, rows, seq_len)
    return pl.pallas_call(
        _in_proj_kernel,
        grid=(rows // tm,),
        in_specs=[_row_spec(tm, D_MODEL)] + in_specs,
        out_specs=out_specs,
        out_shape=out_shape,
        compiler_params=_compiler_params(1),
        name="in_proj",
    )(x2d, *in_w)


def _attn_kernel(sink_ref, bounded_ref, q_ref, ka_ref, kb_ref, va_ref, vb_ref, sg_ref, o_ref, *, n_blocks):
    refs = (sink_ref, q_ref, ka_ref, kb_ref, va_ref, vb_ref, sg_ref, o_ref)

    @pl.when(bounded_ref[0] == 1)
    def _():
        _attn_body(*refs, n_blocks=n_blocks, shift=False)

    @pl.when(bounded_ref[0] == 0)
    def _():
        _attn_body(*refs, n_blocks=n_blocks, shift=True)


def _attn_body(sink_ref, q_ref, ka_ref, kb_ref, va_ref, vb_ref, sg_ref, o_ref, *, n_blocks, shift):
    i = pl.program_id(1)
    blocks_per_step = ATTN_TILE // BLOCK
    row = jax.lax.broadcasted_iota(jnp.int32, (BLOCK, 2 * BLOCK), 0)
    col = jax.lax.broadcasted_iota(jnp.int32, (BLOCK, 2 * BLOCK), 1) % BLOCK
    lane = jax.lax.broadcasted_iota(jnp.int32, (1, LANES), 1)
    low = lane < HEAD_DIM
    zero = jnp.zeros((), BF16)
    lane_full = jax.lax.broadcasted_iota(jnp.int32, (BLOCK, LANES), 1)
    ones_low = jnp.where(lane_full < HEAD_DIM, 1.0, 0.0).astype(BF16)
    ones_high = jnp.where(lane_full < HEAD_DIM, 0.0, 1.0).astype(BF16)

    for jb in range(blocks_per_step):
        ib = i * blocks_per_step + jb
        rows = slice(jb * BLOCK, (jb + 1) * BLOCK)
        edge_prev = jnp.where(ib == 0, NEG, 0.0)
        edge_next = jnp.where(ib == n_blocks - 1, NEG, 0.0)
        bias_prev = jnp.where(col >= row, 0.0, NEG) + edge_prev
        bias_next = jnp.where(col <= row, 0.0, NEG) + edge_next
        starts = [pl.multiple_of(jnp.clip(ib + c, 0, n_blocks - 1) * BLOCK, BLOCK) for c in (-1, 0, 1)]
        for kvh, (k_ref, v_ref) in enumerate(((ka_ref, va_ref), (kb_ref, vb_ref))):
            kbd, vbd = [], []
            for st in starts:
                kblk = k_ref[0, pl.ds(st, BLOCK), :]
                vblk = v_ref[0, pl.ds(st, BLOCK), :]
                kbd.append(jnp.concatenate([jnp.where(low, kblk, zero), jnp.where(low, zero, kblk)], axis=0))
                vbd.append(jnp.concatenate([
                    jnp.concatenate([jnp.where(low, vblk, zero), ones_low], axis=1),
                    jnp.concatenate([jnp.where(low, zero, vblk), ones_high], axis=1)], axis=0))
            vbd = jnp.concatenate(vbd, axis=0)
            for pair in range(2):
                pidx = kvh * 2 + pair
                lanes = slice(pidx * LANES, (pidx + 1) * LANES)
                qp = q_ref[0, rows, lanes]
                scores = [_dot_nt(qp, kb) for kb in kbd]
                sink_e = sink_ref[2 * pidx]
                sink_o = sink_ref[2 * pidx + 1]
                if shift:
                    scores = [scores[0] + bias_prev, scores[1], scores[2] + bias_next]
                    smax = jnp.maximum(jnp.maximum(scores[0], scores[1]), scores[2])
                    m_e = jnp.maximum(jnp.max(smax[:, :BLOCK], axis=-1, keepdims=True), sink_e)
                    m_o = jnp.maximum(jnp.max(smax[:, BLOCK:], axis=-1, keepdims=True), sink_o)
                    m_both = jnp.concatenate([jnp.broadcast_to(m_e, (BLOCK, BLOCK)),
                                              jnp.broadcast_to(m_o, (BLOCK, BLOCK))], axis=1)
                    probs = [jnp.exp2(s - m_both).astype(BF16) for s in scores]
                    p_sink = jnp.where(low, jnp.exp2(sink_e - m_e), jnp.exp2(sink_o - m_o))
                else:
                    scores = [s.astype(BF16) for s in scores]
                    scores = [scores[0] + bias_prev.astype(BF16), scores[1], scores[2] + bias_next.astype(BF16)]
                    probs = [jnp.exp2(s) for s in scores]
                    p_sink = jnp.exp2(jnp.where(low, sink_e, sink_o))
                acc = _dot(jnp.concatenate(probs, axis=1), vbd)
                out = acc[:, :LANES] / (acc[:, LANES:] + p_sink) * sg_ref[0, rows, lanes].astype(F32)
                o_ref[0, rows, lanes] = out.astype(o_ref.dtype)


def _attn_out_kernel(sink_ref, bounded_ref, q_ref, ka_ref, kb_ref, va_ref, vb_ref, sg_ref,
                     x_ref, f_ref, sgf_ref, w_ref, y_ref, ma_ref, *, n_blocks):
    _attn_kernel(sink_ref, bounded_ref, q_ref, ka_ref, kb_ref, va_ref, vb_ref, sg_ref, ma_ref, n_blocks=n_blocks)
    f = jnp.concatenate([_unpack_rows(f_ref[s, 0]) for s in range(FOURIER_WIDTH // LANES)], axis=1)
    y_ref[0] =(x_ref[0] + _dot(ma_ref[0], w_ref[:ATTN_WIDTH, :])
                + _dot(f * sgf_ref[0], w_ref[ATTN_WIDTH:, :]))


def _attention_out_proj(sink2, bounded, q, ka, kb, va, vb, sga, x, f, sgf, w_bf):
    bsz, seq_len, _ = q.shape
    tq = ATTN_TILE
    tile = lambda width: pl.BlockSpec((1, tq, width), lambda b, i, s, fl: (b, i, 0))
    kv_spec = pl.BlockSpec((1, seq_len, LANES), lambda b, i, s, fl: (b, 0, 0))
    f_spec = pl.BlockSpec((FOURIER_WIDTH // LANES, 1, tq // 2, LANES), lambda b, i, s, fl: (0, b, i, 0))
    w_spec = pl.BlockSpec((D_MODEL, D_MODEL), lambda b, i, s, fl: (0, 0), pipeline_mode=pl.Buffered(1))
    grid_spec = pltpu.PrefetchScalarGridSpec(
        num_scalar_prefetch=2,
        grid=(bsz, seq_len // tq),
        in_specs=[tile(ATTN_WIDTH), kv_spec, kv_spec, kv_spec, kv_spec, tile(ATTN_WIDTH),
                  tile(D_MODEL), f_spec, tile(FOURIER_WIDTH), w_spec],
        out_specs=tile(D_MODEL),
        scratch_shapes=[pltpu.VMEM((1, tq, ATTN_WIDTH), BF16)],
    )
    return pl.pallas_call(
        functools.partial(_attn_out_kernel, n_blocks=seq_len // BLOCK),
        grid_spec=grid_spec,
        out_shape=jax.ShapeDtypeStruct((bsz, seq_len, D_MODEL), F32),
        compiler_params=_compiler_params(2),
        name="attention_out_proj",
    )(sink2, bounded, q, ka, kb, va, vb, sga, x, f, sgf, w_bf)


def _attention(sink2, bounded, q, ka, kb, va, vb, sga):
    bsz, seq_len, _ = q.shape
    tq = ATTN_TILE
    q_spec = pl.BlockSpec((1, tq, ATTN_WIDTH), lambda b, i, s, f: (b, i, 0))
    kv_spec = pl.BlockSpec((1, seq_len, LANES), lambda b, i, s, f: (b, 0, 0))
    grid_spec = pltpu.PrefetchScalarGridSpec(
        num_scalar_prefetch=2,
        grid=(bsz, seq_len // tq),
        in_specs=[q_spec, kv_spec, kv_spec, kv_spec, kv_spec, q_spec],
        out_specs=q_spec,
    )
    return pl.pallas_call(
        functools.partial(_attn_kernel, n_blocks=seq_len // BLOCK),
        grid_spec=grid_spec,
        out_shape=jax.ShapeDtypeStruct((bsz, seq_len, ATTN_WIDTH), BF16),
        compiler_params=_compiler_params(2),
        name="band_attention",
    )(sink2, bounded, q, ka, kb, va, vb, sga)


def _seq_dft_kernel(a_ref, b_ref, m_ref, o_ref, zz_ref, pair_ref, *, chunk):
    half = RADIX // 2
    root_half = math.sqrt(0.5)
    cadd = lambda u, v: (u[0] + v[0], u[1] + v[1])
    csub = lambda u, v: (u[0] - v[0], u[1] - v[1])
    add_i = lambda u, v: (u[0] - v[1], u[1] + v[0])
    sub_i = lambda u, v: (u[0] + v[1], u[1] - v[0])

    def four_point(w):
        s02, d02, s13, d13 = cadd(w[0], w[2]), csub(w[0], w[2]), cadd(w[1], w[3]), csub(w[1], w[3])
        return [cadd(s02, s13), add_i(d02, d13), csub(s02, s13), sub_i(d02, d13)]

    def butterflies(parity):
        for rb in range(chunk // DFT_BLOCK_ROWS):
            for cb in range(DFT_COLS // LANES):
                rows = lambda q: slice(q * chunk + rb * DFT_BLOCK_ROWS, q * chunk + (rb + 1) * DFT_BLOCK_ROWS)
                cols = slice(cb * LANES, (cb + 1) * LANES)
                z = [(_unpack_rows(a_ref[0, 0, _half_rows(rows(q)), cols]),
                      _unpack_rows(b_ref[0, 0, _half_rows(rows(q)), cols])) for q in range(RADIX)]
                if parity == 0:
                    w = [cadd(z[q], z[q + half]) for q in range(half)]
                else:
                    d = [csub(z[q], z[q + half]) for q in range(half)]
                    w = [d[0],
                         ((d[1][0] - d[1][1]) * root_half, (d[1][0] + d[1][1]) * root_half),
                         (-d[2][1], d[2][0]),
                         ((-d[3][0] - d[3][1]) * root_half, (d[3][0] - d[3][1]) * root_half)]
                for t, (re, im) in enumerate(four_point(w)):
                    zz_ref[2 * t + parity, rows(0), cols] = re
                    zz_ref[2 * t + parity, rows(1), cols] = im

    def project(r):
        f = _dot(m_ref[r], zz_ref[r])
        for cb in range(DFT_COLS // LANES):
            pair_ref[r // 2, cb, pl.ds(r % 2, chunk, stride=2), :] = f[:, cb * LANES:(cb + 1) * LANES]

    butterflies(0)
    for t in range(half):
        project(2 * t)
    butterflies(1)
    for t in range(half):
        project(2 * t + 1)
        for cb in range(DFT_COLS // LANES):
            o_ref[cb, 0, pl.ds(t, chunk, stride=half), :] = _pack_rows(pair_ref[t, cb])


def _dft_matrix(seq_len):
    chunk = seq_len // RADIX
    split = 32
    period = seq_len // (RADIX * split)
    m = jnp.arange(chunk, dtype=jnp.int32)[None, :]
    ang_hi = ((jnp.arange(chunk // split, dtype=jnp.int32)[:, None] * m) % period).astype(F32) * (2.0 * math.pi / period)
    ang_lo = ((jnp.arange(RADIX * split, dtype=jnp.int32)[:, None] * m) % seq_len).astype(F32) * (2.0 * math.pi / seq_len)
    c_hi, s_hi = jnp.cos(ang_hi)[None, :, None, :], jnp.sin(ang_hi)[None, :, None, :]
    lo = lambda t: t.reshape(split, RADIX, chunk).transpose(1, 0, 2)[:, None, :, :]
    c_lo, s_lo = lo(jnp.cos(ang_lo)), lo(jnp.sin(ang_lo))
    cos = (c_hi * c_lo - s_hi * s_lo).reshape(RADIX, chunk, chunk)
    sin = (s_hi * c_lo + c_hi * s_lo).reshape(RADIX, chunk, chunk)
    return jnp.concatenate([cos, -sin], axis=2).astype(BF16)


def _seq_dft(a, b, dft_m):
    n_halves, bsz, half_len, _ = a.shape
    seq_len = 2 * half_len
    chunk = seq_len // RADIX
    slabs_per_step = DFT_COLS // LANES
    col_spec = pl.BlockSpec((1, 1, half_len, DFT_COLS), lambda bi, h: (h, bi, 0, 0))
    out_spec = pl.BlockSpec((slabs_per_step, 1, half_len, LANES), lambda bi, h: (h, bi, 0, 0))
    m_spec = pl.BlockSpec((RADIX, chunk, 2 * chunk), lambda bi, h: (0, 0, 0), pipeline_mode=pl.Buffered(1))
    return pl.pallas_call(
        functools.partial(_seq_dft_kernel, chunk=chunk),
        grid=(bsz, n_halves),
        in_specs=[col_spec, col_spec, m_spec],
        out_specs=out_spec,
        out_shape=jax.ShapeDtypeStruct((n_halves * slabs_per_step, bsz, half_len, LANES), jnp.uint32),
        scratch_shapes=[pltpu.VMEM((RADIX, 2 * chunk, DFT_COLS), BF16),
                        pltpu.VMEM((RADIX // 2, slabs_per_step, 2 * chunk, LANES), F32)],
        compiler_params=_compiler_params(2),
        name="seq_dft",
    )(a, b, dft_m)


def _out_proj_rows(rows, x_ref, ma_ref, f_ref, sgf_ref, w_ref, o_ref):
    f = jnp.concatenate([_unpack_rows(f_ref[s, _half_rows(rows), :])
                         for s in range(FOURIER_WIDTH // LANES)], axis=1)
    mf = f * sgf_ref[rows, :]
    o_ref[rows, :] = (x_ref[rows, :] + _dot(ma_ref[rows, :], w_ref[:ATTN_WIDTH, :])
                      + _dot(mf, w_ref[ATTN_WIDTH:, :]))


def _out_proj_kernel(x_ref, ma_ref, f_ref, sgf_ref, w_ref, o_ref):
    _out_proj_rows(slice(None), x_ref, ma_ref, f_ref, sgf_ref, w_ref, o_ref)


N_OUT_PROJ_INPUTS = 5


def _out_in_proj_kernel(*refs):
    out_in = refs[:N_OUT_PROJ_INPUTS]
    in_refs = refs[N_OUT_PROJ_INPUTS:N_OUT_PROJ_INPUTS + N_IN_PROJ_INPUTS]
    y_ref = refs[N_OUT_PROJ_INPUTS + N_IN_PROJ_INPUTS]
    out_refs = refs[N_OUT_PROJ_INPUTS + N_IN_PROJ_INPUTS + 1:]
    for sub in range(y_ref.shape[0] // FUSED_SUB_ROWS):
        rows = slice(sub * FUSED_SUB_ROWS, (sub + 1) * FUSED_SUB_ROWS)
        _out_proj_rows(rows, *out_in, y_ref)
        _in_proj_rows(y_ref[rows, :], rows, *in_refs, *out_refs)


def _out_proj_specs(tm):
    return [_row_spec(tm, D_MODEL), _row_spec(tm, ATTN_WIDTH), _split_spec(tm, LANES),
            _row_spec(tm, FOURIER_WIDTH), _resident_spec((D_MODEL, D_MODEL))]


def _out_proj(x2d, ma, f, sgf, w_bf):
    rows = x2d.shape[0]
    tm = ROW_TILE
    return pl.pallas_call(
        _out_proj_kernel,
        grid=(rows // tm,),
        in_specs=_out_proj_specs(tm),
        out_specs=_row_spec(tm, D_MODEL),
        out_shape=jax.ShapeDtypeStruct((rows, D_MODEL), F32),
        compiler_params=_compiler_params(1),
        name="out_proj",
    )(x2d, ma, f, sgf, w_bf)


def _out_in_proj(x2d, ma, f, sgf, w_bf, seq_len, in_w):
    rows = x2d.shape[0]
    tm = FUSED_ROW_TILE
    i_in_specs, i_out_specs, i_out_shape = _in_proj_specs(tm, rows, seq_len)
    outs = pl.pallas_call(
        _out_in_proj_kernel,
        grid=(rows // tm,),
        in_specs=_out_proj_specs(tm) + i_in_specs,
        out_specs=[_row_spec(tm, D_MODEL)] + i_out_specs,
        out_shape=[jax.ShapeDtypeStruct((rows, D_MODEL), F32)] + i_out_shape,
        compiler_params=_compiler_params(1, FUSED_VMEM_LIMIT),
        name="out_in_proj",
    )(x2d, ma, f, sgf, w_bf, *in_w)
    return outs[0], outs[1:]


def _rope_tables(seq_len):
    half = HEAD_DIM // 2
    inv_freq = 1.0 / (ROPE_THETA ** (jnp.arange(half, dtype=F32) / half))
    ang = jnp.arange(seq_len, dtype=F32)[:, None] * inv_freq[None, :]
    cos = jnp.cos(ang)
    sin = jnp.sin(ang)
    cos_t = jnp.concatenate([cos, cos, cos, cos], axis=1)
    sin_t = jnp.concatenate([-sin, sin, -sin, sin], axis=1)
    return cos_t, sin_t


def _rope_gains(q_gain, k_gain):
    pair = lambda g: jnp.tile(g, LANES // HEAD_DIM)
    rot = lambda g: jnp.roll(g, HEAD_DIM // 2)
    gq = q_gain.astype(F32) * (HEAD_DIM ** -0.5 * LOG2E)
    gk = k_gain.astype(F32)
    return jnp.stack([pair(gq), pair(rot(gq)), pair(gk), pair(rot(gk))])


def _head_mean_matrix(width):
    head = jnp.arange(width, dtype=jnp.int32) // HEAD_DIM
    return jnp.where(head[:, None] == head[None, :], 1.0 / HEAD_DIM, 0.0).astype(BF16)


def _trunk(x, layers, dft_m):
    bsz, seq_len, _ = x.shape
    x2d = x.reshape(bsz * seq_len, D_MODEL)
    r3 = lambda t: t.reshape(bsz, seq_len, t.shape[-1])
    split4 = lambda t: t.reshape(t.shape[0], bsz, seq_len // 2, DFT_COLS)
    proj = _in_proj(x2d, seq_len, layers[0]["in_w"])
    for l, layer in enumerate(layers):
        q, ka, kb, va, vb, sga, a, b, sgf = proj
        f = _seq_dft(split4(a), split4(b), dft_m)
        attn_args = (layer["sink2"], layer["bounded"], r3(q), r3(ka), r3(kb), r3(va), r3(vb), r3(sga))
        if l + 1 < len(layers):
            ma = _attention(*attn_args).reshape(bsz * seq_len, ATTN_WIDTH)
            x2d, proj = _out_in_proj(x2d, ma, f.reshape(f.shape[0], bsz * seq_len // 2, LANES), sgf,
                                     layer["w_out"], seq_len, layers[l + 1]["in_w"])
        else:
            return _attention_out_proj(*attn_args, r3(x2d), f, r3(sgf), layer["w_out"])


def kernel(x_prompt, x_sample, norm_gain, w_in, q_norm_gain, k_norm_gain, sink_logit, w_fourier, w_out):
    depth = norm_gain.shape[0]
    seq_p, seq_s = x_prompt.shape[1], x_sample.shape[1]
    assert seq_p == seq_s
    cos_t, sin_t = _rope_tables(seq_p)
    hs = _head_mean_matrix(2 * LANES)
    layers = []
    for l in range(depth):
        sink2 = sink_logit[l].astype(F32) * LOG2E
        logit_bound = (HEAD_DIM ** 0.5 * LOG2E) * jnp.max(jnp.abs(q_norm_gain[l])) * jnp.max(jnp.abs(k_norm_gain[l]))
        bounded = jnp.maximum(logit_bound, jnp.max(jnp.abs(sink2))) <= MAX_UNSHIFTED_LOGIT2
        in_w = (norm_gain[l].astype(F32)[None, :], w_in[l].astype(BF16), cos_t, sin_t,
                _rope_gains(q_norm_gain[l], k_norm_gain[l]), hs, _fourier_weights(w_fourier[l], seq_p))
        layers.append(dict(in_w=in_w, sink2=sink2, bounded=bounded.astype(jnp.int32)[None],
                           w_out=w_out[l].astype(BF16)))
    dft_m = _dft_matrix(seq_p)
    return (_trunk(x_prompt, layers, dft_m), _trunk(x_sample, layers, dft_m))
```

```python
import functools
import math

import jax
import jax.numpy as jnp
from jax.experimental import pallas as pl
from jax.experimental.pallas import tpu as pltpu

D_MODEL = 1024
HEAD_DIM = 64
N_Q_HEADS = 8
N_KV_HEADS = 2
ATTN_WIDTH = N_Q_HEADS * HEAD_DIM
KV_WIDTH = N_KV_HEADS * HEAD_DIM
FOURIER_WIDTH = D_MODEL - ATTN_WIDTH
N_GROUPS = 4
GROUP_DIM = FOURIER_WIDTH // N_GROUPS
IN_WIDTH = 2 * ATTN_WIDTH + 2 * KV_WIDTH + 2 * FOURIER_WIDTH
Q_OFF = 0
K_OFF = ATTN_WIDTH
V_OFF = K_OFF + KV_WIDTH
GA_OFF = V_OFF + KV_WIDTH
U_OFF = GA_OFF + ATTN_WIDTH
GF_OFF = U_OFF + FOURIER_WIDTH
BLOCK = 128
ROPE_THETA = 10000.0
EPS = 1e-6
NEG = -1e30
LANES = 128
RADIX = 8
DFT_COLS = 256
DFT_BLOCK_ROWS = 128
LOG2E = math.log2(math.e)
MAX_UNSHIFTED_LOGIT2 = 100.0

ROW_TILE = 1024
FUSED_ROW_TILE = 1024
FUSED_SUB_ROWS = 512
ATTN_TILE = 1024
VMEM_LIMIT = 48 * 1024 * 1024
FUSED_VMEM_LIMIT = 54 * 1024 * 1024

BF16 = jnp.bfloat16
F32 = jnp.float32


def _compiler_params(grid_rank, vmem_limit=VMEM_LIMIT):
    return pltpu.CompilerParams(dimension_semantics=("arbitrary",) * grid_rank, vmem_limit_bytes=vmem_limit)


def _dot(a, b):
    return jnp.dot(a, b, preferred_element_type=F32)


def _dot_narrow(a, b):
    half = a.shape[0] // 2
    return jnp.concatenate([_dot(a[:half], b), _dot(a[half:], b)], axis=0)


def _dot_nt(a, b):
    return jax.lax.dot_general(a, b, (((1,), (1,)), ((), ())), preferred_element_type=F32)


def _silu(x):
    return x / (1.0 + jnp.exp(-x))


def _pack_rows(x):
    return pltpu.bitcast(x.astype(BF16), jnp.uint32)


def _unpack_rows(bits):
    return pltpu.bitcast(bits, BF16)


def _half_rows(rows):
    return rows if rows == slice(None) else slice(rows.start // 2, rows.stop // 2)


def _rotate_half(t, first_half):
    width = t.shape[-1]
    fwd = pltpu.roll(t, HEAD_DIM // 2, axis=1)
    bwd = pltpu.roll(t, width - HEAD_DIM // 2, axis=1)
    return jnp.where(first_half, bwd, fwd)


def _fourier_weight_kernel(cs_ref, w_ref, o_ref):
    for g in range(N_GROUPS):
        w = w_ref[g]
        c = jnp.dot(cs_ref[0], w, preferred_element_type=F32, precision=jax.lax.Precision.HIGHEST)
        s = jnp.dot(cs_ref[1], w, preferred_element_type=F32, precision=jax.lax.Precision.HIGHEST)
        o_ref[g] = jnp.concatenate([c, s], axis=1).astype(o_ref.dtype)


def _fourier_weights(w_four, seq_len):
    idx = (jnp.arange(GROUP_DIM, dtype=jnp.int32)[:, None] * jnp.arange(GROUP_DIM, dtype=jnp.int32)[None, :]) % GROUP_DIM
    ang = idx.astype(F32) * (2.0 * math.pi / GROUP_DIM)
    scale = 1.0 / math.sqrt(seq_len * GROUP_DIM)
    cs = jnp.stack([jnp.cos(ang), jnp.sin(ang)]) * scale
    return pl.pallas_call(
        _fourier_weight_kernel,
        out_shape=jax.ShapeDtypeStruct((N_GROUPS, GROUP_DIM, 2 * GROUP_DIM), BF16),
        name="fourier_weight_prep",
    )(cs, w_four)


def _in_proj_rows(x, rows, gn_ref, w_ref, cos_ref, sin_ref, gains_ref, hs_ref, wab_ref,
                  q_ref, ka_ref, kb_ref, va_ref, vb_ref, sga_ref, a_ref, b_ref, sgf_ref):
    sub_rows = x.shape[0]
    xb = (x * gn_ref[...]).astype(BF16)
    cos = cos_ref[rows, :]
    sin = sin_ref[rows, :]
    qcos, qsin = cos * gains_ref[0:1, :], sin * gains_ref[1:2, :]
    kcos, ksin = cos * gains_ref[2:3, :], sin * gains_ref[3:4, :]
    ms = jnp.mean(x * x, axis=-1, keepdims=True)
    r = jnp.broadcast_to(jax.lax.rsqrt(ms + EPS), (sub_rows, LANES))
    eps_z = jnp.broadcast_to(EPS * (ms + EPS), (sub_rows, LANES))
    wide = lambda t, width: jnp.concatenate([t] * (width // LANES), axis=1)

    lane = jax.lax.broadcasted_iota(jnp.int32, (1, LANES), 1)
    first_half_pair = (lane % HEAD_DIM) < (HEAD_DIM // 2)
    low_head = lane < HEAD_DIM
    hs = hs_ref[...]

    def head_mean_sq(z):
        return _dot_narrow((z * z).astype(BF16), hs)

    def silu_of_scaled(z, width):
        h = z * wide(0.5 * r, width)
        return h + h * jnp.tanh(h)

    zq = _dot(xb, w_ref[:, Q_OFF:Q_OFF + ATTN_WIDTH])
    half = ATTN_WIDTH // 2
    ssq = jnp.concatenate([head_mean_sq(zq[:, :half]), head_mean_sq(zq[:, half:])], axis=1)
    qr = (zq * wide(qcos, ATTN_WIDTH)
          + _rotate_half(zq, wide(first_half_pair, ATTN_WIDTH)) * wide(qsin, ATTN_WIDTH))
    q_ref[rows, :] = (qr * jax.lax.rsqrt(ssq + wide(eps_z, ATTN_WIDTH))).astype(q_ref.dtype)

    zkv = _dot_narrow(xb, w_ref[:, K_OFF:K_OFF + 2 * KV_WIDTH])
    zk = zkv[:, :KV_WIDTH]
    ssk = head_mean_sq(zkv)[:, :KV_WIDTH]
    kr = (zk * kcos + _rotate_half(zk, first_half_pair) * ksin) * jax.lax.rsqrt(ssk + eps_z)
    kr_sw = pltpu.roll(kr, HEAD_DIM, axis=1)
    ka_ref[rows, :] = jnp.where(low_head, kr, kr_sw).astype(ka_ref.dtype)
    kb_ref[rows, :] = jnp.where(low_head, kr_sw, kr).astype(kb_ref.dtype)
    zv = zkv[:, KV_WIDTH:] * r
    zv_sw = pltpu.roll(zv, HEAD_DIM, axis=1)
    va_ref[rows, :] = jnp.where(low_head, zv, zv_sw).astype(va_ref.dtype)
    vb_ref[rows, :] = jnp.where(low_head, zv_sw, zv).astype(vb_ref.dtype)

    zga = _dot(xb, w_ref[:, GA_OFF:GA_OFF + ATTN_WIDTH])
    sga_ref[rows, :] = silu_of_scaled(zga, ATTN_WIDTH).astype(sga_ref.dtype)
    zgf = _dot(xb, w_ref[:, GF_OFF:GF_OFF + FOURIER_WIDTH])
    sgf_ref[rows, :] = silu_of_scaled(zgf, FOURIER_WIDTH).astype(sgf_ref.dtype)

    zu = (_dot(xb, w_ref[:, U_OFF:U_OFF + FOURIER_WIDTH]) * wide(r, FOURIER_WIDTH)).astype(BF16)
    for g in range(N_GROUPS):
        ab = _dot_narrow(zu[:, g * GROUP_DIM:(g + 1) * GROUP_DIM], wab_ref[g])
        half_idx, col = divmod(g * GROUP_DIM, DFT_COLS)
        a_ref[half_idx, _half_rows(rows), col:col + GROUP_DIM] = _pack_rows(ab[:, :GROUP_DIM])
        b_ref[half_idx, _half_rows(rows), col:col + GROUP_DIM] = _pack_rows(ab[:, GROUP_DIM:])


N_IN_PROJ_INPUTS = 7


def _in_proj_kernel(x_ref, *refs):
    _in_proj_rows(x_ref[...], slice(None), *refs)


def _row_spec(tm, width):
    return pl.BlockSpec((tm, width), lambda i: (i, 0))


def _split_spec(tm, width=DFT_COLS):
    return pl.BlockSpec((FOURIER_WIDTH // width, tm // 2, width), lambda i: (0, i, 0))


def _resident_spec(shape):
    return pl.BlockSpec(shape, lambda i: (0,) * len(shape), pipeline_mode=pl.Buffered(1))


def _in_proj_specs(tm, rows, seq_len):
    steps_per_seq = seq_len // tm
    tab_spec = pl.BlockSpec((tm, LANES), lambda i: (i % steps_per_seq, 0))
    in_specs = [_resident_spec((1, D_MODEL)), _resident_spec((D_MODEL, IN_WIDTH)), tab_spec, tab_spec,
                _resident_spec((4, LANES)), _resident_spec((2 * LANES, 2 * LANES)),
                _resident_spec((N_GROUPS, GROUP_DIM, 2 * GROUP_DIM))]
    row_widths = (ATTN_WIDTH, LANES, LANES, LANES, LANES, ATTN_WIDTH)
    out_specs = ([_row_spec(tm, w) for w in row_widths] + [_split_spec(tm)] * 2
                 + [_row_spec(tm, FOURIER_WIDTH)])
    out_shape = ([jax.ShapeDtypeStruct((rows, w), BF16) for w in row_widths]
                 + [jax.ShapeDtypeStruct((FOURIER_WIDTH // DFT_COLS, rows // 2, DFT_COLS), jnp.uint32)] * 2
                 + [jax.ShapeDtypeStruct((rows, FOURIER_WIDTH), BF16)])
    return in_specs, out_specs, out_shape


def _in_proj(x2d, seq_len, in_w):
    rows = x2d.shape[0]
    tm = ROW_TILE
    in_specs, out_specs, out_shape = _in_proj_specs(tm, rows, seq_len)
    return pl.pallas_call(
        _in_proj_kernel,
        grid=(rows // tm,),
        in_specs=[_row_spec(tm, D_MODEL)] + in_specs,
        out_specs=out_specs,
        out_shape=out_shape,
        compiler_params=_compiler_params(1),
        name="in_proj",
    )(x2d, *in_w)


def _attn_kernel(sink_ref, bounded_ref, q_ref, ka_ref, kb_ref, va_ref, vb_ref, sg_ref, o_ref, *, n_blocks):
    refs = (sink_ref, q_ref, ka_ref, kb_ref, va_ref, vb_ref, sg_ref, o_ref)

    @pl.when(bounded_ref[0] == 1)
    def _():
        _attn_body(*refs, n_blocks=n_blocks, shift=False)

    @pl.when(bounded_ref[0] == 0)
    def _():
        _attn_body(*refs, n_blocks=n_blocks, shift=True)


def _attn_body(sink_ref, q_ref, ka_ref, kb_ref, va_ref, vb_ref, sg_ref, o_ref, *, n_blocks, shift):
    i = pl.program_id(1)
    blocks_per_step = ATTN_TILE // BLOCK
    row = jax.lax.broadcasted_iota(jnp.int32, (BLOCK, 2 * BLOCK), 0)
    col = jax.lax.broadcasted_iota(jnp.int32, (BLOCK, 2 * BLOCK), 1) % BLOCK
    lane = jax.lax.broadcasted_iota(jnp.int32, (1, LANES), 1)
    low = lane < HEAD_DIM
    zero = jnp.zeros((), BF16)
    lane_full = jax.lax.broadcasted_iota(jnp.int32, (BLOCK, LANES), 1)
    ones_low = jnp.where(lane_full < HEAD_DIM, 1.0, 0.0).astype(BF16)
    ones_high = jnp.where(lane_full < HEAD_DIM, 0.0, 1.0).astype(BF16)

    for jb in range(blocks_per_step):
        ib = i * blocks_per_step + jb
        rows = slice(jb * BLOCK, (jb + 1) * BLOCK)
        edge_prev = jnp.where(ib == 0, NEG, 0.0)
        edge_next = jnp.where(ib == n_blocks - 1, NEG, 0.0)
        bias_prev = jnp.where(col >= row, 0.0, NEG) + edge_prev
        bias_next = jnp.where(col <= row, 0.0, NEG) + edge_next
        starts = [pl.multiple_of(jnp.clip(ib + c, 0, n_blocks - 1) * BLOCK, BLOCK) for c in (-1, 0, 1)]
        for kvh, (k_ref, v_ref) in enumerate(((ka_ref, va_ref), (kb_ref, vb_ref))):
            kbd, vbd = [], []
            for st in starts:
                kblk = k_ref[0, pl.ds(st, BLOCK), :]
                vblk = v_ref[0, pl.ds(st, BLOCK), :]
                kbd.append(jnp.concatenate([jnp.where(low, kblk, zero), jnp.where(low, zero, kblk)], axis=0))
                vbd.append(jnp.concatenate([
                    jnp.concatenate([jnp.where(low, vblk, zero), ones_low], axis=1),
                    jnp.concatenate([jnp.where(low, zero, vblk), ones_high], axis=1)], axis=0))
            vbd = jnp.concatenate(vbd, axis=0)
            for pair in range(2):
                pidx = kvh * 2 + pair
                lanes = slice(pidx * LANES, (pidx + 1) * LANES)
                qp = q_ref[0, rows, lanes]
                scores = [_dot_nt(qp, kb) for kb in kbd]
                sink_e = sink_ref[2 * pidx]
                sink_o = sink_ref[2 * pidx + 1]
                if shift:
                    scores = [scores[0] + bias_prev, scores[1], scores[2] + bias_next]
                    smax = jnp.maximum(jnp.maximum(scores[0], scores[1]), scores[2])
                    m_e = jnp.maximum(jnp.max(smax[:, :BLOCK], axis=-1, keepdims=True), sink_e)
                    m_o = jnp.maximum(jnp.max(smax[:, BLOCK:], axis=-1, keepdims=True), sink_o)
                    m_both = jnp.concatenate([jnp.broadcast_to(m_e, (BLOCK, BLOCK)),
                                              jnp.broadcast_to(m_o, (BLOCK, BLOCK))], axis=1)
                    probs = [jnp.exp2(s - m_both).astype(BF16) for s in scores]
                    p_sink = jnp.where(low, jnp.exp2(sink_e - m_e), jnp.exp2(sink_o - m_o))
                else:
                    scores = [s.astype(BF16) for s in scores]
                    scores = [scores[0] + bias_prev.astype(BF16), scores[1], scores[2] + bias_next.astype(BF16)]
                    probs = [jnp.exp2(s) for s in scores]
                    p_sink = jnp.exp2(jnp.where(low, sink_e, sink_o))
                acc = _dot(jnp.concatenate(probs, axis=1), vbd)
                out = acc[:, :LANES] / (acc[:, LANES:] + p_sink) * sg_ref[0, rows, lanes].astype(F32)
                o_ref[0, rows, lanes] = out.astype(o_ref.dtype)


def _attn_out_kernel(sink_ref, bounded_ref, q_ref, ka_ref, kb_ref, va_ref, vb_ref, sg_ref,
                     x_ref, f_ref, sgf_ref, w_ref, y_ref, ma_ref, *, n_blocks):
    _attn_kernel(sink_ref, bounded_ref, q_ref, ka_ref, kb_ref, va_ref, vb_ref, sg_ref, ma_ref, n_blocks=n_blocks)
    f = jnp.concatenate([_unpack_rows(f_ref[s, 0]) for s in range(FOURIER_WIDTH // LANES)], axis=1)
    y_ref[0] =(x_ref[0] + _dot(ma_ref[0], w_ref[:ATTN_WIDTH, :])
                + _dot(f * sgf_ref[0], w_ref[ATTN_WIDTH:, :]))


def _attention_out_proj(sink2, bounded, q, ka, kb, va, vb, sga, x, f, sgf, w_bf):
    bsz, seq_len, _ = q.shape
    tq = ATTN_TILE
    tile = lambda width: pl.BlockSpec((1, tq, width), lambda b, i, s, fl: (b, i, 0))
    kv_spec = pl.BlockSpec((1, seq_len, LANES), lambda b, i, s, fl: (b, 0, 0))
    f_spec = pl.BlockSpec((FOURIER_WIDTH // LANES, 1, tq // 2, LANES), lambda b, i, s, fl: (0, b, i, 0))
    w_spec = pl.BlockSpec((D_MODEL, D_MODEL), lambda b, i, s, fl: (0, 0), pipeline_mode=pl.Buffered(1))
    grid_spec = pltpu.PrefetchScalarGridSpec(
        num_scalar_prefetch=2,
        grid=(bsz, seq_len // tq),
        in_specs=[tile(ATTN_WIDTH), kv_spec, kv_spec, kv_spec, kv_spec, tile(ATTN_WIDTH),
                  tile(D_MODEL), f_spec, tile(FOURIER_WIDTH), w_spec],
        out_specs=tile(D_MODEL),
        scratch_shapes=[pltpu.VMEM((1, tq, ATTN_WIDTH), BF16)],
    )
    return pl.pallas_call(
        functools.partial(_attn_out_kernel, n_blocks=seq_len // BLOCK),
        grid_spec=grid_spec,
        out_shape=jax.ShapeDtypeStruct((bsz, seq_len, D_MODEL), F32),
        compiler_params=_compiler_params(2),
        name="attention_out_proj",
    )(sink2, bounded, q, ka, kb, va, vb, sga, x, f, sgf, w_bf)


def _attention(sink2, bounded, q, ka, kb, va, vb, sga):
    bsz, seq_len, _ = q.shape
    tq = ATTN_TILE
    q_spec = pl.BlockSpec((1, tq, ATTN_WIDTH), lambda b, i, s, f: (b, i, 0))
    kv_spec = pl.BlockSpec((1, seq_len, LANES), lambda b, i, s, f: (b, 0, 0))
    grid_spec = pltpu.PrefetchScalarGridSpec(
        num_scalar_prefetch=2,
        grid=(bsz, seq_len // tq),
        in_specs=[q_spec, kv_spec, kv_spec, kv_spec, kv_spec, q_spec],
        out_specs=q_spec,
    )
    return pl.pallas_call(
        functools.partial(_attn_kernel, n_blocks=seq_len // BLOCK),
        grid_spec=grid_spec,
        out_shape=jax.ShapeDtypeStruct((bsz, seq_len, ATTN_WIDTH), BF16),
        compiler_params=_compiler_params(2),
        name="band_attention",
    )(sink2, bounded, q, ka, kb, va, vb, sga)


def _seq_dft_kernel(a_ref, b_ref, m_ref, o_ref, zz_ref, pair_ref, *, chunk):
    half = RADIX // 2
    root_half = math.sqrt(0.5)
    cadd = lambda u, v: (u[0] + v[0], u[1] + v[1])
    csub = lambda u, v: (u[0] - v[0], u[1] - v[1])
    add_i = lambda u, v: (u[0] - v[1], u[1] + v[0])
    sub_i = lambda u, v: (u[0] + v[1], u[1] - v[0])

    def four_point(w):
        s02, d02, s13, d13 = cadd(w[0], w[2]), csub(w[0], w[2]), cadd(w[1], w[3]), csub(w[1], w[3])
        return [cadd(s02, s13), add_i(d02, d13), csub(s02, s13), sub_i(d02, d13)]

    def butterflies(parity):
        for rb in range(chunk // DFT_BLOCK_ROWS):
            for cb in range(DFT_COLS // LANES):
                rows = lambda q: slice(q * chunk + rb * DFT_BLOCK_ROWS, q * chunk + (rb + 1) * DFT_BLOCK_ROWS)
                cols = slice(cb * LANES, (cb + 1) * LANES)
                z = [(_unpack_rows(a_ref[0, 0, _half_rows(rows(q)), cols]),
                      _unpack_rows(b_ref[0, 0, _half_rows(rows(q)), cols])) for q in range(RADIX)]
                if parity == 0:
                    w = [cadd(z[q], z[q + half]) for q in range(half)]
                else:
                    d = [csub(z[q], z[q + half]) for q in range(half)]
                    w = [d[0],
                         ((d[1][0] - d[1][1]) * root_half, (d[1][0] + d[1][1]) * root_half),
                         (-d[2][1], d[2][0]),
                         ((-d[3][0] - d[3][1]) * root_half, (d[3][0] - d[3][1]) * root_half)]
                for t, (re, im) in enumerate(four_point(w)):
                    zz_ref[2 * t + parity, rows(0), cols] = re
                    zz_ref[2 * t + parity, rows(1), cols] = im

    def project(r):
        f = _dot(m_ref[r], zz_ref[r])
        for cb in range(DFT_COLS // LANES):
            pair_ref[r // 2, cb, pl.ds(r % 2, chunk, stride=2), :] = f[:, cb * LANES:(cb + 1) * LANES]

    butterflies(0)
    for t in range(half):
        project(2 * t)
    butterflies(1)
    for t in range(half):
        project(2 * t + 1)
        for cb in range(DFT_COLS // LANES):
            o_ref[cb, 0, pl.ds(t, chunk, stride=half), :] = _pack_rows(pair_ref[t, cb])


def _dft_matrix(seq_len):
    chunk = seq_len // RADIX
    split = 32
    period = seq_len // (RADIX * split)
    col = jnp.arange(2 * chunk, dtype=jnp.int32)[None, :]
    m = col % chunk
    quarter_turn = jnp.where(col >= chunk, 0.5 * math.pi, 0.0)
    ang_hi = ((jnp.arange(chunk // split, dtype=jnp.int32)[:, None] * m) % period).astype(F32) * (2.0 * math.pi / period)
    ang_lo = (((jnp.arange(RADIX * split, dtype=jnp.int32)[:, None] * m) % seq_len).astype(F32)
              * (2.0 * math.pi / seq_len) + quarter_turn)
    c_hi, s_hi = jnp.cos(ang_hi)[None, :, None, :], jnp.sin(ang_hi)[None, :, None, :]
    lo = lambda t: t.reshape(split, RADIX, 2 * chunk).transpose(1, 0, 2)[:, None, :, :]
    c_lo, s_lo = lo(jnp.cos(ang_lo)), lo(jnp.sin(ang_lo))
    return (c_hi * c_lo - s_hi * s_lo).astype(BF16).reshape(RADIX, chunk, 2 * chunk)


def _seq_dft(a, b, dft_m):
    n_halves, bsz, half_len, _ = a.shape
    seq_len = 2 * half_len
    chunk = seq_len // RADIX
    slabs_per_step = DFT_COLS // LANES
    col_spec = pl.BlockSpec((1, 1, half_len, DFT_COLS), lambda bi, h: (h, bi, 0, 0))
    out_spec = pl.BlockSpec((slabs_per_step, 1, half_len, LANES), lambda bi, h: (h, bi, 0, 0))
    m_spec = pl.BlockSpec((RADIX, chunk, 2 * chunk), lambda bi, h: (0, 0, 0), pipeline_mode=pl.Buffered(1))
    return pl.pallas_call(
        functools.partial(_seq_dft_kernel, chunk=chunk),
        grid=(bsz, n_halves),
        in_specs=[col_spec, col_spec, m_spec],
        out_specs=out_spec,
        out_shape=jax.ShapeDtypeStruct((n_halves * slabs_per_step, bsz, half_len, LANES), jnp.uint32),
        scratch_shapes=[pltpu.VMEM((RADIX, 2 * chunk, DFT_COLS), BF16),
                        pltpu.VMEM((RADIX // 2, slabs_per_step, 2 * chunk, LANES), F32)],
        compiler_params=_compiler_params(2),
        name="seq_dft",
    )(a, b, dft_m)


def _out_proj_rows(rows, x_ref, ma_ref, f_ref, sgf_ref, w_ref, o_ref):
    f = jnp.concatenate([_unpack_rows(f_ref[s, _half_rows(rows), :])
                         for s in range(FOURIER_WIDTH // LANES)], axis=1)
    mf = f * sgf_ref[rows, :]
    o_ref[rows, :] = (x_ref[rows, :] + _dot(ma_ref[rows, :], w_ref[:ATTN_WIDTH, :])
                      + _dot(mf, w_ref[ATTN_WIDTH:, :]))


def _out_proj_kernel(x_ref, ma_ref, f_ref, sgf_ref, w_ref, o_ref):
    _out_proj_rows(slice(None), x_ref, ma_ref, f_ref, sgf_ref, w_ref, o_ref)


N_OUT_PROJ_INPUTS = 5


def _out_in_proj_kernel(*refs):
    out_in = refs[:N_OUT_PROJ_INPUTS]
    in_refs = refs[N_OUT_PROJ_INPUTS:N_OUT_PROJ_INPUTS + N_IN_PROJ_INPUTS]
    y_ref = refs[N_OUT_PROJ_INPUTS + N_IN_PROJ_INPUTS]
    out_refs = refs[N_OUT_PROJ_INPUTS + N_IN_PROJ_INPUTS + 1:]
    for sub in range(y_ref.shape[0] // FUSED_SUB_ROWS):
        rows = slice(sub * FUSED_SUB_ROWS, (sub + 1) * FUSED_SUB_ROWS)
        _out_proj_rows(rows, *out_in, y_ref)
        _in_proj_rows(y_ref[rows, :], rows, *in_refs, *out_refs)


def _out_proj_specs(tm):
    return [_row_spec(tm, D_MODEL), _row_spec(tm, ATTN_WIDTH), _split_spec(tm, LANES),
            _row_spec(tm, FOURIER_WIDTH), _resident_spec((D_MODEL, D_MODEL))]


def _out_proj(x2d, ma, f, sgf, w_bf):
    rows = x2d.shape[0]
    tm = ROW_TILE
    return pl.pallas_call(
        _out_proj_kernel,
        grid=(rows // tm,),
        in_specs=_out_proj_specs(tm),
        out_specs=_row_spec(tm, D_MODEL),
        out_shape=jax.ShapeDtypeStruct((rows, D_MODEL), F32),
        compiler_params=_compiler_params(1),
        name="out_proj",
    )(x2d, ma, f, sgf, w_bf)


def _out_in_proj(x2d, ma, f, sgf, w_bf, seq_len, in_w):
    rows = x2d.shape[0]
    tm = FUSED_ROW_TILE
    i_in_specs, i_out_specs, i_out_shape = _in_proj_specs(tm, rows, seq_len)
    outs = pl.pallas_call(
        _out_in_proj_kernel,
        grid=(rows // tm,),
        in_specs=_out_proj_specs(tm) + i_in_specs,
        out_specs=[_row_spec(tm, D_MODEL)] + i_out_specs,
        out_shape=[jax.ShapeDtypeStruct((rows, D_MODEL), F32)] + i_out_shape,
        compiler_params=_compiler_params(1, FUSED_VMEM_LIMIT),
        name="out_in_proj",
    )(x2d, ma, f, sgf, w_bf, *in_w)
    return outs[0], outs[1:]


def _rope_tables(seq_len):
    half = HEAD_DIM // 2
    lane = jnp.arange(LANES, dtype=jnp.int32)
    inv_freq = 1.0 / (ROPE_THETA ** ((lane % half).astype(F32) / half))
    ang = jnp.arange(seq_len, dtype=F32)[:, None] * inv_freq[None, :]
    sign = jnp.where((lane % HEAD_DIM) < half, -1.0, 1.0)
    return jnp.cos(ang), jnp.sin(ang) * sign[None, :]


def _rope_gains(q_gain, k_gain):
    pair = lambda g: jnp.tile(g, LANES // HEAD_DIM)
    rot = lambda g: jnp.roll(g, HEAD_DIM // 2)
    gq = q_gain.astype(F32) * (HEAD_DIM ** -0.5 * LOG2E)
    gk = k_gain.astype(F32)
    return jnp.stack([pair(gq), pair(rot(gq)), pair(gk), pair(rot(gk))])


def _head_mean_matrix(width):
    head = jnp.arange(width, dtype=jnp.int32) // HEAD_DIM
    return jnp.where(head[:, None] == head[None, :], 1.0 / HEAD_DIM, 0.0).astype(BF16)


def _trunk(x, layers, dft_m):
    bsz, seq_len, _ = x.shape
    x2d = x.reshape(bsz * seq_len, D_MODEL)
    r3 = lambda t: t.reshape(bsz, seq_len, t.shape[-1])
    split4 = lambda t: t.reshape(t.shape[0], bsz, seq_len // 2, DFT_COLS)
    proj = _in_proj(x2d, seq_len, layers[0]["in_w"])
    for l, layer in enumerate(layers):
        q, ka, kb, va, vb, sga, a, b, sgf = proj
        f = _seq_dft(split4(a), split4(b), dft_m)
        attn_args = (layer["sink2"], layer["bounded"], r3(q), r3(ka), r3(kb), r3(va), r3(vb), r3(sga))
        if l + 1 < len(layers):
            ma = _attention(*attn_args).reshape(bsz * seq_len, ATTN_WIDTH)
            x2d, proj = _out_in_proj(x2d, ma, f.reshape(f.shape[0], bsz * seq_len // 2, LANES), sgf,
                                     layer["w_out"], seq_len, layers[l + 1]["in_w"])
        else:
            return _attention_out_proj(*attn_args, r3(x2d), f, r3(sgf), layer["w_out"])


def kernel(x_prompt, x_sample, norm_gain, w_in, q_norm_gain, k_norm_gain, sink_logit, w_fourier, w_out):
    depth = norm_gain.shape[0]
    seq_p, seq_s = x_prompt.shape[1], x_sample.shape[1]
    assert seq_p == seq_s
    cos_t, sin_t = _rope_tables(seq_p)
    hs = _head_mean_matrix(2 * LANES)
    layers = []
    for l in range(depth):
        sink2 = sink_logit[l].astype(F32) * LOG2E
        logit_bound = (HEAD_DIM ** 0.5 * LOG2E) * jnp.max(jnp.abs(q_norm_gain[l])) * jnp.max(jnp.abs(k_norm_gain[l]))
        bounded = jnp.maximum(logit_bound, jnp.max(jnp.abs(sink2))) <= MAX_UNSHIFTED_LOGIT2
        in_w = (norm_gain[l].astype(F32)[None, :], w_in[l].astype(BF16), cos_t, sin_t,
                _rope_gains(q_norm_gain[l], k_norm_gain[l]), hs, _fourier_weights(w_fourier[l], seq_p))
        layers.append(dict(in_w=in_w, sink2=sink2, bounded=bounded.astype(jnp.int32)[None],
                           w_out=w_out[l].astype(BF16)))
    dft_m = _dft_matrix(seq_p)
    return (_trunk(x_prompt, layers, dft_m), _trunk(x_sample, layers, dft_m))
```

```python
import functools
import math

import jax
import jax.numpy as jnp
from jax.experimental import pallas as pl
from jax.experimental.pallas import tpu as pltpu

D_MODEL = 1024
HEAD_DIM = 64
N_Q_HEADS = 8
N_KV_HEADS = 2
ATTN_WIDTH = N_Q_HEADS * HEAD_DIM
KV_WIDTH = N_KV_HEADS * HEAD_DIM
FOURIER_WIDTH = D_MODEL - ATTN_WIDTH
N_GROUPS = 4
GROUP_DIM = FOURIER_WIDTH // N_GROUPS
IN_WIDTH = 2 * ATTN_WIDTH + 2 * KV_WIDTH + 2 * FOURIER_WIDTH
Q_OFF = 0
K_OFF = ATTN_WIDTH
V_OFF = K_OFF + KV_WIDTH
GA_OFF = V_OFF + KV_WIDTH
U_OFF = GA_OFF + ATTN_WIDTH
GF_OFF = U_OFF + FOURIER_WIDTH
BLOCK = 128
ROPE_THETA = 10000.0
EPS = 1e-6
NEG = -1e30
LANES = 128
RADIX = 8
DFT_COLS = 256
DFT_BLOCK_ROWS = 128
LOG2E = math.log2(math.e)
MAX_UNSHIFTED_LOGIT2 = 100.0

ROW_TILE = 1024
FUSED_ROW_TILE = 1024
FUSED_SUB_ROWS = 512
ATTN_TILE = 1024
VMEM_LIMIT = 48 * 1024 * 1024
FUSED_VMEM_LIMIT = 54 * 1024 * 1024

BF16 = jnp.bfloat16
F32 = jnp.float32


def _compiler_params(grid_rank, vmem_limit=VMEM_LIMIT):
    return pltpu.CompilerParams(dimension_semantics=("arbitrary",) * grid_rank, vmem_limit_bytes=vmem_limit)


def _dot(a, b):
    return jnp.dot(a, b, preferred_element_type=F32)


def _dot_narrow(a, b):
    half = a.shape[0] // 2
    return jnp.concatenate([_dot(a[:half], b), _dot(a[half:], b)], axis=0)


def _dot_nt(a, b):
    return jax.lax.dot_general(a, b, (((1,), (1,)), ((), ())), preferred_element_type=F32)


def _silu(x):
    return x / (1.0 + jnp.exp(-x))


def _pack_rows(x):
    return pltpu.bitcast(x.astype(BF16), jnp.uint32)


def _unpack_rows(bits):
    return pltpu.bitcast(bits, BF16)


def _half_rows(rows):
    return rows if rows == slice(None) else slice(rows.start // 2, rows.stop // 2)


def _rotate_half(t, first_half):
    width = t.shape[-1]
    fwd = pltpu.roll(t, HEAD_DIM // 2, axis=1)
    bwd = pltpu.roll(t, width - HEAD_DIM // 2, axis=1)
    return jnp.where(first_half, bwd, fwd)


def _fourier_weight_kernel(cs_ref, w_ref, o_ref):
    for g in range(N_GROUPS):
        w = w_ref[g]
        c = jnp.dot(cs_ref[0], w, preferred_element_type=F32, precision=jax.lax.Precision.HIGHEST)
        s = jnp.dot(cs_ref[1], w, preferred_element_type=F32, precision=jax.lax.Precision.HIGHEST)
        o_ref[g] = jnp.concatenate([c, s], axis=1).astype(o_ref.dtype)


def _fourier_weights(w_four, seq_len):
    idx = (jnp.arange(GROUP_DIM, dtype=jnp.int32)[:, None] * jnp.arange(GROUP_DIM, dtype=jnp.int32)[None, :]) % GROUP_DIM
    ang = idx.astype(F32) * (2.0 * math.pi / GROUP_DIM)
    scale = 1.0 / math.sqrt(seq_len * GROUP_DIM)
    cs = jnp.stack([jnp.cos(ang), jnp.sin(ang)]) * scale
    return pl.pallas_call(
        _fourier_weight_kernel,
        out_shape=jax.ShapeDtypeStruct((N_GROUPS, GROUP_DIM, 2 * GROUP_DIM), BF16),
        name="fourier_weight_prep",
    )(cs, w_four)


def _in_proj_rows(x, rows, gn_ref, w_ref, cos_ref, sin_ref, gains_ref, hs_ref, wab_ref,
                  q_ref, ka_ref, kb_ref, va_ref, vb_ref, sga_ref, a_ref, b_ref, sgf_ref):
    sub_rows = x.shape[0]
    xb = (x * gn_ref[...]).astype(BF16)
    cos = cos_ref[rows, :]
    sin = sin_ref[rows, :]
    qcos, qsin = cos * gains_ref[0:1, :], sin * gains_ref[1:2, :]
    kcos, ksin = cos * gains_ref[2:3, :], sin * gains_ref[3:4, :]
    ms = jnp.mean(x * x, axis=-1, keepdims=True)
    r = jnp.broadcast_to(jax.lax.rsqrt(ms + EPS), (sub_rows, LANES))
    eps_z = jnp.broadcast_to(EPS * (ms + EPS), (sub_rows, LANES))
    wide = lambda t, width: jnp.concatenate([t] * (width // LANES), axis=1)

    lane = jax.lax.broadcasted_iota(jnp.int32, (1, LANES), 1)
    first_half_pair = (lane % HEAD_DIM) < (HEAD_DIM // 2)
    low_head = lane < HEAD_DIM
    hs = hs_ref[...]

    def head_mean_sq(z):
        return _dot_narrow((z * z).astype(BF16), hs)

    def silu_of_scaled(z, width):
        h = z * wide(0.5 * r, width)
        return h + h * jnp.tanh(h)

    zq = _dot(xb, w_ref[:, Q_OFF:Q_OFF + ATTN_WIDTH])
    half = ATTN_WIDTH // 2
    ssq = jnp.concatenate([head_mean_sq(zq[:, :half]), head_mean_sq(zq[:, half:])], axis=1)
    qr = (zq * wide(qcos, ATTN_WIDTH)
          + _rotate_half(zq, wide(first_half_pair, ATTN_WIDTH)) * wide(qsin, ATTN_WIDTH))
    q_ref[rows, :] = (qr * jax.lax.rsqrt(ssq + wide(eps_z, ATTN_WIDTH))).astype(q_ref.dtype)

    zkv = _dot_narrow(xb, w_ref[:, K_OFF:K_OFF + 2 * KV_WIDTH])
    zk = zkv[:, :KV_WIDTH]
    ssk = head_mean_sq(zkv)[:, :KV_WIDTH]
    kr = (zk * kcos + _rotate_half(zk, first_half_pair) * ksin) * jax.lax.rsqrt(ssk + eps_z)
    kr_sw = pltpu.roll(kr, HEAD_DIM, axis=1)
    ka_ref[rows, :] = jnp.where(low_head, kr, kr_sw).astype(ka_ref.dtype)
    kb_ref[rows, :] = jnp.where(low_head, kr_sw, kr).astype(kb_ref.dtype)
    zv = zkv[:, KV_WIDTH:] * r
    zv_sw = pltpu.roll(zv, HEAD_DIM, axis=1)
    va_ref[rows, :] = jnp.where(low_head, zv, zv_sw).astype(va_ref.dtype)
    vb_ref[rows, :] = jnp.where(low_head, zv_sw, zv).astype(vb_ref.dtype)

    zga = _dot(xb, w_ref[:, GA_OFF:GA_OFF + ATTN_WIDTH])
    sga_ref[rows, :] = silu_of_scaled(zga, ATTN_WIDTH).astype(sga_ref.dtype)
    zgf = _dot(xb, w_ref[:, GF_OFF:GF_OFF + FOURIER_WIDTH])
    sgf_ref[rows, :] = silu_of_scaled(zgf, FOURIER_WIDTH).astype(sgf_ref.dtype)

    zu = (_dot(xb, w_ref[:, U_OFF:U_OFF + FOURIER_WIDTH]) * wide(r, FOURIER_WIDTH)).astype(BF16)
    for g in range(N_GROUPS):
        ab = _dot_narrow(zu[:, g * GROUP_DIM:(g + 1) * GROUP_DIM], wab_ref[g])
        half_idx, col = divmod(g * GROUP_DIM, DFT_COLS)
        a_ref[half_idx, _half_rows(rows), col:col + GROUP_DIM] = _pack_rows(ab[:, :GROUP_DIM])
        b_ref[half_idx, _half_rows(rows), col:col + GROUP_DIM] = _pack_rows(ab[:, GROUP_DIM:])


N_IN_PROJ_INPUTS = 7


def _in_proj_kernel(x_ref, *refs):
    _in_proj_rows(x_ref[...], slice(None), *refs)


def _row_spec(tm, width):
    return pl.BlockSpec((tm, width), lambda i: (i, 0))


def _split_spec(tm, width=DFT_COLS):
    return pl.BlockSpec((FOURIER_WIDTH // width, tm // 2, width), lambda i: (0, i, 0))


def _resident_spec(shape):
    return pl.BlockSpec(shape, lambda i: (0,) * len(shape), pipeline_mode=pl.Buffered(1))


def _layer_weight_spec(layer, rows, cols):
    return pl.BlockSpec((None, rows, cols), lambda i: (layer, 0, 0), pipeline_mode=pl.Buffered(1))


def _in_proj_specs(tm, rows, seq_len, layer):
    steps_per_seq = seq_len // tm
    tab_spec = pl.BlockSpec((tm, LANES), lambda i: (i % steps_per_seq, 0))
    in_specs = [_resident_spec((1, D_MODEL)), _layer_weight_spec(layer, D_MODEL, IN_WIDTH), tab_spec, tab_spec,
                _resident_spec((4, LANES)), _resident_spec((2 * LANES, 2 * LANES)),
                _resident_spec((N_GROUPS, GROUP_DIM, 2 * GROUP_DIM))]
    row_widths = (ATTN_WIDTH, LANES, LANES, LANES, LANES, ATTN_WIDTH)
    out_specs = ([_row_spec(tm, w) for w in row_widths] + [_split_spec(tm)] * 2
                 + [_row_spec(tm, FOURIER_WIDTH)])
    out_shape = ([jax.ShapeDtypeStruct((rows, w), BF16) for w in row_widths]
                 + [jax.ShapeDtypeStruct((FOURIER_WIDTH // DFT_COLS, rows // 2, DFT_COLS), jnp.uint32)] * 2
                 + [jax.ShapeDtypeStruct((rows, FOURIER_WIDTH), BF16)])
    return in_specs, out_specs, out_shape


def _in_proj(x2d, seq_len, layer, in_w):
    rows = x2d.shape[0]
    tm = ROW_TILE
    in_specs, out_specs, out_shape = _in_proj_specs(tm, rows, seq_len, layer)
    return pl.pallas_call(
        _in_proj_kernel,
        grid=(rows // tm,),
        in_specs=[_row_spec(tm, D_MODEL)] + in_specs,
        out_specs=out_specs,
        out_shape=out_shape,
        compiler_params=_compiler_params(1),
        name="in_proj",
    )(x2d, *in_w)


def _attn_kernel(sink_ref, bounded_ref, q_ref, ka_ref, kb_ref, va_ref, vb_ref, sg_ref, o_ref, *, n_blocks):
    refs = (sink_ref, q_ref, ka_ref, kb_ref, va_ref, vb_ref, sg_ref, o_ref)

    @pl.when(bounded_ref[0] == 1)
    def _():
        _attn_body(*refs, n_blocks=n_blocks, shift=False)

    @pl.when(bounded_ref[0] == 0)
    def _():
        _attn_body(*refs, n_blocks=n_blocks, shift=True)


def _attn_body(sink_ref, q_ref, ka_ref, kb_ref, va_ref, vb_ref, sg_ref, o_ref, *, n_blocks, shift):
    i = pl.program_id(1)
    blocks_per_step = ATTN_TILE // BLOCK
    row = jax.lax.broadcasted_iota(jnp.int32, (BLOCK, 2 * BLOCK), 0)
    col = jax.lax.broadcasted_iota(jnp.int32, (BLOCK, 2 * BLOCK), 1) % BLOCK
    lane = jax.lax.broadcasted_iota(jnp.int32, (1, LANES), 1)
    low = lane < HEAD_DIM
    zero = jnp.zeros((), BF16)
    lane_full = jax.lax.broadcasted_iota(jnp.int32, (BLOCK, LANES), 1)
    ones_low = jnp.where(lane_full < HEAD_DIM, 1.0, 0.0).astype(BF16)
    ones_high = jnp.where(lane_full < HEAD_DIM, 0.0, 1.0).astype(BF16)

    for jb in range(blocks_per_step):
        ib = i * blocks_per_step + jb
        rows = slice(jb * BLOCK, (jb + 1) * BLOCK)
        edge_prev = jnp.where(ib == 0, NEG, 0.0)
        edge_next = jnp.where(ib == n_blocks - 1, NEG, 0.0)
        bias_prev = jnp.where(col >= row, 0.0, NEG) + edge_prev
        bias_next = jnp.where(col <= row, 0.0, NEG) + edge_next
        starts = [pl.multiple_of(jnp.clip(ib + c, 0, n_blocks - 1) * BLOCK, BLOCK) for c in (-1, 0, 1)]
        for kvh, (k_ref, v_ref) in enumerate(((ka_ref, va_ref), (kb_ref, vb_ref))):
            kbd, vbd = [], []
            for st in starts:
                kblk = k_ref[0, pl.ds(st, BLOCK), :]
                vblk = v_ref[0, pl.ds(st, BLOCK), :]
                kbd.append(jnp.concatenate([jnp.where(low, kblk, zero), jnp.where(low, zero, kblk)], axis=0))
                vbd.append(jnp.concatenate([
                    jnp.concatenate([jnp.where(low, vblk, zero), ones_low], axis=1),
                    jnp.concatenate([jnp.where(low, zero, vblk), ones_high], axis=1)], axis=0))
            vbd = jnp.concatenate(vbd, axis=0)
            for pair in range(2):
                pidx = kvh * 2 + pair
                lanes = slice(pidx * LANES, (pidx + 1) * LANES)
                qp = q_ref[0, rows, lanes]
                scores = [_dot_nt(qp, kb) for kb in kbd]
                sink_e = sink_ref[2 * pidx]
                sink_o = sink_ref[2 * pidx + 1]
                if shift:
                    scores = [scores[0] + bias_prev, scores[1], scores[2] + bias_next]
                    smax = jnp.maximum(jnp.maximum(scores[0], scores[1]), scores[2])
                    m_e = jnp.maximum(jnp.max(smax[:, :BLOCK], axis=-1, keepdims=True), sink_e)
                    m_o = jnp.maximum(jnp.max(smax[:, BLOCK:], axis=-1, keepdims=True), sink_o)
                    m_both = jnp.concatenate([jnp.broadcast_to(m_e, (BLOCK, BLOCK)),
                                              jnp.broadcast_to(m_o, (BLOCK, BLOCK))], axis=1)
                    probs = [jnp.exp2(s - m_both).astype(BF16) for s in scores]
                    p_sink = jnp.where(low, jnp.exp2(sink_e - m_e), jnp.exp2(sink_o - m_o))
                else:
                    scores = [s.astype(BF16) for s in scores]
                    scores = [scores[0] + bias_prev.astype(BF16), scores[1], scores[2] + bias_next.astype(BF16)]
                    probs = [jnp.exp2(s) for s in scores]
                    p_sink = jnp.exp2(jnp.where(low, sink_e, sink_o))
                acc = _dot(jnp.concatenate(probs, axis=1), vbd)
                out = acc[:, :LANES] / (acc[:, LANES:] + p_sink) * sg_ref[0, rows, lanes].astype(F32)
                o_ref[0, rows, lanes] = out.astype(o_ref.dtype)


def _attn_out_kernel(sink_ref, bounded_ref, q_ref, ka_ref, kb_ref, va_ref, vb_ref, sg_ref,
                     x_ref, f_ref, sgf_ref, w_ref, y_ref, ma_ref, *, n_blocks):
    _attn_kernel(sink_ref, bounded_ref, q_ref, ka_ref, kb_ref, va_ref, vb_ref, sg_ref, ma_ref, n_blocks=n_blocks)
    f = jnp.concatenate([_unpack_rows(f_ref[s, 0]) for s in range(FOURIER_WIDTH // LANES)], axis=1)
    y_ref[0] =(x_ref[0] + _dot(ma_ref[0], w_ref[:ATTN_WIDTH, :])
                + _dot(f * sgf_ref[0], w_ref[ATTN_WIDTH:, :]))


def _attention_out_proj(sink2, bounded, q, ka, kb, va, vb, sga, x, f, sgf, layer, w_bf):
    bsz, seq_len, _ = q.shape
    tq = ATTN_TILE
    tile = lambda width: pl.BlockSpec((1, tq, width), lambda b, i, s, fl: (b, i, 0))
    kv_spec = pl.BlockSpec((1, seq_len, LANES), lambda b, i, s, fl: (b, 0, 0))
    f_spec = pl.BlockSpec((FOURIER_WIDTH // LANES, 1, tq // 2, LANES), lambda b, i, s, fl: (0, b, i, 0))
    w_spec = pl.BlockSpec((None, D_MODEL, D_MODEL), lambda b, i, s, fl: (layer, 0, 0), pipeline_mode=pl.Buffered(1))
    grid_spec = pltpu.PrefetchScalarGridSpec(
        num_scalar_prefetch=2,
        grid=(bsz, seq_len // tq),
        in_specs=[tile(ATTN_WIDTH), kv_spec, kv_spec, kv_spec, kv_spec, tile(ATTN_WIDTH),
                  tile(D_MODEL), f_spec, tile(FOURIER_WIDTH), w_spec],
        out_specs=tile(D_MODEL),
        scratch_shapes=[pltpu.VMEM((1, tq, ATTN_WIDTH), BF16)],
    )
    return pl.pallas_call(
        functools.partial(_attn_out_kernel, n_blocks=seq_len // BLOCK),
        grid_spec=grid_spec,
        out_shape=jax.ShapeDtypeStruct((bsz, seq_len, D_MODEL), F32),
        compiler_params=_compiler_params(2),
        name="attention_out_proj",
    )(sink2, bounded, q, ka, kb, va, vb, sga, x, f, sgf, w_bf)


def _attention(sink2, bounded, q, ka, kb, va, vb, sga):
    bsz, seq_len, _ = q.shape
    tq = ATTN_TILE
    q_spec = pl.BlockSpec((1, tq, ATTN_WIDTH), lambda b, i, s, f: (b, i, 0))
    kv_spec = pl.BlockSpec((1, seq_len, LANES), lambda b, i, s, f: (b, 0, 0))
    grid_spec = pltpu.PrefetchScalarGridSpec(
        num_scalar_prefetch=2,
        grid=(bsz, seq_len // tq),
        in_specs=[q_spec, kv_spec, kv_spec, kv_spec, kv_spec, q_spec],
        out_specs=q_spec,
    )
    return pl.pallas_call(
        functools.partial(_attn_kernel, n_blocks=seq_len // BLOCK),
        grid_spec=grid_spec,
        out_shape=jax.ShapeDtypeStruct((bsz, seq_len, ATTN_WIDTH), BF16),
        compiler_params=_compiler_params(2),
        name="band_attention",
    )(sink2, bounded, q, ka, kb, va, vb, sga)


def _seq_dft_kernel(a_ref, b_ref, m_ref, o_ref, zz_ref, pair_ref, *, chunk):
    half = RADIX // 2
    root_half = math.sqrt(0.5)
    cadd = lambda u, v: (u[0] + v[0], u[1] + v[1])
    csub = lambda u, v: (u[0] - v[0], u[1] - v[1])
    add_i = lambda u, v: (u[0] - v[1], u[1] + v[0])
    sub_i = lambda u, v: (u[0] + v[1], u[1] - v[0])

    def four_point(w):
        s02, d02, s13, d13 = cadd(w[0], w[2]), csub(w[0], w[2]), cadd(w[1], w[3]), csub(w[1], w[3])
        return [cadd(s02, s13), add_i(d02, d13), csub(s02, s13), sub_i(d02, d13)]

    def butterflies(parity):
        for rb in range(chunk // DFT_BLOCK_ROWS):
            for cb in range(DFT_COLS // LANES):
                rows = lambda q: slice(q * chunk + rb * DFT_BLOCK_ROWS, q * chunk + (rb + 1) * DFT_BLOCK_ROWS)
                cols = slice(cb * LANES, (cb + 1) * LANES)
                z = [(_unpack_rows(a_ref[0, 0, _half_rows(rows(q)), cols]),
                      _unpack_rows(b_ref[0, 0, _half_rows(rows(q)), cols])) for q in range(RADIX)]
                if parity == 0:
                    w = [cadd(z[q], z[q + half]) for q in range(half)]
                else:
                    d = [csub(z[q], z[q + half]) for q in range(half)]
                    w = [d[0],
                         ((d[1][0] - d[1][1]) * root_half, (d[1][0] + d[1][1]) * root_half),
                         (-d[2][1], d[2][0]),
                         ((-d[3][0] - d[3][1]) * root_half, (d[3][0] - d[3][1]) * root_half)]
                for t, (re, im) in enumerate(four_point(w)):
                    zz_ref[2 * t + parity, rows(0), cols] = re
                    zz_ref[2 * t + parity, rows(1), cols] = im

    def project(r):
        f = _dot(m_ref[r], zz_ref[r])
        for cb in range(DFT_COLS // LANES):
            pair_ref[r // 2, cb, pl.ds(r % 2, chunk, stride=2), :] = f[:, cb * LANES:(cb + 1) * LANES]

    butterflies(0)
    for t in range(half):
        project(2 * t)
    butterflies(1)
    for t in range(half):
        project(2 * t + 1)
        for cb in range(DFT_COLS // LANES):
            o_ref[cb, 0, pl.ds(t, chunk, stride=half), :] = _pack_rows(pair_ref[t, cb])


def _dft_matrix(seq_len):
    chunk = seq_len // RADIX
    split = 32
    period = seq_len // (RADIX * split)
    m = jnp.arange(chunk, dtype=jnp.int32)[None, :]
    ang_hi = ((jnp.arange(chunk // split, dtype=jnp.int32)[:, None] * m) % period).astype(F32) * (2.0 * math.pi / period)
    ang_lo = ((jnp.arange(RADIX * split, dtype=jnp.int32)[:, None] * m) % seq_len).astype(F32) * (2.0 * math.pi / seq_len)
    c_hi, s_hi = jnp.cos(ang_hi)[None, :, None, :], jnp.sin(ang_hi)[None, :, None, :]
    lo = lambda t: t.reshape(split, RADIX, chunk).transpose(1, 0, 2)[:, None, :, :]
    c_lo, s_lo = lo(jnp.cos(ang_lo)), lo(jnp.sin(ang_lo))
    cos = (c_hi * c_lo - s_hi * s_lo).reshape(RADIX, chunk, chunk)
    sin = (s_hi * c_lo + c_hi * s_lo).reshape(RADIX, chunk, chunk)
    return jnp.concatenate([cos, -sin], axis=2).astype(BF16)


def _seq_dft(a, b, dft_m):
    n_halves, bsz, half_len, _ = a.shape
    seq_len = 2 * half_len
    chunk = seq_len // RADIX
    slabs_per_step = DFT_COLS // LANES
    col_spec = pl.BlockSpec((1, 1, half_len, DFT_COLS), lambda bi, h: (h, bi, 0, 0))
    out_spec = pl.BlockSpec((slabs_per_step, 1, half_len, LANES), lambda bi, h: (h, bi, 0, 0))
    m_spec = pl.BlockSpec((RADIX, chunk, 2 * chunk), lambda bi, h: (0, 0, 0), pipeline_mode=pl.Buffered(1))
    return pl.pallas_call(
        functools.partial(_seq_dft_kernel, chunk=chunk),
        grid=(bsz, n_halves),
        in_specs=[col_spec, col_spec, m_spec],
        out_specs=out_spec,
        out_shape=jax.ShapeDtypeStruct((n_halves * slabs_per_step, bsz, half_len, LANES), jnp.uint32),
        scratch_shapes=[pltpu.VMEM((RADIX, 2 * chunk, DFT_COLS), BF16),
                        pltpu.VMEM((RADIX // 2, slabs_per_step, 2 * chunk, LANES), F32)],
        compiler_params=_compiler_params(2),
        name="seq_dft",
    )(a, b, dft_m)


def _out_proj_rows(rows, x_ref, ma_ref, f_ref, sgf_ref, w_ref, o_ref):
    f = jnp.concatenate([_unpack_rows(f_ref[s, _half_rows(rows), :])
                         for s in range(FOURIER_WIDTH // LANES)], axis=1)
    mf = f * sgf_ref[rows, :]
    o_ref[rows, :] = (x_ref[rows, :] + _dot(ma_ref[rows, :], w_ref[:ATTN_WIDTH, :])
                      + _dot(mf, w_ref[ATTN_WIDTH:, :]))


def _out_proj_kernel(x_ref, ma_ref, f_ref, sgf_ref, w_ref, o_ref):
    _out_proj_rows(slice(None), x_ref, ma_ref, f_ref, sgf_ref, w_ref, o_ref)


N_OUT_PROJ_INPUTS = 5


def _out_in_proj_kernel(*refs):
    out_in = refs[:N_OUT_PROJ_INPUTS]
    in_refs = refs[N_OUT_PROJ_INPUTS:N_OUT_PROJ_INPUTS + N_IN_PROJ_INPUTS]
    y_ref = refs[N_OUT_PROJ_INPUTS + N_IN_PROJ_INPUTS]
    out_refs = refs[N_OUT_PROJ_INPUTS + N_IN_PROJ_INPUTS + 1:]
    for sub in range(y_ref.shape[0] // FUSED_SUB_ROWS):
        rows = slice(sub * FUSED_SUB_ROWS, (sub + 1) * FUSED_SUB_ROWS)
        _out_proj_rows(rows, *out_in, y_ref)
        _in_proj_rows(y_ref[rows, :], rows, *in_refs, *out_refs)


def _out_proj_specs(tm, layer):
    return [_row_spec(tm, D_MODEL), _row_spec(tm, ATTN_WIDTH), _split_spec(tm, LANES),
            _row_spec(tm, FOURIER_WIDTH), _layer_weight_spec(layer, D_MODEL, D_MODEL)]


def _out_proj(x2d, ma, f, sgf, layer, w_bf):
    rows = x2d.shape[0]
    tm = ROW_TILE
    return pl.pallas_call(
        _out_proj_kernel,
        grid=(rows // tm,),
        in_specs=_out_proj_specs(tm, layer),
        out_specs=_row_spec(tm, D_MODEL),
        out_shape=jax.ShapeDtypeStruct((rows, D_MODEL), F32),
        compiler_params=_compiler_params(1),
        name="out_proj",
    )(x2d, ma, f, sgf, w_bf)


def _out_in_proj(x2d, ma, f, sgf, layer, w_bf, seq_len, in_w):
    rows = x2d.shape[0]
    tm = FUSED_ROW_TILE
    i_in_specs, i_out_specs, i_out_shape = _in_proj_specs(tm, rows, seq_len, layer + 1)
    outs = pl.pallas_call(
        _out_in_proj_kernel,
        grid=(rows // tm,),
        in_specs=_out_proj_specs(tm, layer) + i_in_specs,
        out_specs=[_row_spec(tm, D_MODEL)] + i_out_specs,
        out_shape=[jax.ShapeDtypeStruct((rows, D_MODEL), F32)] + i_out_shape,
        compiler_params=_compiler_params(1, FUSED_VMEM_LIMIT),
        name="out_in_proj",
    )(x2d, ma, f, sgf, w_bf, *in_w)
    return outs[0], outs[1:]


def _rope_tables(seq_len):
    half = HEAD_DIM // 2
    inv_freq = 1.0 / (ROPE_THETA ** (jnp.arange(half, dtype=F32) / half))
    ang = jnp.arange(seq_len, dtype=F32)[:, None] * inv_freq[None, :]
    cos = jnp.cos(ang)
    sin = jnp.sin(ang)
    cos_t = jnp.concatenate([cos, cos, cos, cos], axis=1)
    sin_t = jnp.concatenate([-sin, sin, -sin, sin], axis=1)
    return cos_t, sin_t


def _rope_gains(q_gain, k_gain):
    pair = lambda g: jnp.tile(g, LANES // HEAD_DIM)
    rot = lambda g: jnp.roll(g, HEAD_DIM // 2)
    gq = q_gain.astype(F32) * (HEAD_DIM ** -0.5 * LOG2E)
    gk = k_gain.astype(F32)
    return jnp.stack([pair(gq), pair(rot(gq)), pair(gk), pair(rot(gk))])


def _head_mean_matrix(width):
    head = jnp.arange(width, dtype=jnp.int32) // HEAD_DIM
    return jnp.where(head[:, None] == head[None, :], 1.0 / HEAD_DIM, 0.0).astype(BF16)


def _trunk(x, layers, dft_m):
    bsz, seq_len, _ = x.shape
    x2d = x.reshape(bsz * seq_len, D_MODEL)
    r3 = lambda t: t.reshape(bsz, seq_len, t.shape[-1])
    split4 = lambda t: t.reshape(t.shape[0], bsz, seq_len // 2, DFT_COLS)
    proj = _in_proj(x2d, seq_len, 0, layers[0]["in_w"])
    for l, layer in enumerate(layers):
        q, ka, kb, va, vb, sga, a, b, sgf = proj
        f = _seq_dft(split4(a), split4(b), dft_m)
        attn_args = (layer["sink2"], layer["bounded"], r3(q), r3(ka), r3(kb), r3(va), r3(vb), r3(sga))
        if l + 1 < len(layers):
            ma = _attention(*attn_args).reshape(bsz * seq_len, ATTN_WIDTH)
            x2d, proj = _out_in_proj(x2d, ma, f.reshape(f.shape[0], bsz * seq_len // 2, LANES), sgf,
                                     l, layer["w_out"], seq_len, layers[l + 1]["in_w"])
        else:
            return _attention_out_proj(*attn_args, r3(x2d), f, r3(sgf), l, layer["w_out"])


def kernel(x_prompt, x_sample, norm_gain, w_in, q_norm_gain, k_norm_gain, sink_logit, w_fourier, w_out):
    depth = norm_gain.shape[0]
    seq_p, seq_s = x_prompt.shape[1], x_sample.shape[1]
    assert seq_p == seq_s
    cos_t, sin_t = _rope_tables(seq_p)
    hs = _head_mean_matrix(2 * LANES)
    w_in_bf = w_in.astype(BF16)
    w_out_bf = w_out.astype(BF16)
    layers = []
    for l in range(depth):
        sink2 = sink_logit[l].astype(F32) * LOG2E
        logit_bound = (HEAD_DIM ** 0.5 * LOG2E) * jnp.max(jnp.abs(q_norm_gain[l])) * jnp.max(jnp.abs(k_norm_gain[l]))
        bounded = jnp.maximum(logit_bound, jnp.max(jnp.abs(sink2))) <= MAX_UNSHIFTED_LOGIT2
        in_w = (norm_gain[l].astype(F32)[None, :], w_in_bf, cos_t, sin_t,
                _rope_gains(q_norm_gain[l], k_norm_gain[l]), hs, _fourier_weights(w_fourier[l], seq_p))
        layers.append(dict(in_w=in_w, sink2=sink2, bounded=bounded.astype(jnp.int32)[None], w_out=w_out_bf))
    dft_m = _dft_matrix(seq_p)
    return (_trunk(x_prompt, layers, dft_m), _trunk(x_sample, layers, dft_m))
```

```python
import functools
import math

import jax
import jax.numpy as jnp
from jax.experimental import pallas as pl
from jax.experimental.pallas import tpu as pltpu

D_MODEL = 1024
HEAD_DIM = 64
N_Q_HEADS = 8
N_KV_HEADS = 2
ATTN_WIDTH = N_Q_HEADS * HEAD_DIM
KV_WIDTH = N_KV_HEADS * HEAD_DIM
FOURIER_WIDTH = D_MODEL - ATTN_WIDTH
N_GROUPS = 4
GROUP_DIM = FOURIER_WIDTH // N_GROUPS
IN_WIDTH = 2 * ATTN_WIDTH + 2 * KV_WIDTH + 2 * FOURIER_WIDTH
Q_OFF = 0
K_OFF = ATTN_WIDTH
V_OFF = K_OFF + KV_WIDTH
GA_OFF = V_OFF + KV_WIDTH
U_OFF = GA_OFF + ATTN_WIDTH
GF_OFF = U_OFF + FOURIER_WIDTH
BLOCK = 128
ROPE_THETA = 10000.0
EPS = 1e-6
NEG = -1e30
LANES = 128
RADIX = 8
DFT_COLS = 256
DFT_BLOCK_ROWS = 128
LOG2E = math.log2(math.e)
MAX_UNSHIFTED_LOGIT2 = 100.0

ROW_TILE = 1024
FUSED_ROW_TILE = 1024
FUSED_SUB_ROWS = 512
ATTN_TILE = 1024
VMEM_LIMIT = 48 * 1024 * 1024
FUSED_VMEM_LIMIT = 54 * 1024 * 1024

BF16 = jnp.bfloat16
F32 = jnp.float32


def _compiler_params(grid_rank, vmem_limit=VMEM_LIMIT):
    return pltpu.CompilerParams(dimension_semantics=("arbitrary",) * grid_rank, vmem_limit_bytes=vmem_limit)


def _dot(a, b):
    return jnp.dot(a, b, preferred_element_type=F32)


def _dot_narrow(a, b):
    half = a.shape[0] // 2
    return jnp.concatenate([_dot(a[:half], b), _dot(a[half:], b)], axis=0)


def _dot_nt(a, b):
    return jax.lax.dot_general(a, b, (((1,), (1,)), ((), ())), preferred_element_type=F32)


def _pack_rows(x):
    return pltpu.bitcast(x.astype(BF16), jnp.uint32)


def _unpack_rows(bits):
    return pltpu.bitcast(bits, BF16)


def _half_rows(rows):
    return rows if rows == slice(None) else slice(rows.start // 2, rows.stop // 2)


def _rotate_half(t, first_half):
    width = t.shape[-1]
    fwd = pltpu.roll(t, HEAD_DIM // 2, axis=1)
    bwd = pltpu.roll(t, width - HEAD_DIM // 2, axis=1)
    return jnp.where(first_half, bwd, fwd)


def _fourier_weight_kernel(cs_ref, w_ref, o_ref):
    for g in range(N_GROUPS):
        w = w_ref[g]
        c = jnp.dot(cs_ref[0], w, preferred_element_type=F32, precision=jax.lax.Precision.HIGHEST)
        s = jnp.dot(cs_ref[1], w, preferred_element_type=F32, precision=jax.lax.Precision.HIGHEST)
        o_ref[g] = jnp.concatenate([c, s], axis=1).astype(o_ref.dtype)


def _fourier_weights(w_four, seq_len):
    idx = (jnp.arange(GROUP_DIM, dtype=jnp.int32)[:, None] * jnp.arange(GROUP_DIM, dtype=jnp.int32)[None, :]) % GROUP_DIM
    ang = idx.astype(F32) * (2.0 * math.pi / GROUP_DIM)
    scale = 1.0 / math.sqrt(seq_len * GROUP_DIM)
    cs = jnp.stack([jnp.cos(ang), jnp.sin(ang)]) * scale
    return pl.pallas_call(
        _fourier_weight_kernel,
        out_shape=jax.ShapeDtypeStruct((N_GROUPS, GROUP_DIM, 2 * GROUP_DIM), BF16),
        name="fourier_weight_prep",
    )(cs, w_four)


def _in_proj_rows(x, rows, gn_ref, w_ref, cos_ref, sin_ref, gains_ref, hs_ref, wab_ref,
                  q_ref, ka_ref, kb_ref, va_ref, vb_ref, sga_ref, a_ref, b_ref, sgf_ref):
    sub_rows = x.shape[0]
    xb = (x * gn_ref[...]).astype(BF16)
    cos = cos_ref[rows, :]
    sin = sin_ref[rows, :]
    qcos, qsin = cos * gains_ref[0:1, :], sin * gains_ref[1:2, :]
    kcos, ksin = cos * gains_ref[2:3, :], sin * gains_ref[3:4, :]
    ms = jnp.mean(x * x, axis=-1, keepdims=True)
    r = jnp.broadcast_to(jax.lax.rsqrt(ms + EPS), (sub_rows, LANES))
    eps_z = jnp.broadcast_to(EPS * (ms + EPS), (sub_rows, LANES))
    wide = lambda t, width: jnp.concatenate([t] * (width // LANES), axis=1)

    lane = jax.lax.broadcasted_iota(jnp.int32, (1, LANES), 1)
    first_half_pair = (lane % HEAD_DIM) < (HEAD_DIM // 2)
    low_head = lane < HEAD_DIM
    hs = hs_ref[...]

    def head_mean_sq(z):
        return _dot_narrow((z * z).astype(BF16), hs)

    def silu_of_scaled(z, width):
        h = z * wide(0.5 * r, width)
        return h + h * jnp.tanh(h)

    zq = _dot(xb, w_ref[:, Q_OFF:Q_OFF + ATTN_WIDTH])
    half = ATTN_WIDTH // 2
    ssq = jnp.concatenate([head_mean_sq(zq[:, :half]), head_mean_sq(zq[:, half:])], axis=1)
    qr = (zq * wide(qcos, ATTN_WIDTH)
          + _rotate_half(zq, wide(first_half_pair, ATTN_WIDTH)) * wide(qsin, ATTN_WIDTH))
    q_ref[rows, :] = (qr * jax.lax.rsqrt(ssq + wide(eps_z, ATTN_WIDTH))).astype(q_ref.dtype)

    zkv = _dot_narrow(xb, w_ref[:, K_OFF:K_OFF + 2 * KV_WIDTH])
    zk = zkv[:, :KV_WIDTH]
    ssk = head_mean_sq(zkv)[:, :KV_WIDTH]
    kr = (zk * kcos + _rotate_half(zk, first_half_pair) * ksin) * jax.lax.rsqrt(ssk + eps_z)
    kr_sw = pltpu.roll(kr, HEAD_DIM, axis=1)
    ka_ref[rows, :] = jnp.where(low_head, kr, kr_sw).astype(ka_ref.dtype)
    kb_ref[rows, :] = jnp.where(low_head, kr_sw, kr).astype(kb_ref.dtype)
    zv = zkv[:, KV_WIDTH:] * r
    zv_sw = pltpu.roll(zv, HEAD_DIM, axis=1)
    va_ref[rows, :] = jnp.where(low_head, zv, zv_sw).astype(va_ref.dtype)
    vb_ref[rows, :] = jnp.where(low_head, zv_sw, zv).astype(vb_ref.dtype)

    zga = _dot(xb, w_ref[:, GA_OFF:GA_OFF + ATTN_WIDTH])
    sga_ref[rows, :] = silu_of_scaled(zga, ATTN_WIDTH).astype(sga_ref.dtype)
    zgf = _dot(xb, w_ref[:, GF_OFF:GF_OFF + FOURIER_WIDTH])
    sgf_ref[rows, :] = silu_of_scaled(zgf, FOURIER_WIDTH).astype(sgf_ref.dtype)

    zu = (_dot(xb, w_ref[:, U_OFF:U_OFF + FOURIER_WIDTH]) * wide(r, FOURIER_WIDTH)).astype(BF16)
    for g in range(N_GROUPS):
        ab = _dot_narrow(zu[:, g * GROUP_DIM:(g + 1) * GROUP_DIM], wab_ref[g])
        half_idx, col = divmod(g * GROUP_DIM, DFT_COLS)
        a_ref[half_idx, _half_rows(rows), col:col + GROUP_DIM] = _pack_rows(ab[:, :GROUP_DIM])
        b_ref[half_idx, _half_rows(rows), col:col + GROUP_DIM] = _pack_rows(ab[:, GROUP_DIM:])


N_IN_PROJ_INPUTS = 7


def _in_proj_kernel(x_ref, *refs):
    _in_proj_rows(x_ref[...], slice(None), *refs)


def _row_spec(tm, width):
    return pl.BlockSpec((tm, width), lambda i: (i, 0))


def _split_spec(tm, width=DFT_COLS):
    return pl.BlockSpec((FOURIER_WIDTH // width, tm // 2, width), lambda i: (0, i, 0))


def _resident_spec(shape):
    return pl.BlockSpec(shape, lambda i: (0,) * len(shape), pipeline_mode=pl.Buffered(1))


def _layer_weight_spec(layer, rows, cols):
    return pl.BlockSpec((None, rows, cols), lambda i: (layer, 0, 0), pipeline_mode=pl.Buffered(1))


def _in_proj_specs(tm, rows, seq_len, layer):
    steps_per_seq = seq_len // tm
    tab_spec = pl.BlockSpec((tm, LANES), lambda i: (i % steps_per_seq, 0))
    in_specs = [_resident_spec((1, D_MODEL)), _layer_weight_spec(layer, D_MODEL, IN_WIDTH), tab_spec, tab_spec,
                _resident_spec((4, LANES)), _resident_spec((2 * LANES, 2 * LANES)),
                _resident_spec((N_GROUPS, GROUP_DIM, 2 * GROUP_DIM))]
    row_widths = (ATTN_WIDTH, LANES, LANES, LANES, LANES, ATTN_WIDTH)
    out_specs = ([_row_spec(tm, w) for w in row_widths] + [_split_spec(tm)] * 2
                 + [_row_spec(tm, FOURIER_WIDTH)])
    out_shape = ([jax.ShapeDtypeStruct((rows, w), BF16) for w in row_widths]
                 + [jax.ShapeDtypeStruct((FOURIER_WIDTH // DFT_COLS, rows // 2, DFT_COLS), jnp.uint32)] * 2
                 + [jax.ShapeDtypeStruct((rows, FOURIER_WIDTH), BF16)])
    return in_specs, out_specs, out_shape


def _in_proj(x2d, seq_len, layer, in_w):
    rows = x2d.shape[0]
    tm = ROW_TILE
    in_specs, out_specs, out_shape = _in_proj_specs(tm, rows, seq_len, layer)
    return pl.pallas_call(
        _in_proj_kernel,
        grid=(rows // tm,),
        in_specs=[_row_spec(tm, D_MODEL)] + in_specs,
        out_specs=out_specs,
        out_shape=out_shape,
        compiler_params=_compiler_params(1),
        name="in_proj",
    )(x2d, *in_w)


def _attn_kernel(sink_ref, bounded_ref, q_ref, ka_ref, kb_ref, va_ref, vb_ref, sg_ref, o_ref, *, n_blocks):
    refs = (sink_ref, q_ref, ka_ref, kb_ref, va_ref, vb_ref, sg_ref, o_ref)

    @pl.when(bounded_ref[0] == 1)
    def _():
        _attn_body(*refs, n_blocks=n_blocks, shift=False)

    @pl.when(bounded_ref[0] == 0)
    def _():
        _attn_body(*refs, n_blocks=n_blocks, shift=True)


def _attn_body(sink_ref, q_ref, ka_ref, kb_ref, va_ref, vb_ref, sg_ref, o_ref, *, n_blocks, shift):
    i = pl.program_id(1)
    blocks_per_step = ATTN_TILE // BLOCK
    row = jax.lax.broadcasted_iota(jnp.int32, (BLOCK, 2 * BLOCK), 0)
    col = jax.lax.broadcasted_iota(jnp.int32, (BLOCK, 2 * BLOCK), 1) % BLOCK
    lane = jax.lax.broadcasted_iota(jnp.int32, (1, LANES), 1)
    low = lane < HEAD_DIM
    zero = jnp.zeros((), BF16)
    lane_full = jax.lax.broadcasted_iota(jnp.int32, (BLOCK, LANES), 1)
    ones_low = jnp.where(lane_full < HEAD_DIM, 1.0, 0.0).astype(BF16)
    ones_high = jnp.where(lane_full < HEAD_DIM, 0.0, 1.0).astype(BF16)

    for jb in range(blocks_per_step):
        ib = i * blocks_per_step + jb
        rows = slice(jb * BLOCK, (jb + 1) * BLOCK)
        edge_prev = jnp.where(ib == 0, NEG, 0.0)
        edge_next = jnp.where(ib == n_blocks - 1, NEG, 0.0)
        bias_prev = jnp.where(col >= row, 0.0, NEG) + edge_prev
        bias_next = jnp.where(col <= row, 0.0, NEG) + edge_next
        starts = [pl.multiple_of(jnp.clip(ib + c, 0, n_blocks - 1) * BLOCK, BLOCK) for c in (-1, 0, 1)]
        for kvh, (k_ref, v_ref) in enumerate(((ka_ref, va_ref), (kb_ref, vb_ref))):
            kbd, vbd = [], []
            for st in starts:
                kblk = k_ref[0, pl.ds(st, BLOCK), :]
                vblk = v_ref[0, pl.ds(st, BLOCK), :]
                kbd.append(jnp.concatenate([jnp.where(low, kblk, zero), jnp.where(low, zero, kblk)], axis=0))
                vbd.append(jnp.concatenate([
                    jnp.concatenate([jnp.where(low, vblk, zero), ones_low], axis=1),
                    jnp.concatenate([jnp.where(low, zero, vblk), ones_high], axis=1)], axis=0))
            vbd = jnp.concatenate(vbd, axis=0)
            for pair in range(2):
                pidx = kvh * 2 + pair
                lanes = slice(pidx * LANES, (pidx + 1) * LANES)
                qp = q_ref[0, rows, lanes]
                scores = [_dot_nt(qp, kb) for kb in kbd]
                sink_e = sink_ref[2 * pidx]
                sink_o = sink_ref[2 * pidx + 1]
                if shift:
                    scores = [scores[0] + bias_prev, scores[1], scores[2] + bias_next]
                    smax = jnp.maximum(jnp.maximum(scores[0], scores[1]), scores[2])
                    m_e = jnp.maximum(jnp.max(smax[:, :BLOCK], axis=-1, keepdims=True), sink_e)
                    m_o = jnp.maximum(jnp.max(smax[:, BLOCK:], axis=-1, keepdims=True), sink_o)
                    m_both = jnp.concatenate([jnp.broadcast_to(m_e, (BLOCK, BLOCK)),
                                              jnp.broadcast_to(m_o, (BLOCK, BLOCK))], axis=1)
                    probs = [jnp.exp2(s - m_both).astype(BF16) for s in scores]
                    p_sink = jnp.where(low, jnp.exp2(sink_e - m_e), jnp.exp2(sink_o - m_o))
                else:
                    scores = [s.astype(BF16) for s in scores]
                    scores = [scores[0] + bias_prev.astype(BF16), scores[1], scores[2] + bias_next.astype(BF16)]
                    probs = [jnp.exp2(s) for s in scores]
                    p_sink = jnp.exp2(jnp.where(low, sink_e, sink_o))
                acc = _dot(jnp.concatenate(probs, axis=1), vbd)
                out = acc[:, :LANES] / (acc[:, LANES:] + p_sink) * sg_ref[0, rows, lanes].astype(F32)
                o_ref[0, rows, lanes] = out.astype(o_ref.dtype)


def _attn_out_kernel(sink_ref, bounded_ref, q_ref, ka_ref, kb_ref, va_ref, vb_ref, sg_ref,
                     x_ref, f_ref, sgf_ref, w_ref, y_ref, ma_ref, *, n_blocks):
    _attn_kernel(sink_ref, bounded_ref, q_ref, ka_ref, kb_ref, va_ref, vb_ref, sg_ref, ma_ref, n_blocks=n_blocks)
    f = jnp.concatenate([_unpack_rows(f_ref[s, 0]) for s in range(FOURIER_WIDTH // LANES)], axis=1)
    y_ref[0] =(x_ref[0] + _dot(ma_ref[0], w_ref[:ATTN_WIDTH, :])
                + _dot(f * sgf_ref[0], w_ref[ATTN_WIDTH:, :]))


def _attention_out_proj(sink2, bounded, q, ka, kb, va, vb, sga, x, f, sgf, layer, w_bf):
    bsz, seq_len, _ = q.shape
    tq = ATTN_TILE
    tile = lambda width: pl.BlockSpec((1, tq, width), lambda b, i, s, fl: (b, i, 0))
    kv_spec = pl.BlockSpec((1, seq_len, LANES), lambda b, i, s, fl: (b, 0, 0))
    f_spec = pl.BlockSpec((FOURIER_WIDTH // LANES, 1, tq // 2, LANES), lambda b, i, s, fl: (0, b, i, 0))
    w_spec = pl.BlockSpec((None, D_MODEL, D_MODEL), lambda b, i, s, fl: (layer, 0, 0), pipeline_mode=pl.Buffered(1))
    grid_spec = pltpu.PrefetchScalarGridSpec(
        num_scalar_prefetch=2,
        grid=(bsz, seq_len // tq),
        in_specs=[tile(ATTN_WIDTH), kv_spec, kv_spec, kv_spec, kv_spec, tile(ATTN_WIDTH),
                  tile(D_MODEL), f_spec, tile(FOURIER_WIDTH), w_spec],
        out_specs=tile(D_MODEL),
        scratch_shapes=[pltpu.VMEM((1, tq, ATTN_WIDTH), BF16)],
    )
    return pl.pallas_call(
        functools.partial(_attn_out_kernel, n_blocks=seq_len // BLOCK),
        grid_spec=grid_spec,
        out_shape=jax.ShapeDtypeStruct((bsz, seq_len, D_MODEL), F32),
        compiler_params=_compiler_params(2),
        name="attention_out_proj",
    )(sink2, bounded, q, ka, kb, va, vb, sga, x, f, sgf, w_bf)


def _attention(sink2, bounded, q, ka, kb, va, vb, sga):
    bsz, seq_len, _ = q.shape
    tq = ATTN_TILE
    q_spec = pl.BlockSpec((1, tq, ATTN_WIDTH), lambda b, i, s, f: (b, i, 0))
    kv_spec = pl.BlockSpec((1, seq_len, LANES), lambda b, i, s, f: (b, 0, 0))
    grid_spec = pltpu.PrefetchScalarGridSpec(
        num_scalar_prefetch=2,
        grid=(bsz, seq_len // tq),
        in_specs=[q_spec, kv_spec, kv_spec, kv_spec, kv_spec, q_spec],
        out_specs=q_spec,
    )
    return pl.pallas_call(
        functools.partial(_attn_kernel, n_blocks=seq_len // BLOCK),
        grid_spec=grid_spec,
        out_shape=jax.ShapeDtypeStruct((bsz, seq_len, ATTN_WIDTH), BF16),
        compiler_params=_compiler_params(2),
        name="band_attention",
    )(sink2, bounded, q, ka, kb, va, vb, sga)


def _seq_dft_kernel(a_ref, b_ref, m_ref, o_ref, zz_ref, pair_ref, *, chunk):
    half = RADIX // 2
    root_half = math.sqrt(0.5)
    cadd = lambda u, v: (u[0] + v[0], u[1] + v[1])
    csub = lambda u, v: (u[0] - v[0], u[1] - v[1])
    add_i = lambda u, v: (u[0] - v[1], u[1] + v[0])
    sub_i = lambda u, v: (u[0] + v[1], u[1] - v[0])

    def four_point(w):
        s02, d02, s13, d13 = cadd(w[0], w[2]), csub(w[0], w[2]), cadd(w[1], w[3]), csub(w[1], w[3])
        return [cadd(s02, s13), add_i(d02, d13), csub(s02, s13), sub_i(d02, d13)]

    def butterflies(parity):
        for rb in range(chunk // DFT_BLOCK_ROWS):
            for cb in range(DFT_COLS // LANES):
                rows = lambda q: slice(q * chunk + rb * DFT_BLOCK_ROWS, q * chunk + (rb + 1) * DFT_BLOCK_ROWS)
                cols = slice(cb * LANES, (cb + 1) * LANES)
                z = [(_unpack_rows(a_ref[0, 0, _half_rows(rows(q)), cols]),
                      _unpack_rows(b_ref[0, 0, _half_rows(rows(q)), cols])) for q in range(RADIX)]
                if parity == 0:
                    w = [cadd(z[q], z[q + half]) for q in range(half)]
                else:
                    d = [csub(z[q], z[q + half]) for q in range(half)]
                    w = [d[0],
                         ((d[1][0] - d[1][1]) * root_half, (d[1][0] + d[1][1]) * root_half),
                         (-d[2][1], d[2][0]),
                         ((-d[3][0] - d[3][1]) * root_half, (d[3][0] - d[3][1]) * root_half)]
                for t, (re, im) in enumerate(four_point(w)):
                    zz_ref[2 * t + parity, rows(0), cols] = re
                    zz_ref[2 * t + parity, rows(1), cols] = im

    def project(r):
        f = _dot(m_ref[r], zz_ref[r])
        for cb in range(DFT_COLS // LANES):
            pair_ref[r // 2, cb, pl.ds(r % 2, chunk, stride=2), :] = f[:, cb * LANES:(cb + 1) * LANES]

    butterflies(0)
    for t in range(half):
        project(2 * t)
    butterflies(1)
    for t in range(half):
        project(2 * t + 1)
        for cb in range(DFT_COLS // LANES):
            o_ref[cb, 0, pl.ds(t, chunk, stride=half), :] = _pack_rows(pair_ref[t, cb])


def _dft_matrix(seq_len):
    chunk = seq_len // RADIX
    split = 32
    period = seq_len // (RADIX * split)
    m = jnp.arange(chunk, dtype=jnp.int32)[None, :]
    ang_hi = ((jnp.arange(chunk // split, dtype=jnp.int32)[:, None] * m) % period).astype(F32) * (2.0 * math.pi / period)
    ang_lo = ((jnp.arange(RADIX * split, dtype=jnp.int32)[:, None] * m) % seq_len).astype(F32) * (2.0 * math.pi / seq_len)
    c_hi, s_hi = jnp.cos(ang_hi)[None, :, None, :], jnp.sin(ang_hi)[None, :, None, :]
    lo = lambda t: t.reshape(split, RADIX, chunk).transpose(1, 0, 2)[:, None, :, :]
    c_lo, s_lo = lo(jnp.cos(ang_lo)), lo(jnp.sin(ang_lo))
    cos = (c_hi * c_lo - s_hi * s_lo).reshape(RADIX, chunk, chunk)
    sin = (s_hi * c_lo + c_hi * s_lo).reshape(RADIX, chunk, chunk)
    return jnp.concatenate([cos, -sin], axis=2).astype(BF16)


def _seq_dft(a, b, dft_m):
    n_halves, bsz, half_len, _ = a.shape
    seq_len = 2 * half_len
    chunk = seq_len // RADIX
    slabs_per_step = DFT_COLS // LANES
    col_spec = pl.BlockSpec((1, 1, half_len, DFT_COLS), lambda bi, h: (h, bi, 0, 0))
    out_spec = pl.BlockSpec((slabs_per_step, 1, half_len, LANES), lambda bi, h: (h, bi, 0, 0))
    m_spec = pl.BlockSpec((RADIX, chunk, 2 * chunk), lambda bi, h: (0, 0, 0), pipeline_mode=pl.Buffered(1))
    return pl.pallas_call(
        functools.partial(_seq_dft_kernel, chunk=chunk),
        grid=(bsz, n_halves),
        in_specs=[col_spec, col_spec, m_spec],
        out_specs=out_spec,
        out_shape=jax.ShapeDtypeStruct((n_halves * slabs_per_step, bsz, half_len, LANES), jnp.uint32),
        scratch_shapes=[pltpu.VMEM((RADIX, 2 * chunk, DFT_COLS), BF16),
                        pltpu.VMEM((RADIX // 2, slabs_per_step, 2 * chunk, LANES), F32)],
        compiler_params=_compiler_params(2),
        name="seq_dft",
    )(a, b, dft_m)


def _out_proj_rows(rows, x_ref, ma_ref, f_ref, sgf_ref, w_ref, o_ref):
    f = jnp.concatenate([_unpack_rows(f_ref[s, _half_rows(rows), :])
                         for s in range(FOURIER_WIDTH // LANES)], axis=1)
    mf = f * sgf_ref[rows, :]
    o_ref[rows, :] = (x_ref[rows, :] + _dot(ma_ref[rows, :], w_ref[:ATTN_WIDTH, :])
                      + _dot(mf, w_ref[ATTN_WIDTH:, :]))


N_OUT_PROJ_INPUTS = 5


def _out_in_proj_kernel(*refs):
    out_in = refs[:N_OUT_PROJ_INPUTS]
    in_refs = refs[N_OUT_PROJ_INPUTS:N_OUT_PROJ_INPUTS + N_IN_PROJ_INPUTS]
    y_ref = refs[N_OUT_PROJ_INPUTS + N_IN_PROJ_INPUTS]
    out_refs = refs[N_OUT_PROJ_INPUTS + N_IN_PROJ_INPUTS + 1:]
    for sub in range(y_ref.shape[0] // FUSED_SUB_ROWS):
        rows = slice(sub * FUSED_SUB_ROWS, (sub + 1) * FUSED_SUB_ROWS)
        _out_proj_rows(rows, *out_in, y_ref)
        _in_proj_rows(y_ref[rows, :], rows, *in_refs, *out_refs)


def _out_proj_specs(tm, layer):
    return [_row_spec(tm, D_MODEL), _row_spec(tm, ATTN_WIDTH), _split_spec(tm, LANES),
            _row_spec(tm, FOURIER_WIDTH), _layer_weight_spec(layer, D_MODEL, D_MODEL)]


def _out_in_proj(x2d, ma, f, sgf, layer, w_bf, seq_len, in_w):
    rows = x2d.shape[0]
    tm = FUSED_ROW_TILE
    i_in_specs, i_out_specs, i_out_shape = _in_proj_specs(tm, rows, seq_len, layer + 1)
    outs = pl.pallas_call(
        _out_in_proj_kernel,
        grid=(rows // tm,),
        in_specs=_out_proj_specs(tm, layer) + i_in_specs,
        out_specs=[_row_spec(tm, D_MODEL)] + i_out_specs,
        out_shape=[jax.ShapeDtypeStruct((rows, D_MODEL), F32)] + i_out_shape,
        compiler_params=_compiler_params(1, FUSED_VMEM_LIMIT),
        name="out_in_proj",
    )(x2d, ma, f, sgf, w_bf, *in_w)
    return outs[0], outs[1:]


def _rope_tables(seq_len):
    half = HEAD_DIM // 2
    inv_freq = 1.0 / (ROPE_THETA ** (jnp.arange(half, dtype=F32) / half))
    ang = jnp.arange(seq_len, dtype=F32)[:, None] * inv_freq[None, :]
    cos = jnp.cos(ang)
    sin = jnp.sin(ang)
    cos_t = jnp.concatenate([cos, cos, cos, cos], axis=1)
    sin_t = jnp.concatenate([-sin, sin, -sin, sin], axis=1)
    return cos_t, sin_t


def _rope_gains(q_gain, k_gain):
    pair = lambda g: jnp.tile(g, LANES // HEAD_DIM)
    rot = lambda g: jnp.roll(g, HEAD_DIM // 2)
    gq = q_gain.astype(F32) * (HEAD_DIM ** -0.5 * LOG2E)
    gk = k_gain.astype(F32)
    return jnp.stack([pair(gq), pair(rot(gq)), pair(gk), pair(rot(gk))])


def _head_mean_matrix(width):
    head = jnp.arange(width, dtype=jnp.int32) // HEAD_DIM
    return jnp.where(head[:, None] == head[None, :], 1.0 / HEAD_DIM, 0.0).astype(BF16)


def _trunk(x, layers, dft_m):
    bsz, seq_len, _ = x.shape
    x2d = x.reshape(bsz * seq_len, D_MODEL)
    r3 = lambda t: t.reshape(bsz, seq_len, t.shape[-1])
    split4 = lambda t: t.reshape(t.shape[0], bsz, seq_len // 2, DFT_COLS)
    proj = _in_proj(x2d, seq_len, 0, layers[0]["in_w"])
    for l, layer in enumerate(layers):
        q, ka, kb, va, vb, sga, a, b, sgf = proj
        f = _seq_dft(split4(a), split4(b), dft_m)
        attn_args = (layer["sink2"], layer["bounded"], r3(q), r3(ka), r3(kb), r3(va), r3(vb), r3(sga))
        if l + 1 < len(layers):
            ma = _attention(*attn_args).reshape(bsz * seq_len, ATTN_WIDTH)
            x2d, proj = _out_in_proj(x2d, ma, f.reshape(f.shape[0], bsz * seq_len // 2, LANES), sgf,
                                     l, layer["w_out"], seq_len, layers[l + 1]["in_w"])
        else:
            return _attention_out_proj(*attn_args, r3(x2d), f, r3(sgf), l, layer["w_out"])


def kernel(x_prompt, x_sample, norm_gain, w_in, q_norm_gain, k_norm_gain, sink_logit, w_fourier, w_out):
    depth = norm_gain.shape[0]
    seq_p, seq_s = x_prompt.shape[1], x_sample.shape[1]
    assert seq_p == seq_s
    cos_t, sin_t = _rope_tables(seq_p)
    hs = _head_mean_matrix(2 * LANES)
    w_in_bf = w_in.astype(BF16)
    w_out_bf = w_out.astype(BF16)
    layers = []
    for l in range(depth):
        sink2 = sink_logit[l].astype(F32) * LOG2E
        logit_bound = (HEAD_DIM ** 0.5 * LOG2E) * jnp.max(jnp.abs(q_norm_gain[l])) * jnp.max(jnp.abs(k_norm_gain[l]))
        bounded = jnp.maximum(logit_bound, jnp.max(jnp.abs(sink2))) <= MAX_UNSHIFTED_LOGIT2
        in_w = (norm_gain[l].astype(F32)[None, :], w_in_bf, cos_t, sin_t,
                _rope_gains(q_norm_gain[l], k_norm_gain[l]), hs, _fourier_weights(w_fourier[l], seq_p))
        layers.append(dict(in_w=in_w, sink2=sink2, bounded=bounded.astype(jnp.int32)[None], w_out=w_out_bf))
    dft_m = _dft_matrix(seq_p)
    return (_trunk(x_prompt, layers, dft_m), _trunk(x_sample, layers, dft_m))
```

```python
import functools
import math

import jax
import jax.numpy as jnp
from jax.experimental import pallas as pl
from jax.experimental.pallas import tpu as pltpu

D_MODEL = 1024
HEAD_DIM = 64
N_Q_HEADS = 8
N_KV_HEADS = 2
ATTN_WIDTH = N_Q_HEADS * HEAD_DIM
KV_WIDTH = N_KV_HEADS * HEAD_DIM
FOURIER_WIDTH = D_MODEL - ATTN_WIDTH
N_GROUPS = 4
GROUP_DIM = FOURIER_WIDTH // N_GROUPS
IN_WIDTH = 2 * ATTN_WIDTH + 2 * KV_WIDTH + 2 * FOURIER_WIDTH
Q_OFF = 0
K_OFF = ATTN_WIDTH
V_OFF = K_OFF + KV_WIDTH
GA_OFF = V_OFF + KV_WIDTH
U_OFF = GA_OFF + ATTN_WIDTH
GF_OFF = U_OFF + FOURIER_WIDTH
BLOCK = 128
ROPE_THETA = 10000.0
EPS = 1e-6
NEG = -1e30
LANES = 128
RADIX = 8
DFT_COLS = 256
DFT_BLOCK_ROWS = 128
DFT_SPLIT = 32
LOG2E = math.log2(math.e)
MAX_UNSHIFTED_LOGIT2 = 100.0

ROW_TILE = 1024
FUSED_ROW_TILE = 1024
FUSED_SUB_ROWS = 512
ATTN_TILE = 1024
VMEM_LIMIT = 48 * 1024 * 1024
FUSED_VMEM_LIMIT = 54 * 1024 * 1024

BF16 = jnp.bfloat16
F32 = jnp.float32


def _compiler_params(grid_rank, vmem_limit=VMEM_LIMIT):
    return pltpu.CompilerParams(dimension_semantics=("arbitrary",) * grid_rank, vmem_limit_bytes=vmem_limit)


def _dot(a, b):
    return jnp.dot(a, b, preferred_element_type=F32)


def _dot_narrow(a, b):
    half = a.shape[0] // 2
    return jnp.concatenate([_dot(a[:half], b), _dot(a[half:], b)], axis=0)


def _dot_nt(a, b):
    return jax.lax.dot_general(a, b, (((1,), (1,)), ((), ())), preferred_element_type=F32)


def _pack_rows(x):
    return pltpu.bitcast(x.astype(BF16), jnp.uint32)


def _unpack_rows(bits):
    return pltpu.bitcast(bits, BF16)


def _half_rows(rows):
    return rows if rows == slice(None) else slice(rows.start // 2, rows.stop // 2)


def _rotate_half(t, first_half):
    width = t.shape[-1]
    fwd = pltpu.roll(t, HEAD_DIM // 2, axis=1)
    bwd = pltpu.roll(t, width - HEAD_DIM // 2, axis=1)
    return jnp.where(first_half, bwd, fwd)


def _fourier_weight_kernel(cs_ref, w_ref, o_ref):
    for g in range(N_GROUPS):
        w = w_ref[g]
        c = jnp.dot(cs_ref[0], w, preferred_element_type=F32, precision=jax.lax.Precision.HIGHEST)
        s = jnp.dot(cs_ref[1], w, preferred_element_type=F32, precision=jax.lax.Precision.HIGHEST)
        o_ref[g] = jnp.concatenate([c, s], axis=1).astype(o_ref.dtype)


def _fourier_weights(w_four, seq_len):
    idx = (jnp.arange(GROUP_DIM, dtype=jnp.int32)[:, None] * jnp.arange(GROUP_DIM, dtype=jnp.int32)[None, :]) % GROUP_DIM
    ang = idx.astype(F32) * (2.0 * math.pi / GROUP_DIM)
    scale = 1.0 / math.sqrt(seq_len * GROUP_DIM)
    cs = jnp.stack([jnp.cos(ang), jnp.sin(ang)]) * scale
    return pl.pallas_call(
        _fourier_weight_kernel,
        out_shape=jax.ShapeDtypeStruct((N_GROUPS, GROUP_DIM, 2 * GROUP_DIM), BF16),
        name="fourier_weight_prep",
    )(cs, w_four)


def _in_proj_rows(x, rows, gn_ref, w_ref, cos_ref, sin_ref, gains_ref, hs_ref, wab_ref,
                  q_ref, ka_ref, kb_ref, va_ref, vb_ref, sga_ref, a_ref, b_ref, sgf_ref):
    sub_rows = x.shape[0]
    xb = (x * gn_ref[...]).astype(BF16)
    cos = cos_ref[rows, :]
    sin = sin_ref[rows, :]
    qcos, qsin = cos * gains_ref[0:1, :], sin * gains_ref[1:2, :]
    kcos, ksin = cos * gains_ref[2:3, :], sin * gains_ref[3:4, :]
    ms = jnp.mean(x * x, axis=-1, keepdims=True)
    r = jnp.broadcast_to(jax.lax.rsqrt(ms + EPS), (sub_rows, LANES))
    eps_z = jnp.broadcast_to(EPS * (ms + EPS), (sub_rows, LANES))
    wide = lambda t, width: jnp.concatenate([t] * (width // LANES), axis=1)

    lane = jax.lax.broadcasted_iota(jnp.int32, (1, LANES), 1)
    first_half_pair = (lane % HEAD_DIM) < (HEAD_DIM // 2)
    low_head = lane < HEAD_DIM
    hs = hs_ref[...]

    def head_mean_sq(z):
        return _dot_narrow((z * z).astype(BF16), hs)

    def silu_of_scaled(z, width):
        h = z * wide(0.5 * r, width)
        return h + h * jnp.tanh(h)

    zq = _dot(xb, w_ref[:, Q_OFF:Q_OFF + ATTN_WIDTH])
    half = ATTN_WIDTH // 2
    ssq = jnp.concatenate([head_mean_sq(zq[:, :half]), head_mean_sq(zq[:, half:])], axis=1)
    qr = (zq * wide(qcos, ATTN_WIDTH)
          + _rotate_half(zq, wide(first_half_pair, ATTN_WIDTH)) * wide(qsin, ATTN_WIDTH))
    q_ref[rows, :] = (qr * jax.lax.rsqrt(ssq + wide(eps_z, ATTN_WIDTH))).astype(q_ref.dtype)

    zkv = _dot_narrow(xb, w_ref[:, K_OFF:K_OFF + 2 * KV_WIDTH])
    zk = zkv[:, :KV_WIDTH]
    ssk = head_mean_sq(zkv)[:, :KV_WIDTH]
    kr = (zk * kcos + _rotate_half(zk, first_half_pair) * ksin) * jax.lax.rsqrt(ssk + eps_z)
    kr_sw = pltpu.roll(kr, HEAD_DIM, axis=1)
    ka_ref[rows, :] = jnp.where(low_head, kr, kr_sw).astype(ka_ref.dtype)
    kb_ref[rows, :] = jnp.where(low_head, kr_sw, kr).astype(kb_ref.dtype)
    zv = zkv[:, KV_WIDTH:] * r
    zv_sw = pltpu.roll(zv, HEAD_DIM, axis=1)
    va_ref[rows, :] = jnp.where(low_head, zv, zv_sw).astype(va_ref.dtype)
    vb_ref[rows, :] = jnp.where(low_head, zv_sw, zv).astype(vb_ref.dtype)

    zga = _dot(xb, w_ref[:, GA_OFF:GA_OFF + ATTN_WIDTH])
    sga_ref[rows, :] = silu_of_scaled(zga, ATTN_WIDTH).astype(sga_ref.dtype)
    zgf = _dot(xb, w_ref[:, GF_OFF:GF_OFF + FOURIER_WIDTH])
    sgf_ref[rows, :] = silu_of_scaled(zgf, FOURIER_WIDTH).astype(sgf_ref.dtype)

    zu = (_dot(xb, w_ref[:, U_OFF:U_OFF + FOURIER_WIDTH]) * wide(r, FOURIER_WIDTH)).astype(BF16)
    for g in range(N_GROUPS):
        ab = _dot_narrow(zu[:, g * GROUP_DIM:(g + 1) * GROUP_DIM], wab_ref[g])
        half_idx, col = divmod(g * GROUP_DIM, DFT_COLS)
        a_ref[half_idx, _half_rows(rows), col:col + GROUP_DIM] = _pack_rows(ab[:, :GROUP_DIM])
        b_ref[half_idx, _half_rows(rows), col:col + GROUP_DIM] = _pack_rows(ab[:, GROUP_DIM:])


N_IN_PROJ_INPUTS = 7


def _in_proj_kernel(x_ref, *refs):
    _in_proj_rows(x_ref[...], slice(None), *refs)


def _row_spec(tm, width):
    return pl.BlockSpec((tm, width), lambda i: (i, 0))


def _split_spec(tm, width=DFT_COLS):
    return pl.BlockSpec((FOURIER_WIDTH // width, tm // 2, width), lambda i: (0, i, 0))


def _resident_spec(shape):
    return pl.BlockSpec(shape, lambda i: (0,) * len(shape), pipeline_mode=pl.Buffered(1))


def _layer_weight_spec(layer, rows, cols):
    return pl.BlockSpec((None, rows, cols), lambda i: (layer, 0, 0), pipeline_mode=pl.Buffered(1))


def _in_proj_specs(tm, rows, seq_len, layer):
    steps_per_seq = seq_len // tm
    tab_spec = pl.BlockSpec((tm, LANES), lambda i: (i % steps_per_seq, 0))
    in_specs = [_resident_spec((1, D_MODEL)), _layer_weight_spec(layer, D_MODEL, IN_WIDTH), tab_spec, tab_spec,
                _resident_spec((4, LANES)), _resident_spec((2 * LANES, 2 * LANES)),
                _resident_spec((N_GROUPS, GROUP_DIM, 2 * GROUP_DIM))]
    row_widths = (ATTN_WIDTH, LANES, LANES, LANES, LANES, ATTN_WIDTH)
    out_specs = ([_row_spec(tm, w) for w in row_widths] + [_split_spec(tm)] * 2
                 + [_row_spec(tm, FOURIER_WIDTH)])
    out_shape = ([jax.ShapeDtypeStruct((rows, w), BF16) for w in row_widths]
                 + [jax.ShapeDtypeStruct((FOURIER_WIDTH // DFT_COLS, rows // 2, DFT_COLS), jnp.uint32)] * 2
                 + [jax.ShapeDtypeStruct((rows, FOURIER_WIDTH), BF16)])
    return in_specs, out_specs, out_shape


def _in_proj(x2d, seq_len, layer, in_w):
    rows = x2d.shape[0]
    tm = ROW_TILE
    in_specs, out_specs, out_shape = _in_proj_specs(tm, rows, seq_len, layer)
    return pl.pallas_call(
        _in_proj_kernel,
        grid=(rows // tm,),
        in_specs=[_row_spec(tm, D_MODEL)] + in_specs,
        out_specs=out_specs,
        out_shape=out_shape,
        compiler_params=_compiler_params(1),
        name="in_proj",
    )(x2d, *in_w)


def _attn_kernel(sink_ref, bounded_ref, q_ref, ka_ref, kb_ref, va_ref, vb_ref, sg_ref, o_ref, *, n_blocks):
    refs = (sink_ref, q_ref, ka_ref, kb_ref, va_ref, vb_ref, sg_ref, o_ref)

    @pl.when(bounded_ref[0] == 1)
    def _():
        _attn_body(*refs, n_blocks=n_blocks, shift=False)

    @pl.when(bounded_ref[0] == 0)
    def _():
        _attn_body(*refs, n_blocks=n_blocks, shift=True)


def _attn_body(sink_ref, q_ref, ka_ref, kb_ref, va_ref, vb_ref, sg_ref, o_ref, *, n_blocks, shift):
    i = pl.program_id(1)
    blocks_per_step = ATTN_TILE // BLOCK
    row = jax.lax.broadcasted_iota(jnp.int32, (BLOCK, 2 * BLOCK), 0)
    col = jax.lax.broadcasted_iota(jnp.int32, (BLOCK, 2 * BLOCK), 1) % BLOCK
    lane = jax.lax.broadcasted_iota(jnp.int32, (1, LANES), 1)
    low = lane < HEAD_DIM
    zero = jnp.zeros((), BF16)
    lane_full = jax.lax.broadcasted_iota(jnp.int32, (BLOCK, LANES), 1)
    ones_low = jnp.where(lane_full < HEAD_DIM, 1.0, 0.0).astype(BF16)
    ones_high = jnp.where(lane_full < HEAD_DIM, 0.0, 1.0).astype(BF16)

    for jb in range(blocks_per_step):
        ib = i * blocks_per_step + jb
        rows = slice(jb * BLOCK, (jb + 1) * BLOCK)
        edge_prev = jnp.where(ib == 0, NEG, 0.0)
        edge_next = jnp.where(ib == n_blocks - 1, NEG, 0.0)
        bias_prev = jnp.where(col >= row, 0.0, NEG) + edge_prev
        bias_next = jnp.where(col <= row, 0.0, NEG) + edge_next
        starts = [pl.multiple_of(jnp.clip(ib + c, 0, n_blocks - 1) * BLOCK, BLOCK) for c in (-1, 0, 1)]
        for kvh, (k_ref, v_ref) in enumerate(((ka_ref, va_ref), (kb_ref, vb_ref))):
            kbd, vbd = [], []
            for st in starts:
                kblk = k_ref[0, pl.ds(st, BLOCK), :]
                vblk = v_ref[0, pl.ds(st, BLOCK), :]
                kbd.append(jnp.concatenate([jnp.where(low, kblk, zero), jnp.where(low, zero, kblk)], axis=0))
                vbd.append(jnp.concatenate([
                    jnp.concatenate([jnp.where(low, vblk, zero), ones_low], axis=1),
                    jnp.concatenate([jnp.where(low, zero, vblk), ones_high], axis=1)], axis=0))
            vbd = jnp.concatenate(vbd, axis=0)
            for pair in range(2):
                pidx = kvh * 2 + pair
                lanes = slice(pidx * LANES, (pidx + 1) * LANES)
                qp = q_ref[0, rows, lanes]
                scores = [_dot_nt(qp, kb) for kb in kbd]
                sink_e = sink_ref[2 * pidx]
                sink_o = sink_ref[2 * pidx + 1]
                if shift:
                    scores = [scores[0] + bias_prev, scores[1], scores[2] + bias_next]
                    smax = jnp.maximum(jnp.maximum(scores[0], scores[1]), scores[2])
                    m_e = jnp.maximum(jnp.max(smax[:, :BLOCK], axis=-1, keepdims=True), sink_e)
                    m_o = jnp.maximum(jnp.max(smax[:, BLOCK:], axis=-1, keepdims=True), sink_o)
                    m_both = jnp.concatenate([jnp.broadcast_to(m_e, (BLOCK, BLOCK)),
                                              jnp.broadcast_to(m_o, (BLOCK, BLOCK))], axis=1)
                    probs = [jnp.exp2(s - m_both).astype(BF16) for s in scores]
                    p_sink = jnp.where(low, jnp.exp2(sink_e - m_e), jnp.exp2(sink_o - m_o))
                else:
                    scores = [s.astype(BF16) for s in scores]
                    scores = [scores[0] + bias_prev.astype(BF16), scores[1], scores[2] + bias_next.astype(BF16)]
                    probs = [jnp.exp2(s) for s in scores]
                    p_sink = jnp.exp2(jnp.where(low, sink_e, sink_o))
                acc = _dot(jnp.concatenate(probs, axis=1), vbd)
                out = acc[:, :LANES] / (acc[:, LANES:] + p_sink) * sg_ref[0, rows, lanes].astype(F32)
                o_ref[0, rows, lanes] = out.astype(o_ref.dtype)


def _attn_out_kernel(sink_ref, bounded_ref, q_ref, ka_ref, kb_ref, va_ref, vb_ref, sg_ref,
                     x_ref, f_ref, sgf_ref, w_ref, y_ref, ma_ref, *, n_blocks):
    _attn_kernel(sink_ref, bounded_ref, q_ref, ka_ref, kb_ref, va_ref, vb_ref, sg_ref, ma_ref, n_blocks=n_blocks)
    f = jnp.concatenate([_unpack_rows(f_ref[s, 0]) for s in range(FOURIER_WIDTH // LANES)], axis=1)
    y_ref[0] =(x_ref[0] + _dot(ma_ref[0], w_ref[:ATTN_WIDTH, :])
                + _dot(f * sgf_ref[0], w_ref[ATTN_WIDTH:, :]))


def _attention_out_proj(sink2, bounded, q, ka, kb, va, vb, sga, x, f, sgf, layer, w_bf):
    bsz, seq_len, _ = q.shape
    tq = ATTN_TILE
    tile = lambda width: pl.BlockSpec((1, tq, width), lambda b, i, s, fl: (b, i, 0))
    kv_spec = pl.BlockSpec((1, seq_len, LANES), lambda b, i, s, fl: (b, 0, 0))
    f_spec = pl.BlockSpec((FOURIER_WIDTH // LANES, 1, tq // 2, LANES), lambda b, i, s, fl: (0, b, i, 0))
    w_spec = pl.BlockSpec((None, D_MODEL, D_MODEL), lambda b, i, s, fl: (layer, 0, 0), pipeline_mode=pl.Buffered(1))
    grid_spec = pltpu.PrefetchScalarGridSpec(
        num_scalar_prefetch=2,
        grid=(bsz, seq_len // tq),
        in_specs=[tile(ATTN_WIDTH), kv_spec, kv_spec, kv_spec, kv_spec, tile(ATTN_WIDTH),
                  tile(D_MODEL), f_spec, tile(FOURIER_WIDTH), w_spec],
        out_specs=tile(D_MODEL),
        scratch_shapes=[pltpu.VMEM((1, tq, ATTN_WIDTH), BF16)],
    )
    return pl.pallas_call(
        functools.partial(_attn_out_kernel, n_blocks=seq_len // BLOCK),
        grid_spec=grid_spec,
        out_shape=jax.ShapeDtypeStruct((bsz, seq_len, D_MODEL), F32),
        compiler_params=_compiler_params(2),
        name="attention_out_proj",
    )(sink2, bounded, q, ka, kb, va, vb, sga, x, f, sgf, w_bf)


def _attention(sink2, bounded, q, ka, kb, va, vb, sga):
    bsz, seq_len, _ = q.shape
    tq = ATTN_TILE
    q_spec = pl.BlockSpec((1, tq, ATTN_WIDTH), lambda b, i, s, f: (b, i, 0))
    kv_spec = pl.BlockSpec((1, seq_len, LANES), lambda b, i, s, f: (b, 0, 0))
    grid_spec = pltpu.PrefetchScalarGridSpec(
        num_scalar_prefetch=2,
        grid=(bsz, seq_len // tq),
        in_specs=[q_spec, kv_spec, kv_spec, kv_spec, kv_spec, q_spec],
        out_specs=q_spec,
    )
    return pl.pallas_call(
        functools.partial(_attn_kernel, n_blocks=seq_len // BLOCK),
        grid_spec=grid_spec,
        out_shape=jax.ShapeDtypeStruct((bsz, seq_len, ATTN_WIDTH), BF16),
        compiler_params=_compiler_params(2),
        name="band_attention",
    )(sink2, bounded, q, ka, kb, va, vb, sga)


def _seq_dft_kernel(a_ref, b_ref, hi_ref, lo_ref, o_ref, m_ref, zz_ref, pair_ref, *, chunk):
    half = RADIX // 2
    root_half = math.sqrt(0.5)

    @pl.when((pl.program_id(0) == 0) & (pl.program_id(1) == 0))
    def _():
        for r in range(RADIX):
            c_lo, s_lo = lo_ref[0, r], lo_ref[1, r]
            for a in range(chunk // DFT_SPLIT):
                c_hi, s_hi = hi_ref[0, a:a + 1, :], hi_ref[1, a:a + 1, :]
                rows = slice(a * DFT_SPLIT, (a + 1) * DFT_SPLIT)
                m_ref[r, rows, :chunk] = (c_hi * c_lo - s_hi * s_lo).astype(m_ref.dtype)
                m_ref[r, rows, chunk:] = (-(s_hi * c_lo + c_hi * s_lo)).astype(m_ref.dtype)

    cadd = lambda u, v: (u[0] + v[0], u[1] + v[1])
    csub = lambda u, v: (u[0] - v[0], u[1] - v[1])
    add_i = lambda u, v: (u[0] - v[1], u[1] + v[0])
    sub_i = lambda u, v: (u[0] + v[1], u[1] - v[0])

    def four_point(w):
        s02, d02, s13, d13 = cadd(w[0], w[2]), csub(w[0], w[2]), cadd(w[1], w[3]), csub(w[1], w[3])
        return [cadd(s02, s13), add_i(d02, d13), csub(s02, s13), sub_i(d02, d13)]

    def butterflies(parity):
        for rb in range(chunk // DFT_BLOCK_ROWS):
            for cb in range(DFT_COLS // LANES):
                rows = lambda q: slice(q * chunk + rb * DFT_BLOCK_ROWS, q * chunk + (rb + 1) * DFT_BLOCK_ROWS)
                cols = slice(cb * LANES, (cb + 1) * LANES)
                z = [(_unpack_rows(a_ref[0, 0, _half_rows(rows(q)), cols]),
                      _unpack_rows(b_ref[0, 0, _half_rows(rows(q)), cols])) for q in range(RADIX)]
                if parity == 0:
                    w = [cadd(z[q], z[q + half]) for q in range(half)]
                else:
                    d = [csub(z[q], z[q + half]) for q in range(half)]
                    w = [d[0],
                         ((d[1][0] - d[1][1]) * root_half, (d[1][0] + d[1][1]) * root_half),
                         (-d[2][1], d[2][0]),
                         ((-d[3][0] - d[3][1]) * root_half, (d[3][0] - d[3][1]) * root_half)]
                for t, (re, im) in enumerate(four_point(w)):
                    zz_ref[2 * t + parity, rows(0), cols] = re
                    zz_ref[2 * t + parity, rows(1), cols] = im

    def project(r):
        f = _dot(m_ref[r], zz_ref[r])
        for cb in range(DFT_COLS // LANES):
            pair_ref[r // 2, cb, pl.ds(r % 2, chunk, stride=2), :] = f[:, cb * LANES:(cb + 1) * LANES]

    butterflies(0)
    for t in range(half):
        project(2 * t)
    butterflies(1)
    for t in range(half):
        project(2 * t + 1)
        for cb in range(DFT_COLS // LANES):
            o_ref[cb, 0, pl.ds(t, chunk, stride=half), :] = _pack_rows(pair_ref[t, cb])


def _dft_angle_tables(seq_len):
    chunk = seq_len // RADIX
    period = seq_len // (RADIX * DFT_SPLIT)
    m = jnp.arange(chunk, dtype=jnp.int32)[None, :]
    ang_hi = ((jnp.arange(chunk // DFT_SPLIT, dtype=jnp.int32)[:, None] * m) % period).astype(F32) * (2.0 * math.pi / period)
    ang_lo = ((jnp.arange(RADIX * DFT_SPLIT, dtype=jnp.int32)[:, None] * m) % seq_len).astype(F32) * (2.0 * math.pi / seq_len)
    ang_lo = ang_lo.reshape(DFT_SPLIT, RADIX, chunk).transpose(1, 0, 2)
    return jnp.stack([jnp.cos(ang_hi), jnp.sin(ang_hi)]), jnp.stack([jnp.cos(ang_lo), jnp.sin(ang_lo)])


def _seq_dft(a, b, dft_tables):
    n_halves, bsz, half_len, _ = a.shape
    seq_len = 2 * half_len
    chunk = seq_len // RADIX
    slabs_per_step = DFT_COLS // LANES
    col_spec = pl.BlockSpec((1, 1, half_len, DFT_COLS), lambda bi, h: (h, bi, 0, 0))
    out_spec = pl.BlockSpec((slabs_per_step, 1, half_len, LANES), lambda bi, h: (h, bi, 0, 0))
    hi_t, lo_t = dft_tables
    table_spec = lambda t: pl.BlockSpec(t.shape, lambda bi, h: (0,) * t.ndim, pipeline_mode=pl.Buffered(1))
    return pl.pallas_call(
        functools.partial(_seq_dft_kernel, chunk=chunk),
        grid=(bsz, n_halves),
        in_specs=[col_spec, col_spec, table_spec(hi_t), table_spec(lo_t)],
        out_specs=out_spec,
        out_shape=jax.ShapeDtypeStruct((n_halves * slabs_per_step, bsz, half_len, LANES), jnp.uint32),
        scratch_shapes=[pltpu.VMEM((RADIX, chunk, 2 * chunk), BF16),
                        pltpu.VMEM((RADIX, 2 * chunk, DFT_COLS), BF16),
                        pltpu.VMEM((RADIX // 2, slabs_per_step, 2 * chunk, LANES), F32)],
        compiler_params=_compiler_params(2),
        name="seq_dft",
    )(a, b, hi_t, lo_t)


def _out_proj_rows(rows, x_ref, ma_ref, f_ref, sgf_ref, w_ref, o_ref):
    f = jnp.concatenate([_unpack_rows(f_ref[s, _half_rows(rows), :])
                         for s in range(FOURIER_WIDTH // LANES)], axis=1)
    mf = f * sgf_ref[rows, :]
    o_ref[rows, :] = (x_ref[rows, :] + _dot(ma_ref[rows, :], w_ref[:ATTN_WIDTH, :])
                      + _dot(mf, w_ref[ATTN_WIDTH:, :]))


N_OUT_PROJ_INPUTS = 5


def _out_in_proj_kernel(*refs):
    out_in = refs[:N_OUT_PROJ_INPUTS]
    in_refs = refs[N_OUT_PROJ_INPUTS:N_OUT_PROJ_INPUTS + N_IN_PROJ_INPUTS]
    y_ref = refs[N_OUT_PROJ_INPUTS + N_IN_PROJ_INPUTS]
    out_refs = refs[N_OUT_PROJ_INPUTS + N_IN_PROJ_INPUTS + 1:]
    for sub in range(y_ref.shape[0] // FUSED_SUB_ROWS):
        rows = slice(sub * FUSED_SUB_ROWS, (sub + 1) * FUSED_SUB_ROWS)
        _out_proj_rows(rows, *out_in, y_ref)
        _in_proj_rows(y_ref[rows, :], rows, *in_refs, *out_refs)


def _out_proj_specs(tm, layer):
    return [_row_spec(tm, D_MODEL), _row_spec(tm, ATTN_WIDTH), _split_spec(tm, LANES),
            _row_spec(tm, FOURIER_WIDTH), _layer_weight_spec(layer, D_MODEL, D_MODEL)]


def _out_in_proj(x2d, ma, f, sgf, layer, w_bf, seq_len, in_w):
    rows = x2d.shape[0]
    tm = FUSED_ROW_TILE
    i_in_specs, i_out_specs, i_out_shape = _in_proj_specs(tm, rows, seq_len, layer + 1)
    outs = pl.pallas_call(
        _out_in_proj_kernel,
        grid=(rows // tm,),
        in_specs=_out_proj_specs(tm, layer) + i_in_specs,
        out_specs=[_row_spec(tm, D_MODEL)] + i_out_specs,
        out_shape=[jax.ShapeDtypeStruct((rows, D_MODEL), F32)] + i_out_shape,
        compiler_params=_compiler_params(1, FUSED_VMEM_LIMIT),
        name="out_in_proj",
    )(x2d, ma, f, sgf, w_bf, *in_w)
    return outs[0], outs[1:]


def _rope_tables(seq_len):
    half = HEAD_DIM // 2
    inv_freq = 1.0 / (ROPE_THETA ** (jnp.arange(half, dtype=F32) / half))
    ang = jnp.arange(seq_len, dtype=F32)[:, None] * inv_freq[None, :]
    cos = jnp.cos(ang)
    sin = jnp.sin(ang)
    cos_t = jnp.concatenate([cos, cos, cos, cos], axis=1)
    sin_t = jnp.concatenate([-sin, sin, -sin, sin], axis=1)
    return cos_t, sin_t


def _rope_gains(q_gain, k_gain):
    pair = lambda g: jnp.tile(g, LANES // HEAD_DIM)
    rot = lambda g: jnp.roll(g, HEAD_DIM // 2)
    gq = q_gain.astype(F32) * (HEAD_DIM ** -0.5 * LOG2E)
    gk = k_gain.astype(F32)
    return jnp.stack([pair(gq), pair(rot(gq)), pair(gk), pair(rot(gk))])


def _head_mean_matrix(width):
    head = jnp.arange(width, dtype=jnp.int32) // HEAD_DIM
    return jnp.where(head[:, None] == head[None, :], 1.0 / HEAD_DIM, 0.0).astype(BF16)


def _trunk(x, layers, dft_tables):
    bsz, seq_len, _ = x.shape
    x2d = x.reshape(bsz * seq_len, D_MODEL)
    r3 = lambda t: t.reshape(bsz, seq_len, t.shape[-1])
    split4 = lambda t: t.reshape(t.shape[0], bsz, seq_len // 2, DFT_COLS)
    proj = _in_proj(x2d, seq_len, 0, layers[0]["in_w"])
    for l, layer in enumerate(layers):
        q, ka, kb, va, vb, sga, a, b, sgf = proj
        f = _seq_dft(split4(a), split4(b), dft_tables)
        attn_args = (layer["sink2"], layer["bounded"], r3(q), r3(ka), r3(kb), r3(va), r3(vb), r3(sga))
        if l + 1 < len(layers):
            ma = _attention(*attn_args).reshape(bsz * seq_len, ATTN_WIDTH)
            x2d, proj = _out_in_proj(x2d, ma, f.reshape(f.shape[0], bsz * seq_len // 2, LANES), sgf,
                                     l, layer["w_out"], seq_len, layers[l + 1]["in_w"])
        else:
            return _attention_out_proj(*attn_args, r3(x2d), f, r3(sgf), l, layer["w_out"])


def kernel(x_prompt, x_sample, norm_gain, w_in, q_norm_gain, k_norm_gain, sink_logit, w_fourier, w_out):
    depth = norm_gain.shape[0]
    seq_p, seq_s = x_prompt.shape[1], x_sample.shape[1]
    assert seq_p == seq_s
    cos_t, sin_t = _rope_tables(seq_p)
    hs = _head_mean_matrix(2 * LANES)
    w_in_bf = w_in.astype(BF16)
    w_out_bf = w_out.astype(BF16)
    layers = []
    for l in range(depth):
        sink2 = sink_logit[l].astype(F32) * LOG2E
        logit_bound = (HEAD_DIM ** 0.5 * LOG2E) * jnp.max(jnp.abs(q_norm_gain[l])) * jnp.max(jnp.abs(k_norm_gain[l]))
        bounded = jnp.maximum(logit_bound, jnp.max(jnp.abs(sink2))) <= MAX_UNSHIFTED_LOGIT2
        in_w = (norm_gain[l].astype(F32)[None, :], w_in_bf, cos_t, sin_t,
                _rope_gains(q_norm_gain[l], k_norm_gain[l]), hs, _fourier_weights(w_fourier[l], seq_p))
        layers.append(dict(in_w=in_w, sink2=sink2, bounded=bounded.astype(jnp.int32)[None], w_out=w_out_bf))
    dft_tables = _dft_angle_tables(seq_p)
    return (_trunk(x_prompt, layers, dft_tables), _trunk(x_sample, layers, dft_tables))
```

```python
import functools
import math

import jax
import jax.numpy as jnp
from jax.experimental import pallas as pl
from jax.experimental.pallas import tpu as pltpu

D_MODEL = 1024
HEAD_DIM = 64
N_Q_HEADS = 8
N_KV_HEADS = 2
ATTN_WIDTH = N_Q_HEADS * HEAD_DIM
KV_WIDTH = N_KV_HEADS * HEAD_DIM
FOURIER_WIDTH = D_MODEL - ATTN_WIDTH
N_GROUPS = 4
GROUP_DIM = FOURIER_WIDTH // N_GROUPS
IN_WIDTH = 2 * ATTN_WIDTH + 2 * KV_WIDTH + 2 * FOURIER_WIDTH
Q_OFF = 0
K_OFF = ATTN_WIDTH
V_OFF = K_OFF + KV_WIDTH
GA_OFF = V_OFF + KV_WIDTH
U_OFF = GA_OFF + ATTN_WIDTH
GF_OFF = U_OFF + FOURIER_WIDTH
BLOCK = 128
ROPE_THETA = 10000.0
EPS = 1e-6
NEG = -1e30
LANES = 128
RADIX = 8
DFT_COLS = 256
DFT_BLOCK_ROWS = 128
DFT_SPLIT = 32
LOG2E = math.log2(math.e)
MAX_UNSHIFTED_LOGIT2 = 100.0

ROW_TILE = 1024
FUSED_ROW_TILE = 1024
FUSED_SUB_ROWS = 512
ATTN_TILE = 1024
VMEM_LIMIT = 48 * 1024 * 1024
FUSED_VMEM_LIMIT = 54 * 1024 * 1024

BF16 = jnp.bfloat16
F32 = jnp.float32


def _compiler_params(grid_rank, vmem_limit=VMEM_LIMIT):
    return pltpu.CompilerParams(dimension_semantics=("arbitrary",) * grid_rank, vmem_limit_bytes=vmem_limit)


def _dot(a, b):
    return jnp.dot(a, b, preferred_element_type=F32)


def _dot_narrow(a, b):
    half = a.shape[0] // 2
    return jnp.concatenate([_dot(a[:half], b), _dot(a[half:], b)], axis=0)


def _dot_nt(a, b):
    return jax.lax.dot_general(a, b, (((1,), (1,)), ((), ())), preferred_element_type=F32)


def _pack_rows(x):
    return pltpu.bitcast(x.astype(BF16), jnp.uint32)


def _unpack_rows(bits):
    return pltpu.bitcast(bits, BF16)


def _half_rows(rows):
    return rows if rows == slice(None) else slice(rows.start // 2, rows.stop // 2)


def _rotate_half(t, first_half):
    width = t.shape[-1]
    fwd = pltpu.roll(t, HEAD_DIM // 2, axis=1)
    bwd = pltpu.roll(t, width - HEAD_DIM // 2, axis=1)
    return jnp.where(first_half, bwd, fwd)


def _fourier_weight_kernel(cs_ref, w_ref, o_ref):
    for g in range(N_GROUPS):
        w = w_ref[g]
        c = jnp.dot(cs_ref[0], w, preferred_element_type=F32, precision=jax.lax.Precision.HIGHEST)
        s = jnp.dot(cs_ref[1], w, preferred_element_type=F32, precision=jax.lax.Precision.HIGHEST)
        o_ref[g] = jnp.concatenate([c, s], axis=1).astype(o_ref.dtype)


def _fourier_weights(w_four, seq_len):
    idx = (jnp.arange(GROUP_DIM, dtype=jnp.int32)[:, None] * jnp.arange(GROUP_DIM, dtype=jnp.int32)[None, :]) % GROUP_DIM
    ang = idx.astype(F32) * (2.0 * math.pi / GROUP_DIM)
    scale = 1.0 / math.sqrt(seq_len * GROUP_DIM)
    cs = jnp.stack([jnp.cos(ang), jnp.sin(ang)]) * scale
    return pl.pallas_call(
        _fourier_weight_kernel,
        out_shape=jax.ShapeDtypeStruct((N_GROUPS, GROUP_DIM, 2 * GROUP_DIM), BF16),
        name="fourier_weight_prep",
    )(cs, w_four)


def _in_proj_rows(x, rows, gn_ref, w_ref, cos_ref, sin_ref, gains_ref, hs_ref, wab_ref,
                  q_ref, ka_ref, kb_ref, va_ref, vb_ref, sga_ref, a_ref, b_ref, sgf_ref):
    sub_rows = x.shape[0]
    xb = (x * gn_ref[...]).astype(BF16)
    cos = cos_ref[rows, :]
    sin = sin_ref[rows, :]
    qcos, qsin = cos * gains_ref[0:1, :], sin * gains_ref[1:2, :]
    kcos, ksin = cos * gains_ref[2:3, :], sin * gains_ref[3:4, :]
    ms = jnp.mean(x * x, axis=-1, keepdims=True)
    r = jnp.broadcast_to(jax.lax.rsqrt(ms + EPS), (sub_rows, LANES))
    eps_z = jnp.broadcast_to(EPS * (ms + EPS), (sub_rows, LANES))
    wide = lambda t, width: jnp.concatenate([t] * (width // LANES), axis=1)

    lane = jax.lax.broadcasted_iota(jnp.int32, (1, LANES), 1)
    first_half_pair = (lane % HEAD_DIM) < (HEAD_DIM // 2)
    low_head = lane < HEAD_DIM
    hs = hs_ref[...]

    def head_mean_sq(z):
        return _dot_narrow((z * z).astype(BF16), hs)

    def silu_of_scaled(z, width):
        h = z * wide(0.5 * r, width)
        return h + h * jnp.tanh(h)

    zq = _dot(xb, w_ref[:, Q_OFF:Q_OFF + ATTN_WIDTH])
    half = ATTN_WIDTH // 2
    ssq = jnp.concatenate([head_mean_sq(zq[:, :half]), head_mean_sq(zq[:, half:])], axis=1)
    qr = (zq * wide(qcos, ATTN_WIDTH)
          + _rotate_half(zq, wide(first_half_pair, ATTN_WIDTH)) * wide(qsin, ATTN_WIDTH))
    q_ref[rows, :] = (qr * jax.lax.rsqrt(ssq + wide(eps_z, ATTN_WIDTH))).astype(q_ref.dtype)

    zkv = _dot_narrow(xb, w_ref[:, K_OFF:K_OFF + 2 * KV_WIDTH])
    zk = zkv[:, :KV_WIDTH]
    ssk = head_mean_sq(zkv)[:, :KV_WIDTH]
    kr = (zk * kcos + _rotate_half(zk, first_half_pair) * ksin) * jax.lax.rsqrt(ssk + eps_z)
    kr_sw = pltpu.roll(kr, HEAD_DIM, axis=1)
    ka_ref[rows, :] = jnp.where(low_head, kr, kr_sw).astype(ka_ref.dtype)
    kb_ref[rows, :] = jnp.where(low_head, kr_sw, kr).astype(kb_ref.dtype)
    zv = zkv[:, KV_WIDTH:] * r
    zv_sw = pltpu.roll(zv, HEAD_DIM, axis=1)
    va_ref[rows, :] = jnp.where(low_head, zv, zv_sw).astype(va_ref.dtype)
    vb_ref[rows, :] = jnp.where(low_head, zv_sw, zv).astype(vb_ref.dtype)

    zga = _dot(xb, w_ref[:, GA_OFF:GA_OFF + ATTN_WIDTH])
    sga_ref[rows, :] = silu_of_scaled(zga, ATTN_WIDTH).astype(sga_ref.dtype)
    zgf = _dot(xb, w_ref[:, GF_OFF:GF_OFF + FOURIER_WIDTH])
    sgf_ref[rows, :] = silu_of_scaled(zgf, FOURIER_WIDTH).astype(sgf_ref.dtype)

    zu = (_dot(xb, w_ref[:, U_OFF:U_OFF + FOURIER_WIDTH]) * wide(r, FOURIER_WIDTH)).astype(BF16)
    for g in range(N_GROUPS):
        ab = _dot_narrow(zu[:, g * GROUP_DIM:(g + 1) * GROUP_DIM], wab_ref[g])
        half_idx, col = divmod(g * GROUP_DIM, DFT_COLS)
        a_ref[half_idx, _half_rows(rows), col:col + GROUP_DIM] = _pack_rows(ab[:, :GROUP_DIM])
        b_ref[half_idx, _half_rows(rows), col:col + GROUP_DIM] = _pack_rows(ab[:, GROUP_DIM:])


N_IN_PROJ_INPUTS = 7


def _in_proj_kernel(x_ref, *refs):
    _in_proj_rows(x_ref[...], slice(None), *refs)


def _row_spec(tm, width):
    return pl.BlockSpec((tm, width), lambda i: (i, 0))


def _split_spec(tm, width=DFT_COLS):
    return pl.BlockSpec((FOURIER_WIDTH // width, tm // 2, width), lambda i: (0, i, 0))


def _resident_spec(shape):
    return pl.BlockSpec(shape, lambda i: (0,) * len(shape), pipeline_mode=pl.Buffered(1))


def _layer_weight_spec(layer, rows, cols):
    return pl.BlockSpec((None, rows, cols), lambda i: (layer, 0, 0), pipeline_mode=pl.Buffered(1))


def _in_proj_specs(tm, rows, seq_len, layer):
    steps_per_seq = seq_len // tm
    tab_spec = pl.BlockSpec((tm, LANES), lambda i: (i % steps_per_seq, 0))
    in_specs = [_resident_spec((1, D_MODEL)), _layer_weight_spec(layer, D_MODEL, IN_WIDTH), tab_spec, tab_spec,
                _resident_spec((4, LANES)), _resident_spec((2 * LANES, 2 * LANES)),
                _resident_spec((N_GROUPS, GROUP_DIM, 2 * GROUP_DIM))]
    row_widths = (ATTN_WIDTH, LANES, LANES, LANES, LANES, ATTN_WIDTH)
    out_specs = ([_row_spec(tm, w) for w in row_widths] + [_split_spec(tm)] * 2
                 + [_row_spec(tm, FOURIER_WIDTH)])
    out_shape = ([jax.ShapeDtypeStruct((rows, w), BF16) for w in row_widths]
                 + [jax.ShapeDtypeStruct((FOURIER_WIDTH // DFT_COLS, rows // 2, DFT_COLS), jnp.uint32)] * 2
                 + [jax.ShapeDtypeStruct((rows, FOURIER_WIDTH), BF16)])
    return in_specs, out_specs, out_shape


def _in_proj(x2d, seq_len, layer, in_w):
    rows = x2d.shape[0]
    tm = ROW_TILE
    in_specs, out_specs, out_shape = _in_proj_specs(tm, rows, seq_len, layer)
    return pl.pallas_call(
        _in_proj_kernel,
        grid=(rows // tm,),
        in_specs=[_row_spec(tm, D_MODEL)] + in_specs,
        out_specs=out_specs,
        out_shape=out_shape,
        compiler_params=_compiler_params(1),
        name="in_proj",
    )(x2d, *in_w)


def _attn_kernel(sink_ref, bounded_ref, q_ref, ka_ref, kb_ref, va_ref, vb_ref, sg_ref, o_ref, *, n_blocks):
    refs = (sink_ref, q_ref, ka_ref, kb_ref, va_ref, vb_ref, sg_ref, o_ref)

    @pl.when(bounded_ref[0] == 1)
    def _():
        _attn_body(*refs, n_blocks=n_blocks, shift=False)

    @pl.when(bounded_ref[0] == 0)
    def _():
        _attn_body(*refs, n_blocks=n_blocks, shift=True)


def _attn_body(sink_ref, q_ref, ka_ref, kb_ref, va_ref, vb_ref, sg_ref, o_ref, *, n_blocks, shift):
    i = pl.program_id(1)
    blocks_per_step = ATTN_TILE // BLOCK
    row = jax.lax.broadcasted_iota(jnp.int32, (BLOCK, 2 * BLOCK), 0)
    col = jax.lax.broadcasted_iota(jnp.int32, (BLOCK, 2 * BLOCK), 1) % BLOCK
    lane = jax.lax.broadcasted_iota(jnp.int32, (1, LANES), 1)
    low = lane < HEAD_DIM
    zero = jnp.zeros((), BF16)
    lane_full = jax.lax.broadcasted_iota(jnp.int32, (BLOCK, LANES), 1)
    ones_low = jnp.where(lane_full < HEAD_DIM, 1.0, 0.0).astype(BF16)
    ones_high = jnp.where(lane_full < HEAD_DIM, 0.0, 1.0).astype(BF16)

    for jb in range(blocks_per_step):
        ib = i * blocks_per_step + jb
        rows = slice(jb * BLOCK, (jb + 1) * BLOCK)
        edge_prev = jnp.where(ib == 0, NEG, 0.0)
        edge_next = jnp.where(ib == n_blocks - 1, NEG, 0.0)
        bias_prev = jnp.where(col >= row, 0.0, NEG) + edge_prev
        bias_next = jnp.where(col <= row, 0.0, NEG) + edge_next
        starts = [pl.multiple_of(jnp.clip(ib + c, 0, n_blocks - 1) * BLOCK, BLOCK) for c in (-1, 0, 1)]
        for kvh, (k_ref, v_ref) in enumerate(((ka_ref, va_ref), (kb_ref, vb_ref))):
            kbd, vbd = [], []
            for st in starts:
                kblk = k_ref[0, pl.ds(st, BLOCK), :]
                vblk = v_ref[0, pl.ds(st, BLOCK), :]
                kbd.append(jnp.concatenate([jnp.where(low, kblk, zero), jnp.where(low, zero, kblk)], axis=0))
                vbd.append(jnp.concatenate([
                    jnp.concatenate([jnp.where(low, vblk, zero), ones_low], axis=1),
                    jnp.concatenate([jnp.where(low, zero, vblk), ones_high], axis=1)], axis=0))
            vbd = jnp.concatenate(vbd, axis=0)
            for pair in range(2):
                pidx = kvh * 2 + pair
                lanes = slice(pidx * LANES, (pidx + 1) * LANES)
                qp = q_ref[0, rows, lanes]
                scores = [_dot_nt(qp, kb) for kb in kbd]
                sink_e = sink_ref[2 * pidx]
                sink_o = sink_ref[2 * pidx + 1]
                if shift:
                    scores = [scores[0] + bias_prev, scores[1], scores[2] + bias_next]
                    smax = jnp.maximum(jnp.maximum(scores[0], scores[1]), scores[2])
                    m_e = jnp.maximum(jnp.max(smax[:, :BLOCK], axis=-1, keepdims=True), sink_e)
                    m_o = jnp.maximum(jnp.max(smax[:, BLOCK:], axis=-1, keepdims=True), sink_o)
                    m_both = jnp.concatenate([jnp.broadcast_to(m_e, (BLOCK, BLOCK)),
                                              jnp.broadcast_to(m_o, (BLOCK, BLOCK))], axis=1)
                    probs = [jnp.exp2(s - m_both).astype(BF16) for s in scores]
                    p_sink = jnp.where(low, jnp.exp2(sink_e - m_e), jnp.exp2(sink_o - m_o))
                else:
                    scores = [s.astype(BF16) for s in scores]
                    scores = [scores[0] + bias_prev.astype(BF16), scores[1], scores[2] + bias_next.astype(BF16)]
                    probs = [jnp.exp2(s) for s in scores]
                    p_sink = jnp.exp2(jnp.where(low, sink_e, sink_o))
                acc = _dot(jnp.concatenate(probs, axis=1), vbd)
                out = acc[:, :LANES] / (acc[:, LANES:] + p_sink) * sg_ref[0, rows, lanes].astype(F32)
                o_ref[0, rows, lanes] = out.astype(o_ref.dtype)


def _attn_out_kernel(sink_ref, bounded_ref, q_ref, ka_ref, kb_ref, va_ref, vb_ref, sg_ref,
                     x_ref, f_ref, sgf_ref, w_ref, y_ref, ma_ref, *, n_blocks):
    _attn_kernel(sink_ref, bounded_ref, q_ref, ka_ref, kb_ref, va_ref, vb_ref, sg_ref, ma_ref, n_blocks=n_blocks)
    f = jnp.concatenate([_unpack_rows(f_ref[s, 0]) for s in range(FOURIER_WIDTH // LANES)], axis=1)
    y_ref[0] =(x_ref[0] + _dot(ma_ref[0], w_ref[:ATTN_WIDTH, :])
                + _dot(f * sgf_ref[0], w_ref[ATTN_WIDTH:, :]))


def _attention_out_proj(sink2, bounded, q, ka, kb, va, vb, sga, x, f, sgf, layer, w_bf):
    bsz, seq_len, _ = q.shape
    tq = ATTN_TILE
    tile = lambda width: pl.BlockSpec((1, tq, width), lambda b, i, s, fl: (b, i, 0))
    kv_spec = pl.BlockSpec((1, seq_len, LANES), lambda b, i, s, fl: (b, 0, 0))
    f_spec = pl.BlockSpec((FOURIER_WIDTH // LANES, 1, tq // 2, LANES), lambda b, i, s, fl: (0, b, i, 0))
    w_spec = pl.BlockSpec((None, D_MODEL, D_MODEL), lambda b, i, s, fl: (layer, 0, 0), pipeline_mode=pl.Buffered(1))
    grid_spec = pltpu.PrefetchScalarGridSpec(
        num_scalar_prefetch=2,
        grid=(bsz, seq_len // tq),
        in_specs=[tile(ATTN_WIDTH), kv_spec, kv_spec, kv_spec, kv_spec, tile(ATTN_WIDTH),
                  tile(D_MODEL), f_spec, tile(FOURIER_WIDTH), w_spec],
        out_specs=tile(D_MODEL),
        scratch_shapes=[pltpu.VMEM((1, tq, ATTN_WIDTH), BF16)],
    )
    return pl.pallas_call(
        functools.partial(_attn_out_kernel, n_blocks=seq_len // BLOCK),
        grid_spec=grid_spec,
        out_shape=jax.ShapeDtypeStruct((bsz, seq_len, D_MODEL), F32),
        compiler_params=_compiler_params(2),
        name="attention_out_proj",
    )(sink2, bounded, q, ka, kb, va, vb, sga, x, f, sgf, w_bf)


def _attention(sink2, bounded, q, ka, kb, va, vb, sga):
    bsz, seq_len, _ = q.shape
    tq = ATTN_TILE
    q_spec = pl.BlockSpec((1, tq, ATTN_WIDTH), lambda b, i, s, f: (b, i, 0))
    kv_spec = pl.BlockSpec((1, seq_len, LANES), lambda b, i, s, f: (b, 0, 0))
    grid_spec = pltpu.PrefetchScalarGridSpec(
        num_scalar_prefetch=2,
        grid=(bsz, seq_len // tq),
        in_specs=[q_spec, kv_spec, kv_spec, kv_spec, kv_spec, q_spec],
        out_specs=q_spec,
    )
    return pl.pallas_call(
        functools.partial(_attn_kernel, n_blocks=seq_len // BLOCK),
        grid_spec=grid_spec,
        out_shape=jax.ShapeDtypeStruct((bsz, seq_len, ATTN_WIDTH), BF16),
        compiler_params=_compiler_params(2),
        name="band_attention",
    )(sink2, bounded, q, ka, kb, va, vb, sga)


def _seq_dft_kernel(a_ref, b_ref, hi_ref, lo_ref, o_ref, m_ref, zz_ref, pair_ref, *, chunk):
    half = RADIX // 2
    root_half = math.sqrt(0.5)

    @pl.when((pl.program_id(0) == 0) & (pl.program_id(1) == 0))
    def _():
        for r in range(RADIX):
            c_lo, s_lo = lo_ref[0, r], lo_ref[1, r]
            for a in range(chunk // DFT_SPLIT):
                c_hi, s_hi = hi_ref[0, a:a + 1, :], hi_ref[1, a:a + 1, :]
                rows = slice(a * DFT_SPLIT, (a + 1) * DFT_SPLIT)
                m_ref[r, rows, :chunk] = (c_hi * c_lo - s_hi * s_lo).astype(m_ref.dtype)
                m_ref[r, rows, chunk:] = (-(s_hi * c_lo + c_hi * s_lo)).astype(m_ref.dtype)

    cadd = lambda u, v: (u[0] + v[0], u[1] + v[1])
    csub = lambda u, v: (u[0] - v[0], u[1] - v[1])
    add_i = lambda u, v: (u[0] - v[1], u[1] + v[0])
    sub_i = lambda u, v: (u[0] + v[1], u[1] - v[0])

    def four_point(w):
        s02, d02, s13, d13 = cadd(w[0], w[2]), csub(w[0], w[2]), cadd(w[1], w[3]), csub(w[1], w[3])
        return [cadd(s02, s13), add_i(d02, d13), csub(s02, s13), sub_i(d02, d13)]

    def butterflies(h, parity):
        for rb in range(chunk // DFT_BLOCK_ROWS):
            for cb in range(DFT_COLS // LANES):
                rows = lambda q: slice(q * chunk + rb * DFT_BLOCK_ROWS, q * chunk + (rb + 1) * DFT_BLOCK_ROWS)
                cols = slice(cb * LANES, (cb + 1) * LANES)
                z = [(_unpack_rows(a_ref[h, 0, _half_rows(rows(q)), cols]),
                      _unpack_rows(b_ref[h, 0, _half_rows(rows(q)), cols])) for q in range(RADIX)]
                if parity == 0:
                    w = [cadd(z[q], z[q + half]) for q in range(half)]
                else:
                    d = [csub(z[q], z[q + half]) for q in range(half)]
                    w = [d[0],
                         ((d[1][0] - d[1][1]) * root_half, (d[1][0] + d[1][1]) * root_half),
                         (-d[2][1], d[2][0]),
                         ((-d[3][0] - d[3][1]) * root_half, (d[3][0] - d[3][1]) * root_half)]
                for t, (re, im) in enumerate(four_point(w)):
                    zz_ref[h, 2 * t + parity, rows(0), cols] = re
                    zz_ref[h, 2 * t + parity, rows(1), cols] = im

    slabs = DFT_COLS // LANES

    def project(h, r):
        f = _dot(m_ref[r], zz_ref[h, r])
        for cb in range(slabs):
            pair_ref[h, r // 2, cb, pl.ds(r % 2, chunk, stride=2), :] = f[:, cb * LANES:(cb + 1) * LANES]

    for h in range(a_ref.shape[0]):
        butterflies(h, 0)
        for t in range(half):
            project(h, 2 * t)
        butterflies(h, 1)
        for t in range(half):
            project(h, 2 * t + 1)
            for cb in range(slabs):
                o_ref[h * slabs + cb, 0, pl.ds(t, chunk, stride=half), :] = _pack_rows(pair_ref[h, t, cb])


def _dft_angle_tables(seq_len):
    chunk = seq_len // RADIX
    period = seq_len // (RADIX * DFT_SPLIT)
    m = jnp.arange(chunk, dtype=jnp.int32)[None, :]
    ang_hi = ((jnp.arange(chunk // DFT_SPLIT, dtype=jnp.int32)[:, None] * m) % period).astype(F32) * (2.0 * math.pi / period)
    ang_lo = ((jnp.arange(RADIX * DFT_SPLIT, dtype=jnp.int32)[:, None] * m) % seq_len).astype(F32) * (2.0 * math.pi / seq_len)
    ang_lo = ang_lo.reshape(DFT_SPLIT, RADIX, chunk).transpose(1, 0, 2)
    return jnp.stack([jnp.cos(ang_hi), jnp.sin(ang_hi)]), jnp.stack([jnp.cos(ang_lo), jnp.sin(ang_lo)])


def _seq_dft(a, b, dft_tables):
    n_halves, bsz, half_len, _ = a.shape
    seq_len = 2 * half_len
    chunk = seq_len // RADIX
    slabs_per_half = DFT_COLS // LANES
    n_slabs = n_halves * slabs_per_half
    col_spec = pl.BlockSpec((n_halves, 1, half_len, DFT_COLS), lambda bi, h: (0, bi, 0, 0))
    out_spec = pl.BlockSpec((n_slabs, 1, half_len, LANES), lambda bi, h: (0, bi, 0, 0))
    hi_t, lo_t = dft_tables
    table_spec = lambda t: pl.BlockSpec(t.shape, lambda bi, h: (0,) * t.ndim, pipeline_mode=pl.Buffered(1))
    return pl.pallas_call(
        functools.partial(_seq_dft_kernel, chunk=chunk),
        grid=(bsz, 1),
        in_specs=[col_spec, col_spec, table_spec(hi_t), table_spec(lo_t)],
        out_specs=out_spec,
        out_shape=jax.ShapeDtypeStruct((n_slabs, bsz, half_len, LANES), jnp.uint32),
        scratch_shapes=[pltpu.VMEM((RADIX, chunk, 2 * chunk), BF16),
                        pltpu.VMEM((n_halves, RADIX, 2 * chunk, DFT_COLS), BF16),
                        pltpu.VMEM((n_halves, RADIX // 2, slabs_per_half, 2 * chunk, LANES), F32)],
        compiler_params=_compiler_params(2, FUSED_VMEM_LIMIT),
        name="seq_dft",
    )(a, b, hi_t, lo_t)


def _out_proj_rows(rows, x_ref, ma_ref, f_ref, sgf_ref, w_ref, o_ref):
    f = jnp.concatenate([_unpack_rows(f_ref[s, _half_rows(rows), :])
                         for s in range(FOURIER_WIDTH // LANES)], axis=1)
    mf = f * sgf_ref[rows, :]
    o_ref[rows, :] = (x_ref[rows, :] + _dot(ma_ref[rows, :], w_ref[:ATTN_WIDTH, :])
                      + _dot(mf, w_ref[ATTN_WIDTH:, :]))


N_OUT_PROJ_INPUTS = 5


def _out_in_proj_kernel(*refs):
    out_in = refs[:N_OUT_PROJ_INPUTS]
    in_refs = refs[N_OUT_PROJ_INPUTS:N_OUT_PROJ_INPUTS + N_IN_PROJ_INPUTS]
    y_ref = refs[N_OUT_PROJ_INPUTS + N_IN_PROJ_INPUTS]
    out_refs = refs[N_OUT_PROJ_INPUTS + N_IN_PROJ_INPUTS + 1:]
    for sub in range(y_ref.shape[0] // FUSED_SUB_ROWS):
        rows = slice(sub * FUSED_SUB_ROWS, (sub + 1) * FUSED_SUB_ROWS)
        _out_proj_rows(rows, *out_in, y_ref)
        _in_proj_rows(y_ref[rows, :], rows, *in_refs, *out_refs)


def _out_proj_specs(tm, layer):
    return [_row_spec(tm, D_MODEL), _row_spec(tm, ATTN_WIDTH), _split_spec(tm, LANES),
            _row_spec(tm, FOURIER_WIDTH), _layer_weight_spec(layer, D_MODEL, D_MODEL)]


def _out_in_proj(x2d, ma, f, sgf, layer, w_bf, seq_len, in_w):
    rows = x2d.shape[0]
    tm = FUSED_ROW_TILE
    i_in_specs, i_out_specs, i_out_shape = _in_proj_specs(tm, rows, seq_len, layer + 1)
    outs = pl.pallas_call(
        _out_in_proj_kernel,
        grid=(rows // tm,),
        in_specs=_out_proj_specs(tm, layer) + i_in_specs,
        out_specs=[_row_spec(tm, D_MODEL)] + i_out_specs,
        out_shape=[jax.ShapeDtypeStruct((rows, D_MODEL), F32)] + i_out_shape,
        compiler_params=_compiler_params(1, FUSED_VMEM_LIMIT),
        name="out_in_proj",
    )(x2d, ma, f, sgf, w_bf, *in_w)
    return outs[0], outs[1:]


def _rope_tables(seq_len):
    half = HEAD_DIM // 2
    inv_freq = 1.0 / (ROPE_THETA ** (jnp.arange(half, dtype=F32) / half))
    ang = jnp.arange(seq_len, dtype=F32)[:, None] * inv_freq[None, :]
    cos = jnp.cos(ang)
    sin = jnp.sin(ang)
    cos_t = jnp.concatenate([cos, cos, cos, cos], axis=1)
    sin_t = jnp.concatenate([-sin, sin, -sin, sin], axis=1)
    return cos_t, sin_t


def _rope_gains(q_gain, k_gain):
    pair = lambda g: jnp.tile(g, LANES // HEAD_DIM)
    rot = lambda g: jnp.roll(g, HEAD_DIM // 2)
    gq = q_gain.astype(F32) * (HEAD_DIM ** -0.5 * LOG2E)
    gk = k_gain.astype(F32)
    return jnp.stack([pair(gq), pair(rot(gq)), pair(gk), pair(rot(gk))])


def _head_mean_matrix(width):
    head = jnp.arange(width, dtype=jnp.int32) // HEAD_DIM
    return jnp.where(head[:, None] == head[None, :], 1.0 / HEAD_DIM, 0.0).astype(BF16)


def _trunk(x, layers, dft_tables):
    bsz, seq_len, _ = x.shape
    x2d = x.reshape(bsz * seq_len, D_MODEL)
    r3 = lambda t: t.reshape(bsz, seq_len, t.shape[-1])
    split4 = lambda t: t.reshape(t.shape[0], bsz, seq_len // 2, DFT_COLS)
    proj = _in_proj(x2d, seq_len, 0, layers[0]["in_w"])
    for l, layer in enumerate(layers):
        q, ka, kb, va, vb, sga, a, b, sgf = proj
        f = _seq_dft(split4(a), split4(b), dft_tables)
        attn_args = (layer["sink2"], layer["bounded"], r3(q), r3(ka), r3(kb), r3(va), r3(vb), r3(sga))
        if l + 1 < len(layers):
            ma = _attention(*attn_args).reshape(bsz * seq_len, ATTN_WIDTH)
            x2d, proj = _out_in_proj(x2d, ma, f.reshape(f.shape[0], bsz * seq_len // 2, LANES), sgf,
                                     l, layer["w_out"], seq_len, layers[l + 1]["in_w"])
        else:
            return _attention_out_proj(*attn_args, r3(x2d), f, r3(sgf), l, layer["w_out"])


def kernel(x_prompt, x_sample, norm_gain, w_in, q_norm_gain, k_norm_gain, sink_logit, w_fourier, w_out):
    depth = norm_gain.shape[0]
    seq_p, seq_s = x_prompt.shape[1], x_sample.shape[1]
    assert seq_p == seq_s
    cos_t, sin_t = _rope_tables(seq_p)
    hs = _head_mean_matrix(2 * LANES)
    w_in_bf = w_in.astype(BF16)
    w_out_bf = w_out.astype(BF16)
    layers = []
    for l in range(depth):
        sink2 = sink_logit[l].astype(F32) * LOG2E
        logit_bound = (HEAD_DIM ** 0.5 * LOG2E) * jnp.max(jnp.abs(q_norm_gain[l])) * jnp.max(jnp.abs(k_norm_gain[l]))
        bounded = jnp.maximum(logit_bound, jnp.max(jnp.abs(sink2))) <= MAX_UNSHIFTED_LOGIT2
        in_w = (norm_gain[l].astype(F32)[None, :], w_in_bf, cos_t, sin_t,
                _rope_gains(q_norm_gain[l], k_norm_gain[l]), hs, _fourier_weights(w_fourier[l], seq_p))
        layers.append(dict(in_w=in_w, sink2=sink2, bounded=bounded.astype(jnp.int32)[None], w_out=w_out_bf))
    dft_tables = _dft_angle_tables(seq_p)
    return (_trunk(x_prompt, layers, dft_tables), _trunk(x_sample, layers, dft_tables))
```
